```python
import jax, jax.numpy as jnp
from jax import lax
import numpy as np

D_MODEL = 1024
BATCH = 8
SEQ = 2048
DEPTH = 2
DEC_BATCH = 128
DEC_SEQ = 8
PAST_LEN = 16384
PAGE_SIZE = 128

N_MIXERS = 2
N_RET_LAYERS = (DEPTH + 1) // 2
N_POOL_LAYERS = DEPTH // 2
RET_HEADS = 4
RET_DK = D_MODEL // RET_HEADS
RET_DV = 2 * D_MODEL // RET_HEADS
RET_CHUNK = 128
ROPE_BASE = 10000.0
POOL_WINDOWS = (2, 4, 8, 16)
POOL_GROUPS = len(POOL_WINDOWS)
POOL_GC = D_MODEL // POOL_GROUPS
POOL_BUF = max(POOL_WINDOWS) - 1
N_MEM = 256
X_HEADS = 4
X_HEAD_DIM = D_MODEL // X_HEADS
D_FF = 2816
CONV_W = 3
NORM_EPS = 1e-6
GN_EPS = 1e-5

kernel_name = 'retention_pool_hybrid_decode_step'

F32 = jnp.float32


def rmsnorm(x, g):
    xf = x.astype(F32)
    y = xf * lax.rsqrt(jnp.mean(xf * xf, axis=-1, keepdims=True) + NORM_EPS)
    return (y * g.astype(F32)).astype(x.dtype)


def rope(x, pos):
    d = x.shape[-1]
    inv = 1.0 / (ROPE_BASE ** (jnp.arange(0, d, 2, dtype=F32) / d))
    ang = pos[:, None] * inv[None, :]
    cos = jnp.cos(ang)[None, :, None, :]
    sin = jnp.sin(ang)[None, :, None, :]
    xf = x.astype(F32)
    x1, x2 = xf[..., : d // 2], xf[..., d // 2:]
    return jnp.concatenate([x1 * cos - x2 * sin, x2 * cos + x1 * sin], axis=-1)


def log_decay():
    return jnp.log(1.0 - 2.0 ** (-5.0 - jnp.arange(RET_HEADS, dtype=F32)))


def retention_chunks(q, k, v, s0, chunk):
    B, L, H, dk = q.shape
    dv = v.shape[-1]
    n = L // chunk
    lg = log_decay()
    idx = jnp.arange(chunk, dtype=F32)
    rel = idx[:, None] - idx[None, :]
    inner = jnp.where(rel[None] >= 0, jnp.exp(jnp.maximum(rel, 0.0)[None] * lg[:, None, None]), 0.0)
    q_dec = jnp.exp((idx + 1.0)[None, :] * lg[:, None])[..., None]
    k_dec = jnp.exp((chunk - 1.0 - idx)[None, :] * lg[:, None])[..., None]
    c_dec = jnp.exp(chunk * lg)[:, None, None]

    def blocks(t):
        return t.reshape(B, n, chunk, H, t.shape[-1]).transpose(1, 0, 3, 2, 4)

    def step(s, inp):
        qc, kc, vc = inp
        scores = jnp.einsum('bhqd,bhkd->bhqk', qc, kc) * inner
        o = jnp.einsum('bhqk,bhkv->bhqv', scores, vc) + jnp.einsum('bhqd,bhdv->bhqv', qc * q_dec, s)
        s = s * c_dec + jnp.einsum('bhkd,bhkv->bhdv', kc * k_dec, vc)
        return s, o

    s, o = lax.scan(step, s0, (blocks(q), blocks(k), blocks(v)))
    o = o.transpose(1, 0, 3, 2, 4).reshape(B, L, H, dv)
    return o, s


def retention_mixer(h, pos, s0, w_in, gn_g, w_out):
    B, L, _ = h.shape
    hk = RET_HEADS * RET_DK
    hv = RET_HEADS * RET_DV
    proj = h @ w_in
    q = rope(proj[..., :hk].reshape(B, L, RET_HEADS, RET_DK), pos)
    k = rope(proj[..., hk:2 * hk].reshape(B, L, RET_HEADS, RET_DK), pos) * (RET_DK ** -0.5)
    v = proj[..., 2 * hk:2 * hk + hv].reshape(B, L, RET_HEADS, RET_DV).astype(F32)
    g = proj[..., 2 * hk + hv:].astype(F32)
    chunk = min(RET_CHUNK, L)
    o, s = retention_chunks(q, k, v, s0.astype(F32), chunk)
    mu = jnp.mean(o, axis=-1, keepdims=True)
    var = jnp.mean(jnp.square(o - mu), axis=-1, keepdims=True)
    o = ((o - mu) * lax.rsqrt(var + GN_EPS)).reshape(B, L, hv) * gn_g.astype(F32)
    out = (jax.nn.silu(g) * o).astype(h.dtype) @ w_out
    return out, s


def pool_mixer(h, pos, buf, w_group, scale):
    B, L, D = h.shape
    P = POOL_BUF
    ext = jnp.concatenate([buf.astype(h.dtype), h], axis=1)
    ef = ext.astype(F32)
    cs = jnp.concatenate([jnp.zeros((B, 1, D), F32), jnp.cumsum(ef, axis=1)], axis=1)
    outs = []
    for gi, w in enumerate(POOL_WINDOWS):
        sl = slice(gi * POOL_GC, (gi + 1) * POOL_GC)
        win_sum = cs[:, P + 1:P + L + 1, sl] - cs[:, P + 1 - w:P + L + 1 - w, sl]
        cnt = jnp.minimum(pos + 1.0, float(w))
        outs.append(win_sum / cnt[None, :, None])
    pooled = jnp.concatenate(outs, axis=-1) - h.astype(F32)
    mixed = jnp.einsum('blgc,gcd->blgd', pooled.reshape(B, L, POOL_GROUPS, POOL_GC), w_group.astype(F32))
    out = mixed.reshape(B, L, D) * scale.astype(F32)
    return out.astype(h.dtype), ext[:, -P:]


def mem_kv(mem, g, wk, wv):
    B, M, _ = mem.shape
    hm = rmsnorm(mem, g)
    k = (hm @ wk).reshape(B, M, X_HEADS, X_HEAD_DIM)
    v = (hm @ wv).reshape(B, M, X_HEADS, X_HEAD_DIM)
    return k, v


def cross_attn(h, k, v, wq, wo):
    B, L, _ = h.shape
    q = (h @ wq).reshape(B, L, X_HEADS, X_HEAD_DIM)
    s = jnp.einsum('blhd,bmhd->bhlm', q.astype(F32), k.astype(F32)) * (X_HEAD_DIM ** -0.5)
    p = jax.nn.softmax(s, axis=-1)
    o = jnp.einsum('bhlm,bmhd->blhd', p, v.astype(F32)).reshape(B, L, D_MODEL)
    return o.astype(h.dtype) @ wo


def conv_ffn(h, buf, w_up, cw, cb, w_down):
    L = h.shape[1]
    u = h @ w_up
    ext = jnp.concatenate([buf.astype(u.dtype), u], axis=1)
    c = cb
    for j in range(CONV_W):
        c = c + cw[j] * ext[:, j:j + L]
    a, gate = c[..., :D_FF], c[..., D_FF:]
    y = (a * jax.nn.silu(gate)) @ w_down
    return y, ext[:, -(CONV_W - 1):]


def trunk(x, pos, ret_s0, pool_b0, conv_b0, mem_k, mem_v, w_ret_in, ret_gn, w_ret_out, pool_w, pool_scale,
          w_xq, w_xo, w_up, conv_w, conv_b, w_down, norm_mix, norm_xattn, norm_ffn, norm_final):
    new_ret, new_pool, new_conv = [], [], []
    for i in range(DEPTH):
        j = i // N_MIXERS
        h = rmsnorm(x, norm_mix[i])
        if i % N_MIXERS == 0:
            out, s = retention_mixer(h, pos, ret_s0[j], w_ret_in[j], ret_gn[j], w_ret_out[j])
            new_ret.append(s)
        else:
            out, b = pool_mixer(h, pos, pool_b0[j], pool_w[j], pool_scale[j])
            new_pool.append(b)
        x = x + out
        x = x + cross_attn(rmsnorm(x, norm_xattn[i]), mem_k[i], mem_v[i], w_xq[i], w_xo[i])
        y, cbuf = conv_ffn(rmsnorm(x, norm_ffn[i]), conv_b0[i], w_up[i], conv_w[i], conv_b[i], w_down[i])
        new_conv.append(cbuf)
        x = x + y
    return rmsnorm(x, norm_final), jnp.stack(new_ret), jnp.stack(new_pool), jnp.stack(new_conv)


def setup_inputs(seed: int = 0) -> dict:
    key = jax.random.key(seed)
    ks = jax.random.split(key, 32)
    nrm = lambda k, shape, s: jax.random.normal(k, shape, F32) * s
    hk = RET_HEADS * RET_DK
    hv = RET_HEADS * RET_DV
    return {
        'x_prompt': nrm(ks[0], (BATCH, SEQ, D_MODEL), 1.0),
        'x_sample': nrm(ks[1], (DEC_BATCH, DEC_SEQ, D_MODEL), 1.0),
        'mem_prompt': nrm(ks[2], (BATCH, N_MEM, D_MODEL), 1.0),
        'cache_mem_k': nrm(ks[3], (DEPTH, DEC_BATCH, N_MEM, X_HEADS, X_HEAD_DIM), 1.0),
        'cache_mem_v': nrm(ks[4], (DEPTH, DEC_BATCH, N_MEM, X_HEADS, X_HEAD_DIM), 1.0),
        'state_ret': nrm(ks[5], (N_RET_LAYERS, DEC_BATCH, RET_HEADS, RET_DK, RET_DV), 0.1),
        'cache_pool': nrm(ks[6], (N_POOL_LAYERS, DEC_BATCH, POOL_BUF, D_MODEL), 1.0),
        'cache_ffn_conv': nrm(ks[7], (DEPTH, DEC_BATCH, CONV_W - 1, 2 * D_FF), 1.0),
        'w_ret_in': nrm(ks[8], (N_RET_LAYERS, D_MODEL, 2 * hk + 2 * hv), D_MODEL ** -0.5),
        'ret_gn': 1.0 + nrm(ks[9], (N_RET_LAYERS, hv), 0.02),
        'w_ret_out': nrm(ks[10], (N_RET_LAYERS, hv, D_MODEL), hv ** -0.5),
        'pool_w': nrm(ks[11], (N_POOL_LAYERS, POOL_GROUPS, POOL_GC, POOL_GC), POOL_GC ** -0.5),
        'pool_scale': 1.0 + nrm(ks[12], (N_POOL_LAYERS, D_MODEL), 0.02),
        'norm_mem': 1.0 + nrm(ks[13], (DEPTH, D_MODEL), 0.02),
        'w_xq': nrm(ks[14], (DEPTH, D_MODEL, D_MODEL), D_MODEL ** -0.5),
        'w_xk': nrm(ks[15], (DEPTH, D_MODEL, D_MODEL), D_MODEL ** -0.5),
        'w_xv': nrm(ks[16], (DEPTH, D_MODEL, D_MODEL), D_MODEL ** -0.5),
        'w_xo': nrm(ks[17], (DEPTH, D_MODEL, D_MODEL), D_MODEL ** -0.5),
        'w_up': nrm(ks[18], (DEPTH, D_MODEL, 2 * D_FF), D_MODEL ** -0.5),
        'conv_w': nrm(ks[19], (DEPTH, CONV_W, 2 * D_FF), CONV_W ** -0.5),
        'conv_b': nrm(ks[20], (DEPTH, 2 * D_FF), 0.02),
        'w_down': nrm(ks[21], (DEPTH, D_FF, D_MODEL), D_FF ** -0.5),
        'norm_mix': 1.0 + nrm(ks[22], (DEPTH, D_MODEL), 0.02),
        'norm_xattn': 1.0 + nrm(ks[23], (DEPTH, D_MODEL), 0.02),
        'norm_ffn': 1.0 + nrm(ks[24], (DEPTH, D_MODEL), 0.02),
        'norm_final': 1.0 + nrm(ks[25], (D_MODEL,), 0.02),
    }


def reference(x_prompt, x_sample, mem_prompt, cache_mem_k, cache_mem_v, state_ret, cache_pool, cache_ffn_conv,
              w_ret_in, ret_gn, w_ret_out, pool_w, pool_scale, norm_mem, w_xq, w_xk, w_xv, w_xo,
              w_up, conv_w, conv_b, w_down, norm_mix, norm_xattn, norm_ffn, norm_final):
    B = x_prompt.shape[0]
    pos_p = jnp.arange(x_prompt.shape[1], dtype=F32)
    pos_s = PAST_LEN + jnp.arange(x_sample.shape[1], dtype=F32)

    kvs = [mem_kv(mem_prompt, norm_mem[i], w_xk[i], w_xv[i]) for i in range(DEPTH)]
    mem_k_p = jnp.stack([kv[0] for kv in kvs])
    mem_v_p = jnp.stack([kv[1] for kv in kvs])
    ret0 = jnp.zeros((N_RET_LAYERS, B, RET_HEADS, RET_DK, RET_DV), F32)
    pool0 = jnp.zeros((N_POOL_LAYERS, B, POOL_BUF, D_MODEL), x_prompt.dtype)
    conv0 = jnp.zeros((DEPTH, B, CONV_W - 1, 2 * D_FF), x_prompt.dtype)
    y_prompt, ret_p, pool_p, conv_p = trunk(
        x_prompt, pos_p, ret0, pool0, conv0, mem_k_p, mem_v_p, w_ret_in, ret_gn, w_ret_out, pool_w, pool_scale,
        w_xq, w_xo, w_up, conv_w, conv_b, w_down, norm_mix, norm_xattn, norm_ffn, norm_final)

    y_sample, ret_s, pool_s, conv_s = trunk(
        x_sample, pos_s, state_ret, cache_pool, cache_ffn_conv, cache_mem_k, cache_mem_v, w_ret_in, ret_gn,
        w_ret_out, pool_w, pool_scale, w_xq, w_xo, w_up, conv_w, conv_b, w_down, norm_mix, norm_xattn,
        norm_ffn, norm_final)

    return (y_prompt, y_sample, ret_p, ret_s.astype(state_ret.dtype), pool_p, pool_s, conv_p, conv_s, mem_k_p, mem_v_p)
```

```python
import functools

import jax
import jax.numpy as jnp
from jax import lax
from jax.experimental import pallas as pl
from jax.experimental.pallas import tpu as pltpu

F32 = jnp.float32
BF16 = jnp.bfloat16

D_MODEL = 1024
PAST_LEN = 16384
RET_HEADS = 4
RET_DK = D_MODEL // RET_HEADS
RET_DV = 2 * D_MODEL // RET_HEADS
HK = RET_HEADS * RET_DK
HV = RET_HEADS * RET_DV
ROPE_BASE = 10000.0
POOL_WINDOWS = (2, 4, 8, 16)
POOL_GC = D_MODEL // len(POOL_WINDOWS)
POOL_BUF = max(POOL_WINDOWS) - 1
X_HEADS = 4
X_HEAD_DIM = D_MODEL // X_HEADS
D_FF = 2816
CONV_W = 3
NORM_EPS = 1e-6
GN_EPS = 1e-5

SUBLANES = 8
ROW_TILE = 512
RET_CHUNK_PROMPT = 256
FF_CHUNK = 256
NF = D_FF // FF_CHUNK
POOL_CARRY = 16
VMEM_LIMIT = 56 * 1024 * 1024


def _params(sem):
    return pltpu.CompilerParams(dimension_semantics=sem, vmem_limit_bytes=VMEM_LIMIT)


def _resident(shape):
    zeros = (0,) * len(shape)
    return pl.BlockSpec(shape, lambda *_: zeros, pipeline_mode=pl.Buffered(1))


def _dot(a, b):
    return jnp.dot(a, b, preferred_element_type=F32)


def _dot_nt(a, b):
    return lax.dot_general(a, b, (((1,), (1,)), ((), ())), preferred_element_type=F32)


def _dot_tn(a, b):
    return lax.dot_general(a, b, (((0,), (0,)), ((), ())), preferred_element_type=F32)


def _rms(x, g):
    return x * lax.rsqrt(jnp.mean(x * x, axis=-1, keepdims=True) + NORM_EPS) * g


def _silu(x):
    return x * (1.0 / (1.0 + jnp.exp(-x)))


def _memkv_body(mem_ref, g_ref, wk_ref, wv_ref, k_ref, v_ref):
    h = _rms(mem_ref[...], g_ref[0]).astype(BF16)
    k_ref[0] = _dot(h, wk_ref[0])
    v_ref[0] = _dot(h, wv_ref[0])


def _mem_kv(mem2d, norm_mem, wk, wv):
    rows, d = mem2d.shape
    depth = wk.shape[0]
    tile = min(rows, 1024)
    w_spec = pl.BlockSpec((1, d, d), lambda i, r: (i, 0, 0))
    o_spec = pl.BlockSpec((1, tile, d), lambda i, r: (i, r, 0))
    return pl.pallas_call(
        _memkv_body,
        grid=(depth, rows // tile),
        in_specs=[pl.BlockSpec((tile, d), lambda i, r: (r, 0)),
                  pl.BlockSpec((1, 1, d), lambda i, r: (i, 0, 0)), w_spec, w_spec],
        out_specs=[o_spec, o_spec],
        out_shape=[jax.ShapeDtypeStruct((depth, rows, d), F32)] * 2,
        compiler_params=_params(("arbitrary", "arbitrary")),
        name="mem_kv",
    )(mem2d, norm_mem.reshape(depth, 1, d), wk, wv)


def _retproj_body(x_ref, g_ref, cos_ref, sin_ref, w_ref, q_ref, k_ref, v_ref, gate_ref):
    h = _rms(x_ref[...], g_ref[...]).astype(BF16)
    cos = cos_ref[...]
    sin = sin_ref[...]
    half = RET_DK // 2

    def rope_store(col0, out_ref, scale):
        p = _dot(h, w_ref[:, col0:col0 + HK])
        for hd in range(RET_HEADS):
            a = p[:, hd * RET_DK: hd * RET_DK + half]
            b = p[:, hd * RET_DK + half: (hd + 1) * RET_DK]
            out_ref[:, hd * RET_DK: hd * RET_DK + half] = ((a * cos - b * sin) * scale).astype(out_ref.dtype)
            out_ref[:, hd * RET_DK + half: (hd + 1) * RET_DK] = ((b * cos + a * sin) * scale).astype(out_ref.dtype)

    rope_store(0, q_ref, 1.0)
    rope_store(HK, k_ref, RET_DK ** -0.5)
    v_ref[...] = _dot(h, w_ref[:, 2 * HK: 2 * HK + HV]).astype(v_ref.dtype)
    gate_ref[...] = _dot(h, w_ref[:, 2 * HK + HV:]).astype(gate_ref.dtype)


def _ret_proj(x2d, g, cos, sin, w_in, n_seq, out_dtype):
    rows, d = x2d.shape
    seq_rows = rows // n_seq
    tile = min(seq_rows, ROW_TILE)
    tps = seq_rows // tile
    row_spec = lambda w: pl.BlockSpec((tile, w), lambda b, t: (b * tps + t, 0))
    tab_spec = pl.BlockSpec((tile, RET_DK // 2), lambda b, t: (t, 0))
    return pl.pallas_call(
        _retproj_body,
        grid=(n_seq, tps),
        in_specs=[row_spec(d), _resident((1, d)), tab_spec, tab_spec, _resident(w_in.shape)],
        out_specs=[row_spec(HK), row_spec(HK), row_spec(HV), row_spec(HV)],
        out_shape=[jax.ShapeDtypeStruct((rows, HK), out_dtype), jax.ShapeDtypeStruct((rows, HK), out_dtype),
                   jax.ShapeDtypeStruct((rows, HV), out_dtype), jax.ShapeDtypeStruct((rows, HV), out_dtype)],
        compiler_params=_params(("arbitrary", "arbitrary")),
        name="ret_proj",
    )(x2d, g.reshape(1, d), cos, sin, w_in)


def _ret_step(qh, kh, vh, s_old, inner, qdec, kdec, cdec):
    qb = qh.astype(BF16)
    kb = kh.astype(BF16)
    vb = vh.astype(BF16)
    p = (_dot_nt(qb, kb) * inner).astype(BF16)
    o = _dot(p, vb) + _dot(qb, s_old.astype(BF16)) * qdec
    kd = (kh.astype(F32) * kdec).astype(BF16)
    s_new = s_old * cdec + _dot_tn(kd, vb)
    return o, s_new


def _decay_tables(chunk):
    lg = jnp.log(1.0 - 2.0 ** (-5.0 - jnp.arange(RET_HEADS, dtype=F32)))
    idx = jnp.arange(chunk, dtype=F32)
    rel = idx[:, None] - idx[None, :]
    inner = jnp.where(rel[None] >= 0, jnp.exp(jnp.maximum(rel, 0.0)[None] * lg[:, None, None]), 0.0)
    q_dec = jnp.exp((idx + 1.0)[None, :] * lg[:, None])[..., None]
    k_dec = jnp.exp((chunk - 1.0 - idx)[None, :] * lg[:, None])[..., None]
    c_dec = jnp.exp(chunk * lg)[:, None, None]
    return (inner, jnp.broadcast_to(q_dec, (RET_HEADS, chunk, RET_DV)),
            jnp.broadcast_to(k_dec, (RET_HEADS, chunk, RET_DK)),
            jnp.broadcast_to(c_dec, (RET_HEADS, 1, RET_DV)))


def _ret_prompt_body(q_ref, k_ref, v_ref, inner_ref, qdec_ref, kdec_ref, cdec_ref, o_ref, s_ref, *, chunk):
    @pl.when(pl.program_id(1) == 0)
    def _():
        s_ref[...] = jnp.zeros_like(s_ref)

    for c in range(q_ref.shape[0] // chunk):
        rows = slice(c * chunk, (c + 1) * chunk)
        for h in range(RET_HEADS):
            o, s_new = _ret_step(q_ref[rows, h * RET_DK:(h + 1) * RET_DK], k_ref[rows, h * RET_DK:(h + 1) * RET_DK],
                                 v_ref[rows, h * RET_DV:(h + 1) * RET_DV], s_ref[0, h],
                                 inner_ref[h], qdec_ref[h], kdec_ref[h], cdec_ref[h])
            o_ref[rows, h * RET_DV:(h + 1) * RET_DV] = o
            s_ref[0, h] = s_new


def _ret_prompt(q, k, v, n_seq):
    rows = q.shape[0]
    seq_rows = rows // n_seq
    tile = min(seq_rows, ROW_TILE)
    chunk = min(tile, RET_CHUNK_PROMPT)
    tps = seq_rows // tile
    tabs = _decay_tables(chunk)
    row_spec = lambda w: pl.BlockSpec((tile, w), lambda b, t: (b * tps + t, 0))
    return pl.pallas_call(
        functools.partial(_ret_prompt_body, chunk=chunk),
        grid=(n_seq, tps),
        in_specs=[row_spec(HK), row_spec(HK), row_spec(HV)] + [_resident(t.shape) for t in tabs],
        out_specs=[row_spec(HV), pl.BlockSpec((1, RET_HEADS, RET_DK, RET_DV), lambda b, t: (b, 0, 0, 0))],
        out_shape=[jax.ShapeDtypeStruct((rows, HV), F32),
                   jax.ShapeDtypeStruct((n_seq, RET_HEADS, RET_DK, RET_DV), F32)],
        compiler_params=_params(("arbitrary", "arbitrary")),
        name="ret_prompt",
    )(q, k, v, *tabs)


def _ret_sample_body(q_ref, k_ref, v_ref, s0_ref, inner_ref, qdec_ref, kdec_ref, cdec_ref, o_ref, s_ref, *, seq_rows):
    for i in range(s0_ref.shape[0]):
        rows = slice(i * seq_rows, (i + 1) * seq_rows)
        for h in range(RET_HEADS):
            o, s_new = _ret_step(q_ref[rows, h * RET_DK:(h + 1) * RET_DK], k_ref[rows, h * RET_DK:(h + 1) * RET_DK],
                                 v_ref[rows, h * RET_DV:(h + 1) * RET_DV], s0_ref[i, h],
                                 inner_ref[h], qdec_ref[h], kdec_ref[h], cdec_ref[h])
            o_ref[rows, h * RET_DV:(h + 1) * RET_DV] = o
            s_ref[i, h] = s_new


def _ret_sample(q, k, v, s0, seq_rows):
    n_seq = s0.shape[0]
    bb = 2 if n_seq % 2 == 0 else 1
    tabs = _decay_tables(seq_rows)
    row_spec = lambda w: pl.BlockSpec((bb * seq_rows, w), lambda i: (i, 0))
    st_spec = pl.BlockSpec((bb, RET_HEADS, RET_DK, RET_DV), lambda i: (i, 0, 0, 0))
    return pl.pallas_call(
        functools.partial(_ret_sample_body, seq_rows=seq_rows),
        grid=(n_seq // bb,),
        in_specs=[row_spec(HK), row_spec(HK), row_spec(HV), st_spec] + [_resident(t.shape) for t in tabs],
        out_specs=[row_spec(HV), st_spec],
        out_shape=[jax.ShapeDtypeStruct((n_seq * seq_rows, HV), F32), jax.ShapeDtypeStruct(s0.shape, F32)],
        compiler_params=_params(("arbitrary",)),
        name="ret_sample",
    )(q, k, v, s0, *tabs)


def _retout_body(o_ref, gate_ref, x_ref, gn_ref, w_ref, out_ref):
    parts = []
    for h in range(RET_HEADS):
        oh = o_ref[:, h * RET_DV:(h + 1) * RET_DV]
        mu = jnp.mean(oh, axis=-1, keepdims=True)
        dlt = oh - mu
        var = jnp.mean(dlt * dlt, axis=-1, keepdims=True)
        parts.append(dlt * lax.rsqrt(var + GN_EPS))
    on = jnp.concatenate(parts, axis=-1) * gn_ref[...]
    z = (_silu(gate_ref[...].astype(F32)) * on).astype(BF16)
    out_ref[...] = x_ref[...] + _dot(z, w_ref[...])


def _ret_out(o, gate, x2d, gn_g, w_out):
    rows, d = x2d.shape
    tile = min(rows, ROW_TILE)
    row_spec = lambda w: pl.BlockSpec((tile, w), lambda r: (r, 0))
    return pl.pallas_call(
        _retout_body,
        grid=(rows // tile,),
        in_specs=[row_spec(HV), row_spec(HV), row_spec(d), _resident((1, HV)), _resident(w_out.shape)],
        out_specs=row_spec(d),
        out_shape=jax.ShapeDtypeStruct((rows, d), F32),
        compiler_params=_params(("arbitrary",)),
        name="ret_out",
    )(o, gate, x2d, gn_g.reshape(1, HV), w_out)


def _xattn_body(x_ref, g_ref, wq_ref, wo_ref, k_ref, v_ref, out_ref, att_ref, *, seq_rows):
    x = x_ref[...]
    h = _rms(x, g_ref[...]).astype(BF16)
    q = _dot(h, wq_ref[...]) * (X_HEAD_DIM ** -0.5)
    for i in range(k_ref.shape[0]):
        rows = slice(i * seq_rows, (i + 1) * seq_rows)
        for hd in range(X_HEADS):
            cols = slice(hd * X_HEAD_DIM, (hd + 1) * X_HEAD_DIM)
            s = _dot_nt(q[rows, cols].astype(BF16), k_ref[i, :, cols].astype(BF16))
            e = jnp.exp(s - jnp.max(s, axis=-1, keepdims=True))
            p = e / jnp.sum(e, axis=-1, keepdims=True)
            att_ref[rows, cols] = _dot(p.astype(BF16), v_ref[i, :, cols].astype(BF16))
    out_ref[...] = x + _dot(att_ref[...].astype(BF16), wo_ref[...])


def _xattn(x2d, g, wq, wo, mem_k, mem_v, seq_rows):
    rows, d = x2d.shape
    n_seq, n_mem, _ = mem_k.shape
    if seq_rows >= ROW_TILE:
        tile, bb = ROW_TILE, 1
        tps = seq_rows // tile
        grid = (n_seq, tps)
        row_spec = pl.BlockSpec((tile, d), lambda b, t: (b * tps + t, 0))
        kv_spec = pl.BlockSpec((1, n_mem, d), lambda b, t: (b, 0, 0))
        body_rows = tile
        sem = ("arbitrary", "arbitrary")
    else:
        bb = 4 if n_seq % 4 == 0 else 1
        tile = bb * seq_rows
        grid = (n_seq // bb,)
        row_spec = pl.BlockSpec((tile, d), lambda i: (i, 0))
        kv_spec = pl.BlockSpec((bb, n_mem, d), lambda i: (i, 0, 0))
        body_rows = seq_rows
        sem = ("arbitrary",)
    return pl.pallas_call(
        functools.partial(_xattn_body, seq_rows=body_rows),
        grid=grid,
        in_specs=[row_spec, _resident((1, d)), _resident(wq.shape), _resident(wo.shape), kv_spec, kv_spec],
        out_specs=row_spec,
        out_shape=jax.ShapeDtypeStruct((rows, d), F32),
        scratch_shapes=[pltpu.VMEM((tile, d), F32)],
        compiler_params=_params(sem),
        name="xattn",
    )(x2d, g.reshape(1, d), wq, wo, mem_k, mem_v)


def _conv_chunk(u, prev, cwb, shift):
    c = cwb[3:4] + cwb[0:1] * shift(u, prev, 2)
    c = c + cwb[1:2] * shift(u, prev, 1)
    return c + cwb[2:3] * u


def _ffn_chunk(h, wa, wg, wd, prev_a, prev_g, cwb_a, cwb_g, shift):
    ua = _dot(h, wa)
    ug = _dot(h, wg)
    z = (_conv_chunk(ua, prev_a, cwb_a, shift) * _silu(_conv_chunk(ug, prev_g, cwb_g, shift))).astype(BF16)
    return _dot(z, wd), ua, ug


def _shift_rows(u, prev, j):
    r = pltpu.roll(u, j, axis=0)
    rp = pltpu.roll(prev, j, axis=0)
    row = lax.broadcasted_iota(jnp.int32, prev.shape, 0)
    return jnp.concatenate([jnp.where(row < j, rp, r[:SUBLANES]), r[SUBLANES:]], axis=0)


def _ffn_prompt_body(x_ref, g_ref, gf_ref, wa_ref, wg_ref, wd_ref, cwa_ref, cwg_ref,
                     out_ref, ca_ref, cg_ref, h_ref, acc_ref, *, final_norm):
    @pl.when(pl.program_id(1) == 0)
    def _():
        ca_ref[...] = jnp.zeros_like(ca_ref)
        cg_ref[...] = jnp.zeros_like(cg_ref)

    x = x_ref[...]
    h_ref[...] = _rms(x, g_ref[...]).astype(BF16)
    acc_ref[...] = x

    def step(f, carry):
        y, ua, ug = _ffn_chunk(h_ref[...], wa_ref[f], wg_ref[f], wd_ref[f], ca_ref[0, f], cg_ref[0, f],
                               cwa_ref[f], cwg_ref[f], _shift_rows)
        acc_ref[...] += y
        ca_ref[0, f] = ua[-SUBLANES:]
        cg_ref[0, f] = ug[-SUBLANES:]
        return carry

    lax.fori_loop(0, NF, step, 0)
    y = acc_ref[...]
    out_ref[...] = _rms(y, gf_ref[...]) if final_norm else y


def _ffn_prompt(x2d, g, g_final, wa, wg, wd, cwa, cwg, n_seq, final_norm):
    rows, d = x2d.shape
    seq_rows = rows // n_seq
    tile = min(seq_rows, ROW_TILE)
    tps = seq_rows // tile
    row_spec = pl.BlockSpec((tile, d), lambda b, t: (b * tps + t, 0))
    c_spec = pl.BlockSpec((1, NF, SUBLANES, FF_CHUNK), lambda b, t: (b, 0, 0, 0))
    c_shape = jax.ShapeDtypeStruct((n_seq, NF, SUBLANES, FF_CHUNK), F32)
    return pl.pallas_call(
        functools.partial(_ffn_prompt_body, final_norm=final_norm),
        grid=(n_seq, tps),
        in_specs=[row_spec, _resident((1, d)), _resident((1, d)), _resident(wa.shape), _resident(wg.shape),
                  _resident(wd.shape), _resident(cwa.shape), _resident(cwg.shape)],
        out_specs=[row_spec, c_spec, c_spec],
        out_shape=[jax.ShapeDtypeStruct((rows, d), F32), c_shape, c_shape],
        scratch_shapes=[pltpu.VMEM((tile, d), BF16), pltpu.VMEM((tile, d), F32)],
        compiler_params=_params(("arbitrary", "arbitrary")),
        name="ffn_prompt",
    )(x2d, g.reshape(1, d), g_final.reshape(1, d), wa, wg, wd, cwa, cwg)


def _ffn_sample_body(x_ref, g_ref, gf_ref, wa_ref, wg_ref, wd_ref, cwa_ref, cwg_ref, ba_ref, bg_ref,
                     out_ref, ca_ref, cg_ref, h_ref, acc_ref, *, n_seq, seq_rows, final_norm):
    f = pl.program_id(0)

    @pl.when(f == 0)
    def _():
        xt = jnp.concatenate([x_ref[:, t * D_MODEL:(t + 1) * D_MODEL] for t in range(seq_rows)], axis=0)
        h_ref[...] = _rms(xt, g_ref[...]).astype(BF16)
        acc_ref[...] = xt

    rows = n_seq * seq_rows
    hist = (CONV_W - 1) * n_seq

    def shift(u, prev, j):
        return jnp.concatenate([prev, u], axis=0)[hist - j * n_seq: hist - j * n_seq + rows]

    y, ua, ug = _ffn_chunk(h_ref[...], wa_ref[0], wg_ref[0], wd_ref[0], ba_ref[0], bg_ref[0],
                           cwa_ref[0], cwg_ref[0], shift)
    acc_ref[...] += y
    ca_ref[0] = ua[rows - hist:]
    cg_ref[0] = ug[rows - hist:]

    @pl.when(f == pl.num_programs(0) - 1)
    def _():
        yt = acc_ref[...]
        if final_norm:
            yt = _rms(yt, gf_ref[...])
        for t in range(seq_rows):
            out_ref[:, t * D_MODEL:(t + 1) * D_MODEL] = yt[t * n_seq:(t + 1) * n_seq]


def _ffn_sample(x2d, g, g_final, wa, wg, wd, cwa, cwg, buf_a, buf_g, n_seq, final_norm):
    rows, d = x2d.shape
    seq_rows = rows // n_seq
    hist = (CONV_W - 1) * n_seq
    chunk_spec = lambda s: pl.BlockSpec((1,) + s, lambda f: (f, 0, 0))
    c_shape = jax.ShapeDtypeStruct((NF, hist, FF_CHUNK), F32)
    wide = (n_seq, seq_rows * d)
    y, ca, cg = pl.pallas_call(
        functools.partial(_ffn_sample_body, n_seq=n_seq, seq_rows=seq_rows, final_norm=final_norm),
        grid=(NF,),
        in_specs=[_resident(wide), _resident((1, d)), _resident((1, d)),
                  chunk_spec((d, FF_CHUNK)), chunk_spec((d, FF_CHUNK)), chunk_spec((FF_CHUNK, d)),
                  chunk_spec((4, FF_CHUNK)), chunk_spec((4, FF_CHUNK)),
                  chunk_spec((hist, FF_CHUNK)), chunk_spec((hist, FF_CHUNK))],
        out_specs=[pl.BlockSpec(wide, lambda f: (0, 0)), chunk_spec((hist, FF_CHUNK)), chunk_spec((hist, FF_CHUNK))],
        out_shape=[jax.ShapeDtypeStruct(wide, F32), c_shape, c_shape],
        scratch_shapes=[pltpu.VMEM((rows, d), BF16), pltpu.VMEM((rows, d), F32)],
        compiler_params=_params(("arbitrary",)),
        name="ffn_sample",
    )(x2d.reshape(wide), g.reshape(1, d), g_final.reshape(1, d), wa, wg, wd, cwa, cwg, buf_a, buf_g)
    return y.reshape(rows, d), ca, cg


def _pool_mix(pooled, w_ref, scale_ref, x):
    mixed = [_dot(pooled[g].astype(BF16), w_ref[g]) for g in range(len(POOL_WINDOWS))]
    return x + jnp.concatenate(mixed, axis=-1) * scale_ref[...]


def _pool_prompt_body(x_ref, g_ref, w_ref, scale_ref, out_ref, hist_ref):
    t = pl.program_id(1)

    @pl.when(t == 0)
    def _():
        hist_ref[...] = jnp.zeros_like(hist_ref)

    x = x_ref[...]
    tile = x.shape[0]
    h = _rms(x, g_ref[...])
    ext = jnp.concatenate([hist_ref[0], h], axis=0)
    hist_ref[0] = h[tile - POOL_CARRY:]
    pos1 = (t * tile + 1 + lax.broadcasted_iota(jnp.int32, (tile, POOL_GC), 0)).astype(F32)
    pooled = []
    s = ext
    for g, w in enumerate(POOL_WINDOWS):
        s = s[:, POOL_GC * (1 if g else 0):]
        s = s + pltpu.roll(s, w // 2, axis=0)
        cnt = jnp.minimum(pos1, float(w))
        pooled.append(s[POOL_CARRY:, :POOL_GC] / cnt - h[:, g * POOL_GC:(g + 1) * POOL_GC])
    out_ref[...] = _pool_mix(pooled, w_ref, scale_ref, x)


def _pool_prompt(x2d, g, pool_w, pool_scale, n_seq):
    rows, d = x2d.shape
    seq_rows = rows // n_seq
    tile = min(seq_rows, ROW_TILE)
    tps = seq_rows // tile
    row_spec = pl.BlockSpec((tile, d), lambda b, t: (b * tps + t, 0))
    return pl.pallas_call(
        _pool_prompt_body,
        grid=(n_seq, tps),
        in_specs=[row_spec, _resident((1, d)), _resident(pool_w.shape), _resident((1, d))],
        out_specs=[row_spec, pl.BlockSpec((1, POOL_CARRY, d), lambda b, t: (b, 0, 0))],
        out_shape=[jax.ShapeDtypeStruct((rows, d), F32), jax.ShapeDtypeStruct((n_seq, POOL_CARRY, d), F32)],
        compiler_params=_params(("arbitrary", "arbitrary")),
        name="pool_prompt",
    )(x2d, g.reshape(1, d), pool_w, pool_scale.reshape(1, d))


def _pool_sample_body(x_ref, g_ref, w_ref, scale_ref, buf_ref, out_ref, h_ref, *, n_seq, seq_rows, pos0):
    xt = jnp.concatenate([x_ref[:, t * D_MODEL:(t + 1) * D_MODEL] for t in range(seq_rows)], axis=0)
    h = _rms(xt, g_ref[...])
    s = jnp.concatenate([buf_ref[...], h], axis=0)
    rows = n_seq * seq_rows
    first = POOL_BUF
    pooled = []
    for g, w in enumerate(POOL_WINDOWS):
        s = s[:, POOL_GC * (1 if g else 0):]
        step = (w // 2) * n_seq
        s = s[step:] + s[:-step]
        first -= w // 2
        win = s[first * n_seq: first * n_seq + rows, :POOL_GC]
        inv = [1.0 / min(pos0 + t + 1, w) for t in range(seq_rows)]
        if len(set(inv)) == 1:
            win = win * inv[0]
        else:
            win = jnp.concatenate([win[t * n_seq:(t + 1) * n_seq] * inv[t] for t in range(seq_rows)], axis=0)
        pooled.append(win - h[:, g * POOL_GC:(g + 1) * POOL_GC])
    yt = _pool_mix(pooled, w_ref, scale_ref, xt)
    for t in range(seq_rows):
        out_ref[:, t * D_MODEL:(t + 1) * D_MODEL] = yt[t * n_seq:(t + 1) * n_seq]
        h_ref[:, t * D_MODEL:(t + 1) * D_MODEL] = h[t * n_seq:(t + 1) * n_seq]


def _pool_sample(x2d, g, pool_w, pool_scale, buf_t, n_seq, pos0):
    rows, d = x2d.shape
    seq_rows = rows // n_seq
    full = lambda s: pl.BlockSpec(s, lambda i: (0,) * len(s))
    wide = (n_seq, seq_rows * d)
    y, h = pl.pallas_call(
        functools.partial(_pool_sample_body, n_seq=n_seq, seq_rows=seq_rows, pos0=pos0),
        grid=(1,),
        in_specs=[full(wide), full((1, d)), full(pool_w.shape), full((1, d)), full(buf_t.shape)],
        out_specs=[full(wide), full(wide)],
        out_shape=[jax.ShapeDtypeStruct(wide, F32)] * 2,
        compiler_params=_params(("arbitrary",)),
        name="pool_sample",
    )(x2d.reshape(wide), g.reshape(1, d), pool_w, pool_scale.reshape(1, d), buf_t)
    return y.reshape(rows, d), h.reshape(rows, d)


def _rope_tables(pos):
    inv = 1.0 / (ROPE_BASE ** (jnp.arange(0, RET_DK, 2, dtype=F32) / RET_DK))
    ang = pos[:, None] * inv[None, :]
    return jnp.cos(ang), jnp.sin(ang)


def _chunked_cols(a):
    lead = a.shape[:-1]
    return jnp.moveaxis(a.reshape(lead + (NF, FF_CHUNK)), -2, 0)


def _unchunk_cols(a):
    a = jnp.moveaxis(a, 0, -2)
    return a.reshape(a.shape[:-2] + (D_FF,))


def kernel(x_prompt, x_sample, mem_prompt, cache_mem_k, cache_mem_v, state_ret, cache_pool, cache_ffn_conv, w_ret_in, ret_gn, w_ret_out, pool_w, pool_scale, norm_mem, w_xq, w_xk, w_xv, w_xo, w_up, conv_w, conv_b, w_down, norm_mix, norm_xattn, norm_ffn, norm_final):
    bp, lp, d = x_prompt.shape
    bs, ls, _ = x_sample.shape
    depth = w_up.shape[0]
    n_mem = mem_prompt.shape[1]

    w_in_b = w_ret_in.astype(BF16)
    w_out_b = w_ret_out.astype(BF16)
    pool_w_b = pool_w.astype(BF16)
    wq_b, wk_b, wv_b, wo_b = (w.astype(BF16) for w in (w_xq, w_xk, w_xv, w_xo))
    wa_b = jnp.stack([_chunked_cols(w_up[i, :, :D_FF]) for i in range(depth)]).astype(BF16)
    wg_b = jnp.stack([_chunked_cols(w_up[i, :, D_FF:]) for i in range(depth)]).astype(BF16)
    wd_b = w_down.reshape(depth, NF, FF_CHUNK, d).astype(BF16)
    cwb = jnp.concatenate([conv_w, conv_b[:, None, :]], axis=1)
    cwa = jnp.stack([_chunked_cols(cwb[i, :, :D_FF]) for i in range(depth)])
    cwg = jnp.stack([_chunked_cols(cwb[i, :, D_FF:]) for i in range(depth)])

    mem_k2, mem_v2 = _mem_kv(mem_prompt.reshape(bp * n_mem, d), norm_mem, wk_b, wv_b)
    mem_k_p = mem_k2.reshape(depth, bp, n_mem, d)
    mem_v_p = mem_v2.reshape(depth, bp, n_mem, d)

    cos_p, sin_p = _rope_tables(jnp.arange(lp, dtype=F32))
    cos_s, sin_s = _rope_tables(PAST_LEN + jnp.arange(ls, dtype=F32))
    cos_s, sin_s = jnp.tile(cos_s, (bs, 1)), jnp.tile(sin_s, (bs, 1))

    xp = x_prompt.reshape(bp * lp, d)
    xs = x_sample.reshape(bs * ls, d)
    ret_p, ret_s, pool_p, pool_s, conv_p, conv_s = [], [], [], [], [], []
    for i in range(depth):
        j = i // 2
        last = i == depth - 1
        if i % 2 == 0:
            q, k, v, gate = _ret_proj(xp, norm_mix[i], cos_p, sin_p, w_in_b[j], bp, BF16)
            o, s = _ret_prompt(q, k, v, bp)
            xp = _ret_out(o, gate, xp, ret_gn[j], w_out_b[j])
            ret_p.append(s)
            q, k, v, gate = _ret_proj(xs, norm_mix[i], cos_s, sin_s, w_in_b[j], 1, F32)
            o, s = _ret_sample(q, k, v, state_ret[j], ls)
            xs = _ret_out(o, gate, xs, ret_gn[j], w_out_b[j])
            ret_s.append(s)
        else:
            xp, hist = _pool_prompt(xp, norm_mix[i], pool_w_b[j], pool_scale[j], bp)
            pool_p.append(hist[:, POOL_CARRY - POOL_BUF:])
            buf_t = jnp.swapaxes(cache_pool[j], 0, 1).reshape(POOL_BUF * bs, d)
            xs, hs = _pool_sample(xs, norm_mix[i], pool_w_b[j], pool_scale[j], buf_t, bs, PAST_LEN)
            pool_s.append(jnp.concatenate([cache_pool[j], hs.reshape(bs, ls, d)], axis=1)[:, -POOL_BUF:])

        xp = _xattn(xp, norm_xattn[i], wq_b[i], wo_b[i], mem_k2[i].reshape(bp, n_mem, d),
                    mem_v2[i].reshape(bp, n_mem, d), lp)
        xs = _xattn(xs, norm_xattn[i], wq_b[i], wo_b[i], cache_mem_k[i].reshape(bs, n_mem, d),
                    cache_mem_v[i].reshape(bs, n_mem, d), ls)

        xp, ca, cg = _ffn_prompt(xp, norm_ffn[i], norm_final, wa_b[i], wg_b[i], wd_b[i], cwa[i], cwg[i], bp, last)
        tail = jnp.concatenate([_unchunk_cols(jnp.moveaxis(ca, 1, 0)), _unchunk_cols(jnp.moveaxis(cg, 1, 0))], axis=-1)
        conv_p.append(tail[:, SUBLANES - (CONV_W - 1):])
        buf = jnp.swapaxes(cache_ffn_conv[i], 0, 1).reshape((CONV_W - 1) * bs, 2 * D_FF)
        xs, ca, cg = _ffn_sample(xs, norm_ffn[i], norm_final, wa_b[i], wg_b[i], wd_b[i], cwa[i], cwg[i],
                                 _chunked_cols(buf[:, :D_FF]), _chunked_cols(buf[:, D_FF:]), bs, last)
        tail = jnp.concatenate([_unchunk_cols(ca), _unchunk_cols(cg)], axis=-1)
        conv_s.append(jnp.swapaxes(tail.reshape(CONV_W - 1, bs, 2 * D_FF), 0, 1))

    mk = mem_k_p.reshape(depth, bp, n_mem, X_HEADS, X_HEAD_DIM)
    mv = mem_v_p.reshape(depth, bp, n_mem, X_HEADS, X_HEAD_DIM)
    return (xp.reshape(bp, lp, d), xs.reshape(bs, ls, d), jnp.stack(ret_p), jnp.stack(ret_s).astype(state_ret.dtype),
            jnp.stack(pool_p), jnp.stack(pool_s), jnp.stack(conv_p), jnp.stack(conv_s), mk, mv)
```

```python
import functools

import jax
import jax.numpy as jnp
from jax import lax
from jax.experimental import pallas as pl
from jax.experimental.pallas import tpu as pltpu

F32 = jnp.float32
BF16 = jnp.bfloat16

D_MODEL = 1024
PAST_LEN = 16384
RET_HEADS = 4
RET_DK = D_MODEL // RET_HEADS
RET_DV = 2 * D_MODEL // RET_HEADS
HK = RET_HEADS * RET_DK
HV = RET_HEADS * RET_DV
ROPE_BASE = 10000.0
POOL_WINDOWS = (2, 4, 8, 16)
POOL_GC = D_MODEL // len(POOL_WINDOWS)
POOL_BUF = max(POOL_WINDOWS) - 1
X_HEADS = 4
X_HEAD_DIM = D_MODEL // X_HEADS
D_FF = 2816
CONV_W = 3
NORM_EPS = 1e-6
GN_EPS = 1e-5

SUBLANES = 8
ROW_TILE = 512
RET_CHUNK_PROMPT = 256
FF_CHUNK = 256
NF = D_FF // FF_CHUNK
FFN_GATE_ROWS = 64
POOL_CARRY = 16
VMEM_LIMIT = 56 * 1024 * 1024


def _params(sem):
    return pltpu.CompilerParams(dimension_semantics=sem, vmem_limit_bytes=VMEM_LIMIT)


def _resident(shape):
    zeros = (0,) * len(shape)
    return pl.BlockSpec(shape, lambda *_: zeros, pipeline_mode=pl.Buffered(1))


def _layer_resident(stacked_shape, layer):
    idx = (layer,) + (0,) * (len(stacked_shape) - 1)
    return pl.BlockSpec((1,) + tuple(stacked_shape[1:]), lambda *_: idx, pipeline_mode=pl.Buffered(1))


def _gain(g):
    return g.reshape(g.shape[0], 1, g.shape[1])


def _dot(a, b):
    return jnp.dot(a, b, preferred_element_type=F32)


def _dot_nt(a, b):
    return lax.dot_general(a, b, (((1,), (1,)), ((), ())), preferred_element_type=F32)


def _dot_tn(a, b):
    return lax.dot_general(a, b, (((0,), (0,)), ((), ())), preferred_element_type=F32)


def _rms(x, g):
    return x * lax.rsqrt(jnp.mean(x * x, axis=-1, keepdims=True) + NORM_EPS) * g


def _silu(x):
    return x * (1.0 / (1.0 + jnp.exp(-x)))


def _memkv_body(mem_ref, g_ref, wk_ref, wv_ref, k_ref, v_ref):
    h = _rms(mem_ref[...], g_ref[0]).astype(BF16)
    k_ref[0] = _dot(h, wk_ref[0])
    v_ref[0] = _dot(h, wv_ref[0])


def _mem_kv(mem2d, norm_mem, wk, wv):
    rows, d = mem2d.shape
    depth = wk.shape[0]
    tile = min(rows, 1024)
    w_spec = pl.BlockSpec((1, d, d), lambda i, r: (i, 0, 0))
    o_spec = pl.BlockSpec((1, tile, d), lambda i, r: (i, r, 0))
    return pl.pallas_call(
        _memkv_body,
        grid=(depth, rows // tile),
        in_specs=[pl.BlockSpec((tile, d), lambda i, r: (r, 0)),
                  pl.BlockSpec((1, 1, d), lambda i, r: (i, 0, 0)), w_spec, w_spec],
        out_specs=[o_spec, o_spec],
        out_shape=[jax.ShapeDtypeStruct((depth, rows, d), F32)] * 2,
        compiler_params=_params(("arbitrary", "arbitrary")),
        name="mem_kv",
    )(mem2d, _gain(norm_mem), wk, wv)


def _retproj_body(x_ref, g_ref, cos_ref, sin_ref, w_ref, q_ref, k_ref, v_ref, gate_ref):
    h = _rms(x_ref[...], g_ref[0]).astype(BF16)
    cos = cos_ref[...]
    sin = sin_ref[...]
    half = RET_DK // 2

    def rope_store(col0, out_ref, scale):
        p = _dot(h, w_ref[0, :, col0:col0 + HK])
        for hd in range(RET_HEADS):
            a = p[:, hd * RET_DK: hd * RET_DK + half]
            b = p[:, hd * RET_DK + half: (hd + 1) * RET_DK]
            out_ref[:, hd * RET_DK: hd * RET_DK + half] = ((a * cos - b * sin) * scale).astype(out_ref.dtype)
            out_ref[:, hd * RET_DK + half: (hd + 1) * RET_DK] = ((b * cos + a * sin) * scale).astype(out_ref.dtype)

    rope_store(0, q_ref, 1.0)
    rope_store(HK, k_ref, RET_DK ** -0.5)
    v_ref[...] = _dot(h, w_ref[0, :, 2 * HK: 2 * HK + HV]).astype(v_ref.dtype)
    gate_ref[...] = _dot(h, w_ref[0, :, 2 * HK + HV:]).astype(gate_ref.dtype)


def _ret_proj(x2d, gains, layer, cos, sin, w_in, ret_layer, n_seq, out_dtype):
    rows, d = x2d.shape
    seq_rows = rows // n_seq
    tile = min(seq_rows, ROW_TILE)
    tps = seq_rows // tile
    row_spec = lambda w: pl.BlockSpec((tile, w), lambda b, t: (b * tps + t, 0))
    tab_spec = pl.BlockSpec((tile, RET_DK // 2), lambda b, t: (t, 0))
    return pl.pallas_call(
        _retproj_body,
        grid=(n_seq, tps),
        in_specs=[row_spec(d), _layer_resident(gains.shape, layer), tab_spec, tab_spec,
                  _layer_resident(w_in.shape, ret_layer)],
        out_specs=[row_spec(HK), row_spec(HK), row_spec(HV), row_spec(HV)],
        out_shape=[jax.ShapeDtypeStruct((rows, HK), out_dtype), jax.ShapeDtypeStruct((rows, HK), out_dtype),
                   jax.ShapeDtypeStruct((rows, HV), out_dtype), jax.ShapeDtypeStruct((rows, HV), out_dtype)],
        compiler_params=_params(("arbitrary", "arbitrary")),
        name="ret_proj",
    )(x2d, gains, cos, sin, w_in)


def _ret_step(qh, kh, vh, s_old, inner, qdec, kdec, cdec):
    qb = qh.astype(BF16)
    kb = kh.astype(BF16)
    vb = vh.astype(BF16)
    p = (_dot_nt(qb, kb) * inner).astype(BF16)
    o = _dot(p, vb) + _dot(qb, s_old.astype(BF16)) * qdec
    kd = (kh.astype(F32) * kdec).astype(BF16)
    s_new = s_old * cdec + _dot_tn(kd, vb)
    return o, s_new


def _decay_tables(chunk):
    lg = jnp.log(1.0 - 2.0 ** (-5.0 - jnp.arange(RET_HEADS, dtype=F32)))
    idx = jnp.arange(chunk, dtype=F32)
    rel = idx[:, None] - idx[None, :]
    inner = jnp.where(rel[None] >= 0, jnp.exp(jnp.maximum(rel, 0.0)[None] * lg[:, None, None]), 0.0)
    q_dec = jnp.exp((idx + 1.0)[None, :] * lg[:, None])[..., None]
    k_dec = jnp.exp((chunk - 1.0 - idx)[None, :] * lg[:, None])[..., None]
    c_dec = jnp.exp(chunk * lg)[:, None, None]
    return (inner, jnp.broadcast_to(q_dec, (RET_HEADS, chunk, RET_DV)),
            jnp.broadcast_to(k_dec, (RET_HEADS, chunk, RET_DK)),
            jnp.broadcast_to(c_dec, (RET_HEADS, 1, RET_DV)))


def _ret_prompt_body(q_ref, k_ref, v_ref, inner_ref, qdec_ref, kdec_ref, cdec_ref, o_ref, s_ref, *, chunk):
    @pl.when(pl.program_id(1) == 0)
    def _():
        s_ref[...] = jnp.zeros_like(s_ref)

    for c in range(q_ref.shape[0] // chunk):
        rows = slice(c * chunk, (c + 1) * chunk)
        for h in range(RET_HEADS):
            o, s_new = _ret_step(q_ref[rows, h * RET_DK:(h + 1) * RET_DK], k_ref[rows, h * RET_DK:(h + 1) * RET_DK],
                                 v_ref[rows, h * RET_DV:(h + 1) * RET_DV], s_ref[0, h],
                                 inner_ref[h], qdec_ref[h], kdec_ref[h], cdec_ref[h])
            o_ref[rows, h * RET_DV:(h + 1) * RET_DV] = o
            s_ref[0, h] = s_new


def _ret_prompt(q, k, v, n_seq):
    rows = q.shape[0]
    seq_rows = rows // n_seq
    tile = min(seq_rows, ROW_TILE)
    chunk = min(tile, RET_CHUNK_PROMPT)
    tps = seq_rows // tile
    tabs = _decay_tables(chunk)
    row_spec = lambda w: pl.BlockSpec((tile, w), lambda b, t: (b * tps + t, 0))
    return pl.pallas_call(
        functools.partial(_ret_prompt_body, chunk=chunk),
        grid=(n_seq, tps),
        in_specs=[row_spec(HK), row_spec(HK), row_spec(HV)] + [_resident(t.shape) for t in tabs],
        out_specs=[row_spec(HV), pl.BlockSpec((1, RET_HEADS, RET_DK, RET_DV), lambda b, t: (b, 0, 0, 0))],
        out_shape=[jax.ShapeDtypeStruct((rows, HV), F32),
                   jax.ShapeDtypeStruct((n_seq, RET_HEADS, RET_DK, RET_DV), F32)],
        compiler_params=_params(("arbitrary", "arbitrary")),
        name="ret_prompt",
    )(q, k, v, *tabs)


def _ret_sample_body(q_ref, k_ref, v_ref, s0_ref, inner_ref, qdec_ref, kdec_ref, cdec_ref, o_ref, s_ref, *, seq_rows):
    for i in range(s0_ref.shape[0]):
        rows = slice(i * seq_rows, (i + 1) * seq_rows)
        for h in range(RET_HEADS):
            o, s_new = _ret_step(q_ref[rows, h * RET_DK:(h + 1) * RET_DK], k_ref[rows, h * RET_DK:(h + 1) * RET_DK],
                                 v_ref[rows, h * RET_DV:(h + 1) * RET_DV], s0_ref[i, h],
                                 inner_ref[h], qdec_ref[h], kdec_ref[h], cdec_ref[h])
            o_ref[rows, h * RET_DV:(h + 1) * RET_DV] = o
            s_ref[i, h] = s_new


def _ret_sample(q, k, v, states, ret_layer, n_seq, seq_rows):
    bb = 2 if n_seq % 2 == 0 else 1
    steps = n_seq // bb
    tabs = _decay_tables(seq_rows)
    row_spec = lambda w: pl.BlockSpec((bb * seq_rows, w), lambda i: (i, 0))
    st_block = (bb, RET_HEADS, RET_DK, RET_DV)
    return pl.pallas_call(
        functools.partial(_ret_sample_body, seq_rows=seq_rows),
        grid=(steps,),
        in_specs=[row_spec(HK), row_spec(HK), row_spec(HV),
                  pl.BlockSpec(st_block, lambda i: (ret_layer * steps + i, 0, 0, 0))] + [_resident(t.shape) for t in tabs],
        out_specs=[row_spec(HV), pl.BlockSpec(st_block, lambda i: (i, 0, 0, 0))],
        out_shape=[jax.ShapeDtypeStruct((n_seq * seq_rows, HV), F32),
                   jax.ShapeDtypeStruct((n_seq, RET_HEADS, RET_DK, RET_DV), F32)],
        compiler_params=_params(("arbitrary",)),
        name="ret_sample",
    )(q, k, v, states, *tabs)


def _retout_body(o_ref, gate_ref, x_ref, gn_ref, w_ref, out_ref):
    parts = []
    for h in range(RET_HEADS):
        oh = o_ref[:, h * RET_DV:(h + 1) * RET_DV]
        mu = jnp.mean(oh, axis=-1, keepdims=True)
        dlt = oh - mu
        var = jnp.mean(dlt * dlt, axis=-1, keepdims=True)
        parts.append(dlt * lax.rsqrt(var + GN_EPS))
    on = jnp.concatenate(parts, axis=-1) * gn_ref[0]
    z = (_silu(gate_ref[...].astype(F32)) * on).astype(BF16)
    out_ref[...] = x_ref[...] + _dot(z, w_ref[0])


def _ret_out(o, gate, x2d, gn_g, w_out, ret_layer):
    rows, d = x2d.shape
    tile = min(rows, ROW_TILE)
    row_spec = lambda w: pl.BlockSpec((tile, w), lambda r: (r, 0))
    return pl.pallas_call(
        _retout_body,
        grid=(rows // tile,),
        in_specs=[row_spec(HV), row_spec(HV), row_spec(d), _layer_resident(gn_g.shape, ret_layer),
                  _layer_resident(w_out.shape, ret_layer)],
        out_specs=row_spec(d),
        out_shape=jax.ShapeDtypeStruct((rows, d), F32),
        compiler_params=_params(("arbitrary",)),
        name="ret_out",
    )(o, gate, x2d, gn_g, w_out)


def _xattn_body(x_ref, g_ref, wq_ref, wo_ref, k_ref, v_ref, out_ref, att_ref, *, seq_rows):
    x = x_ref[...]
    h = _rms(x, g_ref[0]).astype(BF16)
    q = _dot(h, wq_ref[0]) * (X_HEAD_DIM ** -0.5)
    for i in range(k_ref.shape[0]):
        rows = slice(i * seq_rows, (i + 1) * seq_rows)
        for hd in range(X_HEADS):
            cols = slice(hd * X_HEAD_DIM, (hd + 1) * X_HEAD_DIM)
            s = _dot_nt(q[rows, cols].astype(BF16), k_ref[i, :, cols].astype(BF16))
            e = jnp.exp(s - jnp.max(s, axis=-1, keepdims=True))
            p = e / jnp.sum(e, axis=-1, keepdims=True)
            att_ref[rows, cols] = _dot(p.astype(BF16), v_ref[i, :, cols].astype(BF16))
    out_ref[...] = x + _dot(att_ref[...].astype(BF16), wo_ref[0])


def _xattn(x2d, gains, wq, wo, layer, mem_k, mem_v, n_seq, seq_rows):
    rows, d = x2d.shape
    n_mem = mem_k.shape[1]
    if seq_rows >= ROW_TILE:
        tile, body_rows = ROW_TILE, ROW_TILE
        tps = seq_rows // tile
        grid = (n_seq, tps)
        row_spec = pl.BlockSpec((tile, d), lambda b, t: (b * tps + t, 0))
        kv_spec = pl.BlockSpec((1, n_mem, d), lambda b, t: (layer * n_seq + b, 0, 0))
        sem = ("arbitrary", "arbitrary")
    else:
        bb = 4 if n_seq % 4 == 0 else 1
        tile, body_rows = bb * seq_rows, seq_rows
        steps = n_seq // bb
        grid = (steps,)
        row_spec = pl.BlockSpec((tile, d), lambda i: (i, 0))
        kv_spec = pl.BlockSpec((bb, n_mem, d), lambda i: (layer * steps + i, 0, 0))
        sem = ("arbitrary",)
    return pl.pallas_call(
        functools.partial(_xattn_body, seq_rows=body_rows),
        grid=grid,
        in_specs=[row_spec, _layer_resident(gains.shape, layer), _layer_resident(wq.shape, layer),
                  _layer_resident(wo.shape, layer), kv_spec, kv_spec],
        out_specs=row_spec,
        out_shape=jax.ShapeDtypeStruct((rows, d), F32),
        scratch_shapes=[pltpu.VMEM((tile, d), F32)],
        compiler_params=_params(sem),
        name="xattn",
    )(x2d, gains, wq, wo, mem_k, mem_v)


def _conv_chunk(u, prev, cwb, shift):
    c = cwb[3:4] + cwb[0:1] * shift(u, prev, 2)
    c = c + cwb[1:2] * shift(u, prev, 1)
    return c + cwb[2:3] * u


def _ffn_gate(ua, ug, prev_a, prev_g, cwb_a, cwb_g, shift):
    return (_conv_chunk(ua, prev_a, cwb_a, shift) * _silu(_conv_chunk(ug, prev_g, cwb_g, shift))).astype(BF16)


def _ffn_chunk(h, wa, wg, wd, prev_a, prev_g, cwb_a, cwb_g, shift):
    ua = _dot(h, wa)
    ug = _dot(h, wg)
    return _dot(_ffn_gate(ua, ug, prev_a, prev_g, cwb_a, cwb_g, shift), wd), ua, ug


def _ffn_prompt_body(x_ref, g_ref, gf_ref, wu_ref, wd_ref, cwb_ref, out_ref, c_ref,
                     h_ref, acc_ref, ua0_ref, ug0_ref, ua1_ref, ug1_ref, z0_ref, z1_ref, *, final_norm):
    halo = (CONV_W - 1) * SUBLANES
    tile = acc_ref.shape[0]
    groups = tile // SUBLANES

    @pl.when(pl.program_id(1) == 0)
    def _():
        c_ref[...] = jnp.zeros_like(c_ref)

    x = jnp.concatenate([x_ref[0, :, m * D_MODEL:(m + 1) * D_MODEL] for m in range(groups)], axis=0)
    h_ref[...] = _rms(x, g_ref[0]).astype(BF16)
    acc_ref[...] = x
    slots = ((ua0_ref, ug0_ref), (ua1_ref, ug1_ref))
    first_sublane = lax.broadcasted_iota(jnp.int32, (SUBLANES, FF_CHUNK), 0) == 0

    def cols(f):
        return (slice(f * FF_CHUNK, (f + 1) * FF_CHUNK), slice(D_FF + f * FF_CHUNK, D_FF + (f + 1) * FF_CHUNK))

    def up(f):
        for u_ref, cc in zip(slots[f % 2], cols(f)):
            u = _dot(h_ref[...], wu_ref[0, :, cc])
            u_ref[halo:] = u
            for m in range(CONV_W - 1):
                rows = slice(m * SUBLANES, (m + 1) * SUBLANES)
                src = u[tile - halo + m * SUBLANES: tile - halo + (m + 1) * SUBLANES]
                u_ref[rows] = jnp.where(first_sublane, pltpu.roll(c_ref[0, rows, cc], 1, axis=0),
                                        pltpu.roll(src, 1, axis=0))
                c_ref[0, rows, cc] = src

    def conv(u_ref, cw, r0, n):
        c = cw[3:4] + cw[0:1] * u_ref[r0: r0 + n]
        c = c + cw[1:2] * u_ref[r0 + SUBLANES: r0 + SUBLANES + n]
        return c + cw[2:3] * u_ref[r0 + halo: r0 + halo + n]

    def gate(f):
        ca, cg = cols(f)
        ua_ref, ug_ref = slots[f % 2]
        cwa, cwg = cwb_ref[0, :, ca], cwb_ref[0, :, cg]
        z_ref = (z0_ref, z1_ref)[f % 2]
        for r0 in range(0, tile, FFN_GATE_ROWS):
            z_ref[r0: r0 + FFN_GATE_ROWS] = (conv(ua_ref, cwa, r0, FFN_GATE_ROWS)
                                             * _silu(conv(ug_ref, cwg, r0, FFN_GATE_ROWS))).astype(BF16)
        return z_ref

    up(0)
    for f in range(NF):
        if f + 1 < NF:
            up(f + 1)
        z_ref = gate(f)
        acc_ref[...] += _dot(z_ref[...], wd_ref[0, cols(f)[0], :])
    y = acc_ref[...]
    if final_norm:
        y = _rms(y, gf_ref[...])
    for m in range(groups):
        out_ref[0, :, m * D_MODEL:(m + 1) * D_MODEL] = y[m * SUBLANES:(m + 1) * SUBLANES]


def _ffn_prompt(x2d, gains, g_final, w_up, w_down, cwb, layer, n_seq, final_norm):
    rows, d = x2d.shape
    seq_rows = rows // n_seq
    tile = min(seq_rows, ROW_TILE)
    tps = seq_rows // tile
    halo = (CONV_W - 1) * SUBLANES
    wide = (rows // tile, SUBLANES, (tile // SUBLANES) * d)
    row_spec = pl.BlockSpec((1,) + wide[1:], lambda b, t: (b * tps + t, 0, 0))
    u_scratch = pltpu.VMEM((halo + tile, FF_CHUNK), F32)
    z_scratch = pltpu.VMEM((tile, FF_CHUNK), BF16)
    y, carry = pl.pallas_call(
        functools.partial(_ffn_prompt_body, final_norm=final_norm),
        grid=(n_seq, tps),
        in_specs=[row_spec, _layer_resident(gains.shape, layer), _resident((1, d)),
                  _layer_resident(w_up.shape, layer), _layer_resident(w_down.shape, layer),
                  _layer_resident(cwb.shape, layer)],
        out_specs=[row_spec, pl.BlockSpec((1, halo, 2 * D_FF), lambda b, t: (b, 0, 0))],
        out_shape=[jax.ShapeDtypeStruct(wide, F32), jax.ShapeDtypeStruct((n_seq, halo, 2 * D_FF), F32)],
        scratch_shapes=[pltpu.VMEM((tile, d), BF16), pltpu.VMEM((tile, d), F32),
                        u_scratch, u_scratch, u_scratch, u_scratch, z_scratch, z_scratch],
        compiler_params=_params(("arbitrary", "arbitrary")),
        name="ffn_prompt",
    )(x2d.reshape(wide), gains, g_final.reshape(1, d), w_up, w_down, cwb)
    return y.reshape(rows, d), carry[:, SUBLANES - 1::SUBLANES]


def _ffn_sample_body(x_ref, g_ref, gf_ref, wa_ref, wg_ref, wd_ref, cwa_ref, cwg_ref, ba_ref, bg_ref,
                     out_ref, ca_ref, cg_ref, h_ref, acc_ref, *, n_seq, seq_rows, final_norm):
    f = pl.program_id(0)

    @pl.when(f == 0)
    def _():
        xt = jnp.concatenate([x_ref[:, t * D_MODEL:(t + 1) * D_MODEL] for t in range(seq_rows)], axis=0)
        h_ref[...] = _rms(xt, g_ref[0]).astype(BF16)
        acc_ref[...] = xt

    rows = n_seq * seq_rows
    hist = (CONV_W - 1) * n_seq

    def shift(u, prev, j):
        return jnp.concatenate([prev, u], axis=0)[hist - j * n_seq: hist - j * n_seq + rows]

    y, ua, ug = _ffn_chunk(h_ref[...], wa_ref[0], wg_ref[0], wd_ref[0], ba_ref[...], bg_ref[...],
                           cwa_ref[0], cwg_ref[0], shift)
    acc_ref[...] += y
    ca_ref[...] = ua[rows - hist:]
    cg_ref[...] = ug[rows - hist:]

    @pl.when(f == pl.num_programs(0) - 1)
    def _():
        yt = acc_ref[...]
        if final_norm:
            yt = _rms(yt, gf_ref[...])
        for t in range(seq_rows):
            out_ref[:, t * D_MODEL:(t + 1) * D_MODEL] = yt[t * n_seq:(t + 1) * n_seq]


def _ffn_sample(x2d, gains, g_final, w_up, w_down, cwb, layer, buf, n_seq, final_norm):
    rows, d = x2d.shape
    seq_rows = rows // n_seq
    hist = (CONV_W - 1) * n_seq
    a_col = lambda f: f
    g_col = lambda f: NF + f
    c_shape = jax.ShapeDtypeStruct((hist, D_FF), F32)
    c_spec = pl.BlockSpec((hist, FF_CHUNK), lambda f: (0, f))
    wide = (n_seq, seq_rows * d)
    y, ca, cg = pl.pallas_call(
        functools.partial(_ffn_sample_body, n_seq=n_seq, seq_rows=seq_rows, final_norm=final_norm),
        grid=(NF,),
        in_specs=[_resident(wide), _layer_resident(gains.shape, layer), _resident((1, d)),
                  pl.BlockSpec((1, d, FF_CHUNK), lambda f: (layer, 0, a_col(f))),
                  pl.BlockSpec((1, d, FF_CHUNK), lambda f: (layer, 0, g_col(f))),
                  pl.BlockSpec((1, FF_CHUNK, d), lambda f: (layer, f, 0)),
                  pl.BlockSpec((1, 4, FF_CHUNK), lambda f: (layer, 0, a_col(f))),
                  pl.BlockSpec((1, 4, FF_CHUNK), lambda f: (layer, 0, g_col(f))),
                  pl.BlockSpec((hist, FF_CHUNK), lambda f: (0, a_col(f))),
                  pl.BlockSpec((hist, FF_CHUNK), lambda f: (0, g_col(f)))],
        out_specs=[pl.BlockSpec(wide, lambda f: (0, 0)), c_spec, c_spec],
        out_shape=[jax.ShapeDtypeStruct(wide, F32), c_shape, c_shape],
        scratch_shapes=[pltpu.VMEM((rows, d), BF16), pltpu.VMEM((rows, d), F32)],
        compiler_params=_params(("arbitrary",)),
        name="ffn_sample",
    )(x2d.reshape(wide), gains, g_final.reshape(1, d), w_up, w_up, w_down, cwb, cwb, buf, buf)
    return y.reshape(rows, d), ca, cg


def _pool_mix(pooled, w_ref, scale_ref, x):
    mixed = [_dot(pooled[g].astype(BF16), w_ref[0, g]) for g in range(len(POOL_WINDOWS))]
    return x + jnp.concatenate(mixed, axis=-1) * scale_ref[0]


def _pool_prompt_body(x_ref, g_ref, w_ref, scale_ref, out_ref, hist_ref):
    t = pl.program_id(1)

    @pl.when(t == 0)
    def _():
        hist_ref[...] = jnp.zeros_like(hist_ref)

    x = x_ref[...]
    tile = x.shape[0]
    h = _rms(x, g_ref[0])
    ext = jnp.concatenate([hist_ref[0], h], axis=0)
    hist_ref[0] = h[tile - POOL_CARRY:]
    pos1 = (t * tile + 1 + lax.broadcasted_iota(jnp.int32, (tile, POOL_GC), 0)).astype(F32)
    pooled = []
    s = ext
    for g, w in enumerate(POOL_WINDOWS):
        s = s[:, POOL_GC * (1 if g else 0):]
        s = s + pltpu.roll(s, w // 2, axis=0)
        cnt = jnp.minimum(pos1, float(w))
        pooled.append(s[POOL_CARRY:, :POOL_GC] / cnt - h[:, g * POOL_GC:(g + 1) * POOL_GC])
    out_ref[...] = _pool_mix(pooled, w_ref, scale_ref, x)


def _pool_prompt(x2d, gains, layer, pool_w, pool_scale, pool_layer, n_seq):
    rows, d = x2d.shape
    seq_rows = rows // n_seq
    tile = min(seq_rows, ROW_TILE)
    tps = seq_rows // tile
    row_spec = pl.BlockSpec((tile, d), lambda b, t: (b * tps + t, 0))
    return pl.pallas_call(
        _pool_prompt_body,
        grid=(n_seq, tps),
        in_specs=[row_spec, _layer_resident(gains.shape, layer), _layer_resident(pool_w.shape, pool_layer),
                  _layer_resident(pool_scale.shape, pool_layer)],
        out_specs=[row_spec, pl.BlockSpec((1, POOL_CARRY, d), lambda b, t: (b, 0, 0))],
        out_shape=[jax.ShapeDtypeStruct((rows, d), F32), jax.ShapeDtypeStruct((n_seq, POOL_CARRY, d), F32)],
        compiler_params=_params(("arbitrary", "arbitrary")),
        name="pool_prompt",
    )(x2d, gains, pool_w, pool_scale)


def _pool_sample_body(x_ref, g_ref, w_ref, scale_ref, buf_ref, out_ref, h_ref, *, n_seq, seq_rows, pos0):
    xt = jnp.concatenate([x_ref[:, t * D_MODEL:(t + 1) * D_MODEL] for t in range(seq_rows)], axis=0)
    h = _rms(xt, g_ref[0])
    s = jnp.concatenate([buf_ref[...], h], axis=0)
    rows = n_seq * seq_rows
    first = POOL_BUF
    pooled = []
    for g, w in enumerate(POOL_WINDOWS):
        s = s[:, POOL_GC * (1 if g else 0):]
        step = (w // 2) * n_seq
        s = s[step:] + s[:-step]
        first -= w // 2
        win = s[first * n_seq: first * n_seq + rows, :POOL_GC]
        inv = [1.0 / min(pos0 + t + 1, w) for t in range(seq_rows)]
        if len(set(inv)) == 1:
            win = win * inv[0]
        else:
            win = jnp.concatenate([win[t * n_seq:(t + 1) * n_seq] * inv[t] for t in range(seq_rows)], axis=0)
        pooled.append(win - h[:, g * POOL_GC:(g + 1) * POOL_GC])
    yt = _pool_mix(pooled, w_ref, scale_ref, xt)
    for t in range(seq_rows):
        out_ref[:, t * D_MODEL:(t + 1) * D_MODEL] = yt[t * n_seq:(t + 1) * n_seq]
        h_ref[:, t * D_MODEL:(t + 1) * D_MODEL] = h[t * n_seq:(t + 1) * n_seq]


def _pool_sample(x2d, gains, layer, pool_w, pool_scale, pool_layer, buf_t, n_seq, pos0):
    rows, d = x2d.shape
    seq_rows = rows // n_seq
    full = lambda s: pl.BlockSpec(s, lambda i: (0,) * len(s))
    wide = (n_seq, seq_rows * d)
    y, h = pl.pallas_call(
        functools.partial(_pool_sample_body, n_seq=n_seq, seq_rows=seq_rows, pos0=pos0),
        grid=(1,),
        in_specs=[full(wide), _layer_resident(gains.shape, layer), _layer_resident(pool_w.shape, pool_layer),
                  _layer_resident(pool_scale.shape, pool_layer), full(buf_t.shape)],
        out_specs=[full(wide), full(wide)],
        out_shape=[jax.ShapeDtypeStruct(wide, F32)] * 2,
        compiler_params=_params(("arbitrary",)),
        name="pool_sample",
    )(x2d.reshape(wide), gains, pool_w, pool_scale, buf_t)
    return y.reshape(rows, d), h.reshape(rows, d)


def _rope_tables(pos):
    inv = 1.0 / (ROPE_BASE ** (jnp.arange(0, RET_DK, 2, dtype=F32) / RET_DK))
    ang = pos[:, None] * inv[None, :]
    return jnp.cos(ang), jnp.sin(ang)


def kernel(x_prompt, x_sample, mem_prompt, cache_mem_k, cache_mem_v, state_ret, cache_pool, cache_ffn_conv, w_ret_in, ret_gn, w_ret_out, pool_w, pool_scale, norm_mem, w_xq, w_xk, w_xv, w_xo, w_up, conv_w, conv_b, w_down, norm_mix, norm_xattn, norm_ffn, norm_final):
    bp, lp, d = x_prompt.shape
    bs, ls, _ = x_sample.shape
    depth = w_up.shape[0]
    n_mem = mem_prompt.shape[1]

    w_in_b, w_out_b, pool_w_b, wq_b, wk_b, wv_b, wo_b, w_up_b, w_down_b = (
        w.astype(BF16) for w in (w_ret_in, w_ret_out, pool_w, w_xq, w_xk, w_xv, w_xo, w_up, w_down))
    cwb = jnp.concatenate([conv_w, conv_b[:, None, :]], axis=1)
    g_mix, g_x, g_ffn = _gain(norm_mix), _gain(norm_xattn), _gain(norm_ffn)
    gn_g, p_scale = _gain(ret_gn), _gain(pool_scale)

    mem_k2, mem_v2 = _mem_kv(mem_prompt.reshape(bp * n_mem, d), norm_mem, wk_b, wv_b)
    kv_p = (mem_k2.reshape(depth * bp, n_mem, d), mem_v2.reshape(depth * bp, n_mem, d))
    kv_s = (cache_mem_k.reshape(depth * bs, n_mem, d), cache_mem_v.reshape(depth * bs, n_mem, d))
    states = state_ret.reshape((state_ret.shape[0] * bs,) + state_ret.shape[2:])

    cos_p, sin_p = _rope_tables(jnp.arange(lp, dtype=F32))
    cos_s, sin_s = _rope_tables(PAST_LEN + jnp.arange(ls, dtype=F32))
    cos_s, sin_s = jnp.tile(cos_s, (bs, 1)), jnp.tile(sin_s, (bs, 1))

    xp = x_prompt.reshape(bp * lp, d)
    xs = x_sample.reshape(bs * ls, d)
    ret_p, ret_s, pool_p, pool_s, conv_p, conv_s = [], [], [], [], [], []
    for i in range(depth):
        j = i // 2
        last = i == depth - 1
        if i % 2 == 0:
            q, k, v, gate = _ret_proj(xp, g_mix, i, cos_p, sin_p, w_in_b, j, bp, BF16)
            o, s = _ret_prompt(q, k, v, bp)
            xp = _ret_out(o, gate, xp, gn_g, w_out_b, j)
            ret_p.append(s)
            q, k, v, gate = _ret_proj(xs, g_mix, i, cos_s, sin_s, w_in_b, j, 1, F32)
            o, s = _ret_sample(q, k, v, states, j, bs, ls)
            xs = _ret_out(o, gate, xs, gn_g, w_out_b, j)
            ret_s.append(s)
        else:
            xp, hist = _pool_prompt(xp, g_mix, i, pool_w_b, p_scale, j, bp)
            pool_p.append(hist[:, POOL_CARRY - POOL_BUF:])
            buf_t = jnp.swapaxes(cache_pool[j], 0, 1).reshape(POOL_BUF * bs, d)
            xs, hs = _pool_sample(xs, g_mix, i, pool_w_b, p_scale, j, buf_t, bs, PAST_LEN)
            pool_s.append(jnp.concatenate([cache_pool[j], hs.reshape(bs, ls, d)], axis=1)[:, -POOL_BUF:])

        xp = _xattn(xp, g_x, wq_b, wo_b, i, *kv_p, bp, lp)
        xs = _xattn(xs, g_x, wq_b, wo_b, i, *kv_s, bs, ls)

        xp, tail = _ffn_prompt(xp, g_ffn, norm_final, w_up_b, w_down_b, cwb, i, bp, last)
        conv_p.append(tail)
        buf = jnp.swapaxes(cache_ffn_conv[i], 0, 1).reshape((CONV_W - 1) * bs, 2 * D_FF)
        xs, ca, cg = _ffn_sample(xs, g_ffn, norm_final, w_up_b, w_down_b, cwb, i, buf, bs, last)
        tail = jnp.concatenate([ca, cg], axis=-1)
        conv_s.append(jnp.swapaxes(tail.reshape(CONV_W - 1, bs, 2 * D_FF), 0, 1))

    mk = mem_k2.reshape(depth, bp, n_mem, X_HEADS, X_HEAD_DIM)
    mv = mem_v2.reshape(depth, bp, n_mem, X_HEADS, X_HEAD_DIM)
    return (xp.reshape(bp, lp, d), xs.reshape(bs, ls, d), jnp.stack(ret_p), jnp.stack(ret_s).astype(state_ret.dtype),
            jnp.stack(pool_p), jnp.stack(pool_s), jnp.stack(conv_p), jnp.stack(conv_s), mk, mv)
```

```python
import functools

import jax
import jax.numpy as jnp
from jax import lax
from jax.experimental import pallas as pl
from jax.experimental.pallas import tpu as pltpu

F32 = jnp.float32
BF16 = jnp.bfloat16

D_MODEL = 1024
PAST_LEN = 16384
RET_HEADS = 4
RET_DK = D_MODEL // RET_HEADS
RET_DV = 2 * D_MODEL // RET_HEADS
HK = RET_HEADS * RET_DK
HV = RET_HEADS * RET_DV
ROPE_BASE = 10000.0
POOL_WINDOWS = (2, 4, 8, 16)
POOL_GC = D_MODEL // len(POOL_WINDOWS)
POOL_BUF = max(POOL_WINDOWS) - 1
X_HEADS = 4
X_HEAD_DIM = D_MODEL // X_HEADS
D_FF = 2816
CONV_W = 3
NORM_EPS = 1e-6
GN_EPS = 1e-5

SUBLANES = 8
LANES = 128
KV_LANE_TILES = X_HEAD_DIM // LANES
KV_ROW_STRIDE = X_HEADS * KV_LANE_TILES
assert KV_ROW_STRIDE == SUBLANES
ROW_TILE = 512
RET_CHUNK_PROMPT = 256
FF_CHUNK = 256
NF = D_FF // FF_CHUNK
FFN_GATE_ROWS = 64
POOL_CARRY = 16
VMEM_LIMIT = 56 * 1024 * 1024


def _params(sem):
    return pltpu.CompilerParams(dimension_semantics=sem, vmem_limit_bytes=VMEM_LIMIT)


def _resident(shape):
    zeros = (0,) * len(shape)
    return pl.BlockSpec(shape, lambda *_: zeros, pipeline_mode=pl.Buffered(1))


def _layer_resident(stacked_shape, layer):
    idx = (layer,) + (0,) * (len(stacked_shape) - 1)
    return pl.BlockSpec((1,) + tuple(stacked_shape[1:]), lambda *_: idx, pipeline_mode=pl.Buffered(1))


def _gain(g):
    return g.reshape(g.shape[0], 1, g.shape[1])


def _dot(a, b):
    return jnp.dot(a, b, preferred_element_type=F32)


def _dot_nt(a, b):
    return lax.dot_general(a, b, (((1,), (1,)), ((), ())), preferred_element_type=F32)


def _dot_tn(a, b):
    return lax.dot_general(a, b, (((0,), (0,)), ((), ())), preferred_element_type=F32)


def _rms(x, g):
    return x * lax.rsqrt(jnp.mean(x * x, axis=-1, keepdims=True) + NORM_EPS) * g


def _silu(x):
    return x * (1.0 / (1.0 + jnp.exp(-x)))


def _memkv_body(mem_ref, g_ref, wk_ref, wv_ref, k_ref, v_ref):
    h = _rms(mem_ref[...], g_ref[0]).astype(BF16)
    for w_ref, o_ref in ((wk_ref, k_ref), (wv_ref, v_ref)):
        y = _dot(h, w_ref[0])
        for hd in range(X_HEADS):
            for c in range(KV_LANE_TILES):
                col = hd * X_HEAD_DIM + c * LANES
                o_ref[0, pl.ds(c * X_HEADS + hd, y.shape[0], stride=KV_ROW_STRIDE), :] = y[:, col:col + LANES]


def _mem_kv(mem2d, norm_mem, wk, wv):
    rows, d = mem2d.shape
    depth = wk.shape[0]
    tile = min(rows, 1024)
    w_spec = pl.BlockSpec((1, d, d), lambda i, r: (i, 0, 0))
    o_spec = pl.BlockSpec((1, tile * KV_ROW_STRIDE, LANES), lambda i, r: (i, r, 0))
    return pl.pallas_call(
        _memkv_body,
        grid=(depth, rows // tile),
        in_specs=[pl.BlockSpec((tile, d), lambda i, r: (r, 0)),
                  pl.BlockSpec((1, 1, d), lambda i, r: (i, 0, 0)), w_spec, w_spec],
        out_specs=[o_spec, o_spec],
        out_shape=[jax.ShapeDtypeStruct((depth, rows * KV_ROW_STRIDE, LANES), F32)] * 2,
        compiler_params=_params(("arbitrary", "arbitrary")),
        name="mem_kv",
    )(mem2d, _gain(norm_mem), wk, wv)


def _kv_rows(a):
    lead, n_mem = a.shape[:-3], a.shape[-3]
    a = a.reshape(lead + (n_mem, X_HEADS, KV_LANE_TILES, LANES))
    a = jnp.swapaxes(a, -3, -2)
    return a.reshape((-1, n_mem * KV_ROW_STRIDE, LANES))


def _kv_unrows(a, lead, n_mem):
    a = a.reshape(lead + (n_mem, KV_LANE_TILES, X_HEADS, LANES))
    a = jnp.swapaxes(a, -3, -2)
    return a.reshape(lead + (n_mem, X_HEADS, X_HEAD_DIM))


def _head_kv(ref, i, hd, n_mem):
    parts = [ref[i, pl.ds(c * X_HEADS + hd, n_mem, stride=KV_ROW_STRIDE), :] for c in range(KV_LANE_TILES)]
    return jnp.concatenate(parts, axis=1).astype(BF16)


def _retproj_body(x_ref, g_ref, cos_ref, sin_ref, w_ref, q_ref, k_ref, v_ref, gate_ref):
    h = _rms(x_ref[...], g_ref[0]).astype(BF16)
    cos = cos_ref[...]
    sin = sin_ref[...]
    half = RET_DK // 2

    def rope_store(col0, out_ref, scale):
        p = _dot(h, w_ref[0, :, col0:col0 + HK])
        for hd in range(RET_HEADS):
            a = p[:, hd * RET_DK: hd * RET_DK + half]
            b = p[:, hd * RET_DK + half: (hd + 1) * RET_DK]
            out_ref[:, hd * RET_DK: hd * RET_DK + half] = ((a * cos - b * sin) * scale).astype(out_ref.dtype)
            out_ref[:, hd * RET_DK + half: (hd + 1) * RET_DK] = ((b * cos + a * sin) * scale).astype(out_ref.dtype)

    rope_store(0, q_ref, 1.0)
    rope_store(HK, k_ref, RET_DK ** -0.5)
    v_ref[...] = _dot(h, w_ref[0, :, 2 * HK: 2 * HK + HV]).astype(v_ref.dtype)
    gate_ref[...] = _dot(h, w_ref[0, :, 2 * HK + HV:]).astype(gate_ref.dtype)


def _ret_proj(x2d, gains, layer, cos, sin, w_in, ret_layer, n_seq, out_dtype):
    rows, d = x2d.shape
    seq_rows = rows // n_seq
    tile = min(seq_rows, ROW_TILE)
    tps = seq_rows // tile
    row_spec = lambda w: pl.BlockSpec((tile, w), lambda b, t: (b * tps + t, 0))
    tab_spec = pl.BlockSpec((tile, RET_DK // 2), lambda b, t: (t, 0))
    return pl.pallas_call(
        _retproj_body,
        grid=(n_seq, tps),
        in_specs=[row_spec(d), _layer_resident(gains.shape, layer), tab_spec, tab_spec,
                  _layer_resident(w_in.shape, ret_layer)],
        out_specs=[row_spec(HK), row_spec(HK), row_spec(HV), row_spec(HV)],
        out_shape=[jax.ShapeDtypeStruct((rows, HK), out_dtype), jax.ShapeDtypeStruct((rows, HK), out_dtype),
                   jax.ShapeDtypeStruct((rows, HV), out_dtype), jax.ShapeDtypeStruct((rows, HV), out_dtype)],
        compiler_params=_params(("arbitrary", "arbitrary")),
        name="ret_proj",
    )(x2d, gains, cos, sin, w_in)


def _ret_step(qh, kh, vh, s_old, inner, qdec, kdec, cdec):
    qb = qh.astype(BF16)
    kb = kh.astype(BF16)
    vb = vh.astype(BF16)
    p = (_dot_nt(qb, kb) * inner).astype(BF16)
    o = _dot(p, vb) + _dot(qb, s_old.astype(BF16)) * qdec
    kd = (kh.astype(F32) * kdec).astype(BF16)
    s_new = s_old * cdec + _dot_tn(kd, vb)
    return o, s_new


def _decay_tables(chunk):
    lg = jnp.log(1.0 - 2.0 ** (-5.0 - jnp.arange(RET_HEADS, dtype=F32)))
    idx = jnp.arange(chunk, dtype=F32)
    rel = idx[:, None] - idx[None, :]
    inner = jnp.where(rel[None] >= 0, jnp.exp(jnp.maximum(rel, 0.0)[None] * lg[:, None, None]), 0.0)
    q_dec = jnp.exp((idx + 1.0)[None, :] * lg[:, None])[..., None]
    k_dec = jnp.exp((chunk - 1.0 - idx)[None, :] * lg[:, None])[..., None]
    c_dec = jnp.exp(chunk * lg)[:, None, None]
    return (inner, jnp.broadcast_to(q_dec, (RET_HEADS, chunk, RET_DV)),
            jnp.broadcast_to(k_dec, (RET_HEADS, chunk, RET_DK)),
            jnp.broadcast_to(c_dec, (RET_HEADS, 1, RET_DV)))


def _ret_prompt_body(q_ref, k_ref, v_ref, inner_ref, qdec_ref, kdec_ref, cdec_ref, o_ref, s_ref, *, chunk):
    @pl.when(pl.program_id(1) == 0)
    def _():
        s_ref[...] = jnp.zeros_like(s_ref)

    for c in range(q_ref.shape[0] // chunk):
        rows = slice(c * chunk, (c + 1) * chunk)
        for h in range(RET_HEADS):
            o, s_new = _ret_step(q_ref[rows, h * RET_DK:(h + 1) * RET_DK], k_ref[rows, h * RET_DK:(h + 1) * RET_DK],
                                 v_ref[rows, h * RET_DV:(h + 1) * RET_DV], s_ref[0, h],
                                 inner_ref[h], qdec_ref[h], kdec_ref[h], cdec_ref[h])
            o_ref[rows, h * RET_DV:(h + 1) * RET_DV] = o
            s_ref[0, h] = s_new


def _ret_prompt(q, k, v, n_seq):
    rows = q.shape[0]
    seq_rows = rows // n_seq
    tile = min(seq_rows, ROW_TILE)
    chunk = min(tile, RET_CHUNK_PROMPT)
    tps = seq_rows // tile
    tabs = _decay_tables(chunk)
    row_spec = lambda w: pl.BlockSpec((tile, w), lambda b, t: (b * tps + t, 0))
    return pl.pallas_call(
        functools.partial(_ret_prompt_body, chunk=chunk),
        grid=(n_seq, tps),
        in_specs=[row_spec(HK), row_spec(HK), row_spec(HV)] + [_resident(t.shape) for t in tabs],
        out_specs=[row_spec(HV), pl.BlockSpec((1, RET_HEADS, RET_DK, RET_DV), lambda b, t: (b, 0, 0, 0))],
        out_shape=[jax.ShapeDtypeStruct((rows, HV), F32),
                   jax.ShapeDtypeStruct((n_seq, RET_HEADS, RET_DK, RET_DV), F32)],
        compiler_params=_params(("arbitrary", "arbitrary")),
        name="ret_prompt",
    )(q, k, v, *tabs)


def _ret_sample_body(q_ref, k_ref, v_ref, s0_ref, inner_ref, qdec_ref, kdec_ref, cdec_ref, o_ref, s_ref, *, seq_rows):
    for i in range(s0_ref.shape[0]):
        rows = slice(i * seq_rows, (i + 1) * seq_rows)
        for h in range(RET_HEADS):
            o, s_new = _ret_step(q_ref[rows, h * RET_DK:(h + 1) * RET_DK], k_ref[rows, h * RET_DK:(h + 1) * RET_DK],
                                 v_ref[rows, h * RET_DV:(h + 1) * RET_DV], s0_ref[i, h],
                                 inner_ref[h], qdec_ref[h], kdec_ref[h], cdec_ref[h])
            o_ref[rows, h * RET_DV:(h + 1) * RET_DV] = o
            s_ref[i, h] = s_new


def _ret_sample(q, k, v, states, ret_layer, n_seq, seq_rows):
    bb = 4 if n_seq % 4 == 0 else 1
    steps = n_seq // bb
    tabs = _decay_tables(seq_rows)
    row_spec = lambda w: pl.BlockSpec((bb * seq_rows, w), lambda i: (i, 0))
    st_block = (bb, RET_HEADS, RET_DK, RET_DV)
    return pl.pallas_call(
        functools.partial(_ret_sample_body, seq_rows=seq_rows),
        grid=(steps,),
        in_specs=[row_spec(HK), row_spec(HK), row_spec(HV),
                  pl.BlockSpec(st_block, lambda i: (ret_layer * steps + i, 0, 0, 0))] + [_resident(t.shape) for t in tabs],
        out_specs=[row_spec(HV), pl.BlockSpec(st_block, lambda i: (i, 0, 0, 0))],
        out_shape=[jax.ShapeDtypeStruct((n_seq * seq_rows, HV), F32),
                   jax.ShapeDtypeStruct((n_seq, RET_HEADS, RET_DK, RET_DV), F32)],
        compiler_params=_params(("arbitrary",)),
        name="ret_sample",
    )(q, k, v, states, *tabs)


def _retout_body(o_ref, gate_ref, x_ref, gn_ref, w_ref, out_ref):
    parts = []
    for h in range(RET_HEADS):
        oh = o_ref[:, h * RET_DV:(h + 1) * RET_DV]
        mu = jnp.mean(oh, axis=-1, keepdims=True)
        dlt = oh - mu
        var = jnp.mean(dlt * dlt, axis=-1, keepdims=True)
        parts.append(dlt * lax.rsqrt(var + GN_EPS))
    on = jnp.concatenate(parts, axis=-1) * gn_ref[0]
    z = (_silu(gate_ref[...].astype(F32)) * on).astype(BF16)
    out_ref[...] = x_ref[...] + _dot(z, w_ref[0])


def _ret_out(o, gate, x2d, gn_g, w_out, ret_layer):
    rows, d = x2d.shape
    tile = min(rows, ROW_TILE)
    row_spec = lambda w: pl.BlockSpec((tile, w), lambda r: (r, 0))
    return pl.pallas_call(
        _retout_body,
        grid=(rows // tile,),
        in_specs=[row_spec(HV), row_spec(HV), row_spec(d), _layer_resident(gn_g.shape, ret_layer),
                  _layer_resident(w_out.shape, ret_layer)],
        out_specs=row_spec(d),
        out_shape=jax.ShapeDtypeStruct((rows, d), F32),
        compiler_params=_params(("arbitrary",)),
        name="ret_out",
    )(o, gate, x2d, gn_g, w_out)


def _softmax(s):
    e = jnp.exp(s - jnp.max(s, axis=-1, keepdims=True))
    return e / jnp.sum(e, axis=-1, keepdims=True)


def _xattn_body(x_ref, g_ref, wq_ref, wo_ref, k_ref, v_ref, out_ref, att_ref, *s_refs, seq_rows):
    x = x_ref[...]
    h = _rms(x, g_ref[0]).astype(BF16)
    q = _dot(h, wq_ref[0]) * (X_HEAD_DIM ** -0.5)
    n_mem = k_ref.shape[1] // KV_ROW_STRIDE
    pairs = [(slice(i * seq_rows, (i + 1) * seq_rows), slice(hd * X_HEAD_DIM, (hd + 1) * X_HEAD_DIM), i, hd)
             for i in range(k_ref.shape[0]) for hd in range(X_HEADS)]

    def scores(rows, cols, i, hd):
        return _dot_nt(q[rows, cols].astype(BF16), _head_kv(k_ref, i, hd, n_mem))

    if s_refs:
        s_ref, = s_refs
        for n, pair in enumerate(pairs):
            s_ref[n * seq_rows:(n + 1) * seq_rows] = scores(*pair)
        s_ref[...] = _softmax(s_ref[...])
        for n, (rows, cols, i, hd) in enumerate(pairs):
            p = s_ref[n * seq_rows:(n + 1) * seq_rows]
            att_ref[rows, cols] = _dot(p.astype(BF16), _head_kv(v_ref, i, hd, n_mem))
    else:
        for rows, cols, i, hd in pairs:
            p = _softmax(scores(rows, cols, i, hd))
            att_ref[rows, cols] = _dot(p.astype(BF16), _head_kv(v_ref, i, hd, n_mem))
    out_ref[...] = x + _dot(att_ref[...].astype(BF16), wo_ref[0])


def _xattn(x2d, gains, wq, wo, layer, mem_k, mem_v, n_seq, seq_rows):
    rows, d = x2d.shape
    n_mem = mem_k.shape[1]
    if seq_rows >= ROW_TILE:
        tile, body_rows = ROW_TILE, ROW_TILE
        tps = seq_rows // tile
        grid = (n_seq, tps)
        row_spec = pl.BlockSpec((tile, d), lambda b, t: (b * tps + t, 0))
        kv_spec = pl.BlockSpec((1, n_mem, LANES), lambda b, t: (layer * n_seq + b, 0, 0))
        sem = ("arbitrary", "arbitrary")
        score_scratch = []
    else:
        bb = 4 if n_seq % 4 == 0 else 1
        tile, body_rows = bb * seq_rows, seq_rows
        steps = n_seq // bb
        grid = (steps,)
        row_spec = pl.BlockSpec((tile, d), lambda i: (i, 0))
        kv_spec = pl.BlockSpec((bb, n_mem, LANES), lambda i: (layer * steps + i, 0, 0))
        sem = ("arbitrary",)
        score_scratch = [pltpu.VMEM((bb * X_HEADS * seq_rows, n_mem // KV_ROW_STRIDE), F32)]
    return pl.pallas_call(
        functools.partial(_xattn_body, seq_rows=body_rows),
        grid=grid,
        in_specs=[row_spec, _layer_resident(gains.shape, layer), _layer_resident(wq.shape, layer),
                  _layer_resident(wo.shape, layer), kv_spec, kv_spec],
        out_specs=row_spec,
        out_shape=jax.ShapeDtypeStruct((rows, d), F32),
        scratch_shapes=[pltpu.VMEM((tile, d), F32)] + score_scratch,
        compiler_params=_params(sem),
        name="xattn",
    )(x2d, gains, wq, wo, mem_k, mem_v)


def _conv_chunk(u, prev, cwb, shift):
    c = cwb[3:4] + cwb[0:1] * shift(u, prev, 2)
    c = c + cwb[1:2] * shift(u, prev, 1)
    return c + cwb[2:3] * u


def _ffn_gate(ua, ug, prev_a, prev_g, cwb_a, cwb_g, shift):
    return (_conv_chunk(ua, prev_a, cwb_a, shift) * _silu(_conv_chunk(ug, prev_g, cwb_g, shift))).astype(BF16)


def _ffn_chunk(h, wa, wg, wd, prev_a, prev_g, cwb_a, cwb_g, shift):
    ua = _dot(h, wa)
    ug = _dot(h, wg)
    return _dot(_ffn_gate(ua, ug, prev_a, prev_g, cwb_a, cwb_g, shift), wd), ua, ug


def _ffn_prompt_body(x_ref, g_ref, gf_ref, wu_ref, wd_ref, cwb_ref, out_ref, c_ref,
                     h_ref, acc_ref, ua0_ref, ug0_ref, ua1_ref, ug1_ref, z0_ref, z1_ref, *, final_norm):
    halo = (CONV_W - 1) * SUBLANES
    tile = acc_ref.shape[0]
    groups = tile // SUBLANES

    @pl.when(pl.program_id(1) == 0)
    def _():
        c_ref[...] = jnp.zeros_like(c_ref)

    x = jnp.swapaxes(x_ref[...].reshape(SUBLANES, groups, D_MODEL), 0, 1).reshape(tile, D_MODEL)
    h_ref[...] = _rms(x, g_ref[0]).astype(BF16)
    acc_ref[...] = x
    slots = ((ua0_ref, ug0_ref), (ua1_ref, ug1_ref))
    first_sublane = lax.broadcasted_iota(jnp.int32, (SUBLANES, FF_CHUNK), 0) == 0

    def cols(f):
        return (slice(f * FF_CHUNK, (f + 1) * FF_CHUNK), slice(D_FF + f * FF_CHUNK, D_FF + (f + 1) * FF_CHUNK))

    def up(f):
        for u_ref, cc in zip(slots[f % 2], cols(f)):
            u = _dot(h_ref[...], wu_ref[0, :, cc])
            u_ref[halo:] = u
            for m in range(CONV_W - 1):
                rows = slice(m * SUBLANES, (m + 1) * SUBLANES)
                src = u[tile - halo + m * SUBLANES: tile - halo + (m + 1) * SUBLANES]
                u_ref[rows] = jnp.where(first_sublane, pltpu.roll(c_ref[0, rows, cc], 1, axis=0),
                                        pltpu.roll(src, 1, axis=0))
                c_ref[0, rows, cc] = src

    def conv(u_ref, cw, r0, n):
        c = cw[3:4] + cw[0:1] * u_ref[r0: r0 + n]
        c = c + cw[1:2] * u_ref[r0 + SUBLANES: r0 + SUBLANES + n]
        return c + cw[2:3] * u_ref[r0 + halo: r0 + halo + n]

    def gate(f):
        ca, cg = cols(f)
        ua_ref, ug_ref = slots[f % 2]
        cwa, cwg = cwb_ref[0, :, ca], cwb_ref[0, :, cg]
        z_ref = (z0_ref, z1_ref)[f % 2]
        for r0 in range(0, tile, FFN_GATE_ROWS):
            z_ref[r0: r0 + FFN_GATE_ROWS] = (conv(ua_ref, cwa, r0, FFN_GATE_ROWS)
                                             * _silu(conv(ug_ref, cwg, r0, FFN_GATE_ROWS))).astype(BF16)
        return z_ref

    up(0)
    for f in range(NF):
        if f + 1 < NF:
            up(f + 1)
        z_ref = gate(f)
        acc_ref[...] += _dot(z_ref[...], wd_ref[0, cols(f)[0], :])
    y = acc_ref[...]
    if final_norm:
        y = _rms(y, gf_ref[...])
    out_ref[...] = jnp.swapaxes(y.reshape(groups, SUBLANES, D_MODEL), 0, 1).reshape(tile, D_MODEL)


def _ffn_prompt(x2d, gains, g_final, w_up, w_down, cwb, layer, n_seq, final_norm):
    rows, d = x2d.shape
    seq_rows = rows // n_seq
    tile = min(seq_rows, ROW_TILE)
    tps = seq_rows // tile
    halo = (CONV_W - 1) * SUBLANES
    row_spec = pl.BlockSpec((tile, d), lambda b, t: (b * tps + t, 0))
    u_scratch = pltpu.VMEM((halo + tile, FF_CHUNK), F32)
    z_scratch = pltpu.VMEM((tile, FF_CHUNK), BF16)
    y, carry = pl.pallas_call(
        functools.partial(_ffn_prompt_body, final_norm=final_norm),
        grid=(n_seq, tps),
        in_specs=[row_spec, _layer_resident(gains.shape, layer), _resident((1, d)),
                  _layer_resident(w_up.shape, layer), _layer_resident(w_down.shape, layer),
                  _layer_resident(cwb.shape, layer)],
        out_specs=[row_spec, pl.BlockSpec((1, halo, 2 * D_FF), lambda b, t: (b, 0, 0))],
        out_shape=[jax.ShapeDtypeStruct((rows, d), F32), jax.ShapeDtypeStruct((n_seq, halo, 2 * D_FF), F32)],
        scratch_shapes=[pltpu.VMEM((tile, d), BF16), pltpu.VMEM((tile, d), F32),
                        u_scratch, u_scratch, u_scratch, u_scratch, z_scratch, z_scratch],
        compiler_params=_params(("arbitrary", "arbitrary")),
        name="ffn_prompt",
    )(x2d, gains, g_final.reshape(1, d), w_up, w_down, cwb)
    return y, carry[:, SUBLANES - 1::SUBLANES]


def _ffn_sample_body(x_ref, g_ref, gf_ref, wa_ref, wg_ref, wd_ref, cwa_ref, cwg_ref, ba_ref, bg_ref,
                     out_ref, ca_ref, cg_ref, h_ref, acc_ref, *, n_seq, seq_rows, final_norm):
    f = pl.program_id(0)

    @pl.when(f == 0)
    def _():
        xt = jnp.concatenate([x_ref[:, t * D_MODEL:(t + 1) * D_MODEL] for t in range(seq_rows)], axis=0)
        h_ref[...] = _rms(xt, g_ref[0]).astype(BF16)
        acc_ref[...] = xt

    rows = n_seq * seq_rows
    hist = (CONV_W - 1) * n_seq

    def shift(u, prev, j):
        return jnp.concatenate([prev, u], axis=0)[hist - j * n_seq: hist - j * n_seq + rows]

    y, ua, ug = _ffn_chunk(h_ref[...], wa_ref[0], wg_ref[0], wd_ref[0], ba_ref[...], bg_ref[...],
                           cwa_ref[0], cwg_ref[0], shift)
    acc_ref[...] += y
    ca_ref[...] = ua[rows - hist:]
    cg_ref[...] = ug[rows - hist:]

    @pl.when(f == pl.num_programs(0) - 1)
    def _():
        yt = acc_ref[...]
        if final_norm:
            yt = _rms(yt, gf_ref[...])
        for t in range(seq_rows):
            out_ref[:, t * D_MODEL:(t + 1) * D_MODEL] = yt[t * n_seq:(t + 1) * n_seq]


def _ffn_sample(x2d, gains, g_final, w_up, w_down, cwb, layer, buf, n_seq, final_norm):
    rows, d = x2d.shape
    seq_rows = rows // n_seq
    hist = (CONV_W - 1) * n_seq
    a_col = lambda f: f
    g_col = lambda f: NF + f
    c_shape = jax.ShapeDtypeStruct((hist, D_FF), F32)
    c_spec = pl.BlockSpec((hist, FF_CHUNK), lambda f: (0, f))
    wide = (n_seq, seq_rows * d)
    y, ca, cg = pl.pallas_call(
        functools.partial(_ffn_sample_body, n_seq=n_seq, seq_rows=seq_rows, final_norm=final_norm),
        grid=(NF,),
        in_specs=[_resident(wide), _layer_resident(gains.shape, layer), _resident((1, d)),
                  pl.BlockSpec((1, d, FF_CHUNK), lambda f: (layer, 0, a_col(f))),
                  pl.BlockSpec((1, d, FF_CHUNK), lambda f: (layer, 0, g_col(f))),
                  pl.BlockSpec((1, FF_CHUNK, d), lambda f: (layer, f, 0)),
                  pl.BlockSpec((1, 4, FF_CHUNK), lambda f: (layer, 0, a_col(f))),
                  pl.BlockSpec((1, 4, FF_CHUNK), lambda f: (layer, 0, g_col(f))),
                  pl.BlockSpec((hist, FF_CHUNK), lambda f: (0, a_col(f))),
                  pl.BlockSpec((hist, FF_CHUNK), lambda f: (0, g_col(f)))],
        out_specs=[pl.BlockSpec(wide, lambda f: (0, 0)), c_spec, c_spec],
        out_shape=[jax.ShapeDtypeStruct(wide, F32), c_shape, c_shape],
        scratch_shapes=[pltpu.VMEM((rows, d), BF16), pltpu.VMEM((rows, d), F32)],
        compiler_params=_params(("arbitrary",)),
        name="ffn_sample",
    )(x2d.reshape(wide), gains, g_final.reshape(1, d), w_up, w_up, w_down, cwb, cwb, buf, buf)
    return y.reshape(rows, d), ca, cg


def _pool_mix(pooled, w_ref, scale_ref, x):
    mixed = [_dot(pooled[g].astype(BF16), w_ref[0, g]) for g in range(len(POOL_WINDOWS))]
    return x + jnp.concatenate(mixed, axis=-1) * scale_ref[0]


def _pool_prompt_body(x_ref, g_ref, w_ref, scale_ref, out_ref, hist_ref):
    t = pl.program_id(1)

    @pl.when(t == 0)
    def _():
        hist_ref[...] = jnp.zeros_like(hist_ref)

    x = x_ref[...]
    tile = x.shape[0]
    h = _rms(x, g_ref[0])
    ext = jnp.concatenate([hist_ref[0], h], axis=0)
    hist_ref[0] = h[tile - POOL_CARRY:]
    pos1 = (t * tile + 1 + lax.broadcasted_iota(jnp.int32, (tile, POOL_GC), 0)).astype(F32)
    pooled = []
    s = ext
    for g, w in enumerate(POOL_WINDOWS):
        s = s[:, POOL_GC * (1 if g else 0):]
        s = s + pltpu.roll(s, w // 2, axis=0)
        cnt = jnp.minimum(pos1, float(w))
        pooled.append(s[POOL_CARRY:, :POOL_GC] / cnt - h[:, g * POOL_GC:(g + 1) * POOL_GC])
    out_ref[...] = _pool_mix(pooled, w_ref, scale_ref, x)


def _pool_prompt(x2d, gains, layer, pool_w, pool_scale, pool_layer, n_seq):
    rows, d = x2d.shape
    seq_rows = rows // n_seq
    tile = min(seq_rows, ROW_TILE)
    tps = seq_rows // tile
    row_spec = pl.BlockSpec((tile, d), lambda b, t: (b * tps + t, 0))
    return pl.pallas_call(
        _pool_prompt_body,
        grid=(n_seq, tps),
        in_specs=[row_spec, _layer_resident(gains.shape, layer), _layer_resident(pool_w.shape, pool_layer),
                  _layer_resident(pool_scale.shape, pool_layer)],
        out_specs=[row_spec, pl.BlockSpec((1, POOL_CARRY, d), lambda b, t: (b, 0, 0))],
        out_shape=[jax.ShapeDtypeStruct((rows, d), F32), jax.ShapeDtypeStruct((n_seq, POOL_CARRY, d), F32)],
        compiler_params=_params(("arbitrary", "arbitrary")),
        name="pool_prompt",
    )(x2d, gains, pool_w, pool_scale)


def _pool_sample_body(x_ref, g_ref, w_ref, scale_ref, buf_ref, out_ref, h_ref, *, n_seq, seq_rows, pos0):
    xt = jnp.concatenate([x_ref[:, t * D_MODEL:(t + 1) * D_MODEL] for t in range(seq_rows)], axis=0)
    h = _rms(xt, g_ref[0])
    s = jnp.concatenate([buf_ref[...], h], axis=0)
    rows = n_seq * seq_rows
    first = POOL_BUF
    pooled = []
    for g, w in enumerate(POOL_WINDOWS):
        s = s[:, POOL_GC * (1 if g else 0):]
        step = (w // 2) * n_seq
        s = s[step:] + s[:-step]
        first -= w // 2
        win = s[first * n_seq: first * n_seq + rows, :POOL_GC]
        inv = [1.0 / min(pos0 + t + 1, w) for t in range(seq_rows)]
        if len(set(inv)) == 1:
            win = win * inv[0]
        else:
            win = jnp.concatenate([win[t * n_seq:(t + 1) * n_seq] * inv[t] for t in range(seq_rows)], axis=0)
        pooled.append(win - h[:, g * POOL_GC:(g + 1) * POOL_GC])
    yt = _pool_mix(pooled, w_ref, scale_ref, xt)
    for t in range(seq_rows):
        out_ref[:, t * D_MODEL:(t + 1) * D_MODEL] = yt[t * n_seq:(t + 1) * n_seq]
        h_ref[:, t * D_MODEL:(t + 1) * D_MODEL] = h[t * n_seq:(t + 1) * n_seq]


def _pool_sample(x2d, gains, layer, pool_w, pool_scale, pool_layer, buf_t, n_seq, pos0):
    rows, d = x2d.shape
    seq_rows = rows // n_seq
    full = lambda s: pl.BlockSpec(s, lambda i: (0,) * len(s))
    wide = (n_seq, seq_rows * d)
    y, h = pl.pallas_call(
        functools.partial(_pool_sample_body, n_seq=n_seq, seq_rows=seq_rows, pos0=pos0),
        grid=(1,),
        in_specs=[full(wide), _layer_resident(gains.shape, layer), _layer_resident(pool_w.shape, pool_layer),
                  _layer_resident(pool_scale.shape, pool_layer), full(buf_t.shape)],
        out_specs=[full(wide), full(wide)],
        out_shape=[jax.ShapeDtypeStruct(wide, F32)] * 2,
        compiler_params=_params(("arbitrary",)),
        name="pool_sample",
    )(x2d.reshape(wide), gains, pool_w, pool_scale, buf_t)
    return y.reshape(rows, d), h.reshape(rows, d)


def _rope_tables(pos):
    inv = 1.0 / (ROPE_BASE ** (jnp.arange(0, RET_DK, 2, dtype=F32) / RET_DK))
    ang = pos[:, None] * inv[None, :]
    return jnp.cos(ang), jnp.sin(ang)


def kernel(x_prompt, x_sample, mem_prompt, cache_mem_k, cache_mem_v, state_ret, cache_pool, cache_ffn_conv, w_ret_in, ret_gn, w_ret_out, pool_w, pool_scale, norm_mem, w_xq, w_xk, w_xv, w_xo, w_up, conv_w, conv_b, w_down, norm_mix, norm_xattn, norm_ffn, norm_final):
    bp, lp, d = x_prompt.shape
    bs, ls, _ = x_sample.shape
    depth = w_up.shape[0]
    n_mem = mem_prompt.shape[1]

    w_in_b, w_out_b, pool_w_b, wq_b, wk_b, wv_b, wo_b, w_up_b, w_down_b = (
        w.astype(BF16) for w in (w_ret_in, w_ret_out, pool_w, w_xq, w_xk, w_xv, w_xo, w_up, w_down))
    cwb = jnp.concatenate([conv_w, conv_b[:, None, :]], axis=1)
    g_mix, g_x, g_ffn = _gain(norm_mix), _gain(norm_xattn), _gain(norm_ffn)
    gn_g, p_scale = _gain(ret_gn), _gain(pool_scale)

    mem_k2, mem_v2 = _mem_kv(mem_prompt.reshape(bp * n_mem, d), norm_mem, wk_b, wv_b)
    kv_p = tuple(a.reshape(depth * bp, n_mem * KV_ROW_STRIDE, LANES) for a in (mem_k2, mem_v2))
    kv_s = (_kv_rows(cache_mem_k), _kv_rows(cache_mem_v))
    states = state_ret.reshape((state_ret.shape[0] * bs,) + state_ret.shape[2:])

    cos_p, sin_p = _rope_tables(jnp.arange(lp, dtype=F32))
    cos_s, sin_s = _rope_tables(PAST_LEN + jnp.arange(ls, dtype=F32))
    cos_s, sin_s = jnp.tile(cos_s, (bs, 1)), jnp.tile(sin_s, (bs, 1))

    xp = x_prompt.reshape(bp * lp, d)
    xs = x_sample.reshape(bs * ls, d)
    ret_p, ret_s, pool_p, pool_s, conv_p, conv_s = [], [], [], [], [], []
    for i in range(depth):
        j = i // 2
        last = i == depth - 1
        if i % 2 == 0:
            q, k, v, gate = _ret_proj(xp, g_mix, i, cos_p, sin_p, w_in_b, j, bp, BF16)
            o, s = _ret_prompt(q, k, v, bp)
            xp = _ret_out(o, gate, xp, gn_g, w_out_b, j)
            ret_p.append(s)
            q, k, v, gate = _ret_proj(xs, g_mix, i, cos_s, sin_s, w_in_b, j, 1, F32)
            o, s = _ret_sample(q, k, v, states, j, bs, ls)
            xs = _ret_out(o, gate, xs, gn_g, w_out_b, j)
            ret_s.append(s)
        else:
            xp, hist = _pool_prompt(xp, g_mix, i, pool_w_b, p_scale, j, bp)
            pool_p.append(hist[:, POOL_CARRY - POOL_BUF:])
            buf_t = jnp.swapaxes(cache_pool[j], 0, 1).reshape(POOL_BUF * bs, d)
            xs, hs = _pool_sample(xs, g_mix, i, pool_w_b, p_scale, j, buf_t, bs, PAST_LEN)
            pool_s.append(jnp.concatenate([cache_pool[j], hs.reshape(bs, ls, d)], axis=1)[:, -POOL_BUF:])

        xp = _xattn(xp, g_x, wq_b, wo_b, i, *kv_p, bp, lp)
        xs = _xattn(xs, g_x, wq_b, wo_b, i, *kv_s, bs, ls)

        xp, tail = _ffn_prompt(xp, g_ffn, norm_final, w_up_b, w_down_b, cwb, i, bp, last)
        conv_p.append(tail)
        buf = jnp.swapaxes(cache_ffn_conv[i], 0, 1).reshape((CONV_W - 1) * bs, 2 * D_FF)
        xs, ca, cg = _ffn_sample(xs, g_ffn, norm_final, w_up_b, w_down_b, cwb, i, buf, bs, last)
        tail = jnp.concatenate([ca, cg], axis=-1)
        conv_s.append(jnp.swapaxes(tail.reshape(CONV_W - 1, bs, 2 * D_FF), 0, 1))

    mk = _kv_unrows(mem_k2, (depth, bp), n_mem)
    mv = _kv_unrows(mem_v2, (depth, bp), n_mem)
    return (xp.reshape(bp, lp, d), xs.reshape(bs, ls, d), jnp.stack(ret_p), jnp.stack(ret_s).astype(state_ret.dtype),
            jnp.stack(pool_p), jnp.stack(pool_s), jnp.stack(conv_p), jnp.stack(conv_s), mk, mv)
```

```python
import functools

import jax
import jax.numpy as jnp
from jax import lax
from jax.experimental import pallas as pl
from jax.experimental.pallas import tpu as pltpu

F32 = jnp.float32
BF16 = jnp.bfloat16

D_MODEL = 1024
PAST_LEN = 16384
RET_HEADS = 4
RET_DK = D_MODEL // RET_HEADS
RET_DV = 2 * D_MODEL // RET_HEADS
HK = RET_HEADS * RET_DK
HV = RET_HEADS * RET_DV
ROPE_BASE = 10000.0
POOL_WINDOWS = (2, 4, 8, 16)
POOL_GC = D_MODEL // len(POOL_WINDOWS)
POOL_BUF = max(POOL_WINDOWS) - 1
X_HEADS = 4
X_HEAD_DIM = D_MODEL // X_HEADS
D_FF = 2816
CONV_W = 3
NORM_EPS = 1e-6
GN_EPS = 1e-5

SUBLANES = 8
LANES = 128
KV_LANE_TILES = X_HEAD_DIM // LANES
KV_ROW_STRIDE = X_HEADS * KV_LANE_TILES
assert KV_ROW_STRIDE == SUBLANES
ROW_TILE = 512
RET_CHUNK_PROMPT = 256
FF_CHUNK = 256
NF = D_FF // FF_CHUNK
FFN_PROMPT_CHUNK = 512
FFN_GATE_ROWS = 64
POOL_CARRY = 16
VMEM_LIMIT = 56 * 1024 * 1024


def _params(sem):
    return pltpu.CompilerParams(dimension_semantics=sem, vmem_limit_bytes=VMEM_LIMIT)


def _resident(shape):
    zeros = (0,) * len(shape)
    return pl.BlockSpec(shape, lambda *_: zeros, pipeline_mode=pl.Buffered(1))


def _layer_resident(stacked_shape, layer):
    idx = (layer,) + (0,) * (len(stacked_shape) - 1)
    return pl.BlockSpec((1,) + tuple(stacked_shape[1:]), lambda *_: idx, pipeline_mode=pl.Buffered(1))


def _gain(g):
    return g.reshape(g.shape[0], 1, g.shape[1])


def _dot(a, b):
    return jnp.dot(a, b, preferred_element_type=F32)


def _dot_nt(a, b):
    return lax.dot_general(a, b, (((1,), (1,)), ((), ())), preferred_element_type=F32)


def _dot_tn(a, b):
    return lax.dot_general(a, b, (((0,), (0,)), ((), ())), preferred_element_type=F32)


def _rms(x, g):
    return x * lax.rsqrt(jnp.mean(x * x, axis=-1, keepdims=True) + NORM_EPS) * g


def _silu(x):
    return x * (1.0 / (1.0 + jnp.exp(-x)))


def _to_time_major(x, n_seq, seq_rows):
    return jnp.swapaxes(x.reshape(n_seq, seq_rows, x.shape[-1]), 0, 1).reshape(x.shape)


def _from_time_major(x, n_seq, seq_rows):
    return jnp.swapaxes(x.reshape(seq_rows, n_seq, x.shape[-1]), 0, 1).reshape(x.shape)


def _memkv_body(mem_ref, g_ref, wk_ref, wv_ref, k_ref, v_ref):
    h = _rms(mem_ref[...], g_ref[0]).astype(BF16)
    for w_ref, o_ref in ((wk_ref, k_ref), (wv_ref, v_ref)):
        y = _dot(h, w_ref[0])
        for hd in range(X_HEADS):
            for c in range(KV_LANE_TILES):
                col = hd * X_HEAD_DIM + c * LANES
                o_ref[0, pl.ds(c * X_HEADS + hd, y.shape[0], stride=KV_ROW_STRIDE), :] = y[:, col:col + LANES]


def _mem_kv(mem2d, norm_mem, wk, wv):
    rows, d = mem2d.shape
    depth = wk.shape[0]
    tile = min(rows, 1024)
    w_spec = pl.BlockSpec((1, d, d), lambda i, r: (i, 0, 0))
    o_spec = pl.BlockSpec((1, tile * KV_ROW_STRIDE, LANES), lambda i, r: (i, r, 0))
    return pl.pallas_call(
        _memkv_body,
        grid=(depth, rows // tile),
        in_specs=[pl.BlockSpec((tile, d), lambda i, r: (r, 0)),
                  pl.BlockSpec((1, 1, d), lambda i, r: (i, 0, 0)), w_spec, w_spec],
        out_specs=[o_spec, o_spec],
        out_shape=[jax.ShapeDtypeStruct((depth, rows * KV_ROW_STRIDE, LANES), F32)] * 2,
        compiler_params=_params(("arbitrary", "arbitrary")),
        name="mem_kv",
    )(mem2d, _gain(norm_mem), wk, wv)


def _kv_rows(a):
    lead, n_mem = a.shape[:-3], a.shape[-3]
    a = a.reshape(lead + (n_mem, X_HEADS, KV_LANE_TILES, LANES))
    a = jnp.swapaxes(a, -3, -2)
    return a.reshape((-1, n_mem * KV_ROW_STRIDE, LANES))


def _kv_unrows(a, lead, n_mem):
    a = a.reshape(lead + (n_mem, KV_LANE_TILES, X_HEADS, LANES))
    a = jnp.swapaxes(a, -3, -2)
    return a.reshape(lead + (n_mem, X_HEADS, X_HEAD_DIM))


def _head_kv(ref, i, hd, n_mem):
    parts = [ref[i, pl.ds(c * X_HEADS + hd, n_mem, stride=KV_ROW_STRIDE), :] for c in range(KV_LANE_TILES)]
    return jnp.concatenate(parts, axis=1).astype(BF16)


def _retproj_body(x_ref, g_ref, cos_ref, sin_ref, w_ref, q_ref, k_ref, v_ref, gate_ref):
    h = _rms(x_ref[...], g_ref[0]).astype(BF16)
    cos = cos_ref[...]
    sin = sin_ref[...]
    half = RET_DK // 2

    def rope_store(col0, out_ref, scale):
        p = _dot(h, w_ref[0, :, col0:col0 + HK])
        for hd in range(RET_HEADS):
            a = p[:, hd * RET_DK: hd * RET_DK + half]
            b = p[:, hd * RET_DK + half: (hd + 1) * RET_DK]
            out_ref[:, hd * RET_DK: hd * RET_DK + half] = ((a * cos - b * sin) * scale).astype(out_ref.dtype)
            out_ref[:, hd * RET_DK + half: (hd + 1) * RET_DK] = ((b * cos + a * sin) * scale).astype(out_ref.dtype)

    rope_store(0, q_ref, 1.0)
    rope_store(HK, k_ref, RET_DK ** -0.5)
    v_ref[...] = _dot(h, w_ref[0, :, 2 * HK: 2 * HK + HV]).astype(v_ref.dtype)
    gate_ref[...] = _dot(h, w_ref[0, :, 2 * HK + HV:]).astype(gate_ref.dtype)


def _ret_proj(x2d, gains, layer, cos, sin, w_in, ret_layer, n_seq, out_dtype):
    rows, d = x2d.shape
    seq_rows = rows // n_seq
    tile = min(seq_rows, ROW_TILE)
    tps = seq_rows // tile
    row_spec = lambda w: pl.BlockSpec((tile, w), lambda b, t: (b * tps + t, 0))
    tab_spec = pl.BlockSpec((tile, RET_DK // 2), lambda b, t: (t, 0))
    return pl.pallas_call(
        _retproj_body,
        grid=(n_seq, tps),
        in_specs=[row_spec(d), _layer_resident(gains.shape, layer), tab_spec, tab_spec,
                  _layer_resident(w_in.shape, ret_layer)],
        out_specs=[row_spec(HK), row_spec(HK), row_spec(HV), row_spec(HV)],
        out_shape=[jax.ShapeDtypeStruct((rows, HK), out_dtype), jax.ShapeDtypeStruct((rows, HK), out_dtype),
                   jax.ShapeDtypeStruct((rows, HV), out_dtype), jax.ShapeDtypeStruct((rows, HV), out_dtype)],
        compiler_params=_params(("arbitrary", "arbitrary")),
        name="ret_proj",
    )(x2d, gains, cos, sin, w_in)


def _ret_step(qh, kh, vh, s_old, inner, qdec, kdec, cdec):
    qb = qh.astype(BF16)
    kb = kh.astype(BF16)
    vb = vh.astype(BF16)
    p = (_dot_nt(qb, kb) * inner).astype(BF16)
    o = _dot(p, vb) + _dot(qb, s_old.astype(BF16)) * qdec
    kd = (kh.astype(F32) * kdec).astype(BF16)
    s_new = s_old * cdec + _dot_tn(kd, vb)
    return o, s_new


def _decay_tables(chunk):
    lg = jnp.log(1.0 - 2.0 ** (-5.0 - jnp.arange(RET_HEADS, dtype=F32)))
    idx = jnp.arange(chunk, dtype=F32)
    rel = idx[:, None] - idx[None, :]
    inner = jnp.where(rel[None] >= 0, jnp.exp(jnp.maximum(rel, 0.0)[None] * lg[:, None, None]), 0.0)
    q_dec = jnp.exp((idx + 1.0)[None, :] * lg[:, None])[..., None]
    k_dec = jnp.exp((chunk - 1.0 - idx)[None, :] * lg[:, None])[..., None]
    c_dec = jnp.exp(chunk * lg)[:, None, None]
    return (inner, jnp.broadcast_to(q_dec, (RET_HEADS, chunk, RET_DV)),
            jnp.broadcast_to(k_dec, (RET_HEADS, chunk, RET_DK)),
            jnp.broadcast_to(c_dec, (RET_HEADS, 1, RET_DV)))


def _gn_gate(o, gate, gn):
    parts = []
    for h in range(RET_HEADS):
        oh = o[:, h * RET_DV:(h + 1) * RET_DV]
        mu = jnp.mean(oh, axis=-1, keepdims=True)
        dlt = oh - mu
        var = jnp.mean(dlt * dlt, axis=-1, keepdims=True)
        parts.append(dlt * lax.rsqrt(var + GN_EPS))
    on = jnp.concatenate(parts, axis=-1) * gn
    return (_silu(gate.astype(F32)) * on).astype(BF16)


def _ret_prompt_body(q_ref, k_ref, v_ref, gate_ref, x_ref, gn_ref, w_ref, inner_ref, qdec_ref, kdec_ref, cdec_ref,
                     out_ref, s_ref, o_ref, *, chunk):
    @pl.when(pl.program_id(1) == 0)
    def _():
        s_ref[...] = jnp.zeros_like(s_ref)

    chunks = [slice(c * chunk, (c + 1) * chunk) for c in range(q_ref.shape[0] // chunk)]
    for rows in chunks:
        for h in range(RET_HEADS):
            o, s_new = _ret_step(q_ref[rows, h * RET_DK:(h + 1) * RET_DK], k_ref[rows, h * RET_DK:(h + 1) * RET_DK],
                                 v_ref[rows, h * RET_DV:(h + 1) * RET_DV], s_ref[0, h],
                                 inner_ref[h], qdec_ref[h], kdec_ref[h], cdec_ref[h])
            o_ref[rows, h * RET_DV:(h + 1) * RET_DV] = o
            s_ref[0, h] = s_new
    for rows in chunks:
        z = _gn_gate(o_ref[rows], gate_ref[rows], gn_ref[0])
        out_ref[rows] = x_ref[rows] + _dot(z, w_ref[0])


def _ret_prompt(q, k, v, gate, x2d, gn_g, w_out, ret_layer, n_seq):
    rows, d = x2d.shape
    seq_rows = rows // n_seq
    tile = min(seq_rows, ROW_TILE)
    chunk = min(tile, RET_CHUNK_PROMPT)
    tps = seq_rows // tile
    tabs = _decay_tables(chunk)
    row_spec = lambda w: pl.BlockSpec((tile, w), lambda b, t: (b * tps + t, 0))
    return pl.pallas_call(
        functools.partial(_ret_prompt_body, chunk=chunk),
        grid=(n_seq, tps),
        in_specs=[row_spec(HK), row_spec(HK), row_spec(HV), row_spec(HV), row_spec(d),
                  _layer_resident(gn_g.shape, ret_layer), _layer_resident(w_out.shape, ret_layer)]
                 + [_resident(t.shape) for t in tabs],
        out_specs=[row_spec(d), pl.BlockSpec((1, RET_HEADS, RET_DK, RET_DV), lambda b, t: (b, 0, 0, 0))],
        out_shape=[jax.ShapeDtypeStruct((rows, d), F32),
                   jax.ShapeDtypeStruct((n_seq, RET_HEADS, RET_DK, RET_DV), F32)],
        scratch_shapes=[pltpu.VMEM((tile, HV), F32)],
        compiler_params=_params(("arbitrary", "arbitrary")),
        name="ret_prompt",
    )(q, k, v, gate, x2d, gn_g, w_out, *tabs)


def _ret_sample_body(q_ref, k_ref, v_ref, s0_ref, inner_ref, qdec_ref, kdec_ref, cdec_ref, o_ref, s_ref, *, seq_rows):
    for i in range(s0_ref.shape[0]):
        rows = slice(i * seq_rows, (i + 1) * seq_rows)
        for h in range(RET_HEADS):
            o, s_new = _ret_step(q_ref[rows, h * RET_DK:(h + 1) * RET_DK], k_ref[rows, h * RET_DK:(h + 1) * RET_DK],
                                 v_ref[rows, h * RET_DV:(h + 1) * RET_DV], s0_ref[i, h],
                                 inner_ref[h], qdec_ref[h], kdec_ref[h], cdec_ref[h])
            o_ref[rows, h * RET_DV:(h + 1) * RET_DV] = o
            s_ref[i, h] = s_new


def _ret_sample(q, k, v, states, ret_layer, n_seq, seq_rows):
    bb = 4 if n_seq % 4 == 0 else 1
    steps = n_seq // bb
    tabs = _decay_tables(seq_rows)
    row_spec = lambda w: pl.BlockSpec((bb * seq_rows, w), lambda i: (i, 0))
    st_block = (bb, RET_HEADS, RET_DK, RET_DV)
    return pl.pallas_call(
        functools.partial(_ret_sample_body, seq_rows=seq_rows),
        grid=(steps,),
        in_specs=[row_spec(HK), row_spec(HK), row_spec(HV),
                  pl.BlockSpec(st_block, lambda i: (ret_layer * steps + i, 0, 0, 0))] + [_resident(t.shape) for t in tabs],
        out_specs=[row_spec(HV), pl.BlockSpec(st_block, lambda i: (i, 0, 0, 0))],
        out_shape=[jax.ShapeDtypeStruct((n_seq * seq_rows, HV), F32),
                   jax.ShapeDtypeStruct((n_seq, RET_HEADS, RET_DK, RET_DV), F32)],
        compiler_params=_params(("arbitrary",)),
        name="ret_sample",
    )(q, k, v, states, *tabs)


def _retout_body(o_ref, gate_ref, x_ref, gn_ref, w_ref, out_ref):
    out_ref[...] = x_ref[...] + _dot(_gn_gate(o_ref[...], gate_ref[...], gn_ref[0]), w_ref[0])


def _ret_out(o, gate, x2d, gn_g, w_out, ret_layer):
    rows, d = x2d.shape
    tile = min(rows, ROW_TILE)
    row_spec = lambda w: pl.BlockSpec((tile, w), lambda r: (r, 0))
    return pl.pallas_call(
        _retout_body,
        grid=(rows // tile,),
        in_specs=[row_spec(HV), row_spec(HV), row_spec(d), _layer_resident(gn_g.shape, ret_layer),
                  _layer_resident(w_out.shape, ret_layer)],
        out_specs=row_spec(d),
        out_shape=jax.ShapeDtypeStruct((rows, d), F32),
        compiler_params=_params(("arbitrary",)),
        name="ret_out",
    )(o, gate, x2d, gn_g, w_out)


def _softmax(s):
    e = jnp.exp(s - jnp.max(s, axis=-1, keepdims=True))
    return e / jnp.sum(e, axis=-1, keepdims=True)


def _xattn_body(x_ref, g_ref, wq_ref, wo_ref, k_ref, v_ref, out_ref, att_ref, *s_refs, seq_rows):
    n_mem = k_ref.shape[1] // KV_ROW_STRIDE
    head_cols = [slice(hd * X_HEAD_DIM, (hd + 1) * X_HEAD_DIM) for hd in range(X_HEADS)]

    def project_q(rows):
        h = _rms(x_ref[rows], g_ref[0]).astype(BF16)
        return _dot(h, wq_ref[0]) * (X_HEAD_DIM ** -0.5)

    def project_out(rows):
        out_ref[rows] = x_ref[rows] + _dot(att_ref[rows].astype(BF16), wo_ref[0])

    if s_refs:
        s_ref, = s_refs
        q = project_q(slice(None))
        pairs = [(slice(i * seq_rows, (i + 1) * seq_rows), cols, i, hd)
                 for i in range(k_ref.shape[0]) for hd, cols in enumerate(head_cols)]
        for n, (rows, cols, i, hd) in enumerate(pairs):
            s_ref[n * seq_rows:(n + 1) * seq_rows] = _dot_nt(q[rows, cols].astype(BF16), _head_kv(k_ref, i, hd, n_mem))
        s_ref[...] = _softmax(s_ref[...])
        for n, (rows, cols, i, hd) in enumerate(pairs):
            p = s_ref[n * seq_rows:(n + 1) * seq_rows]
            att_ref[rows, cols] = _dot(p.astype(BF16), _head_kv(v_ref, i, hd, n_mem))
        project_out(slice(None))
    else:
        q = project_q(slice(None))
        for hd, cols in enumerate(head_cols):
            p = _softmax(_dot_nt(q[:, cols].astype(BF16), _head_kv(k_ref, 0, hd, n_mem)))
            att_ref[:, cols] = _dot(p.astype(BF16), _head_kv(v_ref, 0, hd, n_mem))
        project_out(slice(None))


def _xattn(x2d, gains, wq, wo, layer, mem_k, mem_v, n_seq, seq_rows):
    rows, d = x2d.shape
    n_mem = mem_k.shape[1]
    if seq_rows >= ROW_TILE:
        tile, body_rows = ROW_TILE, ROW_TILE
        tps = seq_rows // tile
        grid = (n_seq, tps)
        row_spec = pl.BlockSpec((tile, d), lambda b, t: (b * tps + t, 0))
        kv_spec = pl.BlockSpec((1, n_mem, LANES), lambda b, t: (layer * n_seq + b, 0, 0))
        sem = ("arbitrary", "arbitrary")
        score_scratch = []
    else:
        bb = 4 if n_seq % 4 == 0 else 1
        tile, body_rows = bb * seq_rows, seq_rows
        steps = n_seq // bb
        grid = (steps,)
        row_spec = pl.BlockSpec((tile, d), lambda i: (i, 0))
        kv_spec = pl.BlockSpec((bb, n_mem, LANES), lambda i: (layer * steps + i, 0, 0))
        sem = ("arbitrary",)
        score_scratch = [pltpu.VMEM((bb * X_HEADS * seq_rows, n_mem // KV_ROW_STRIDE), F32)]
    return pl.pallas_call(
        functools.partial(_xattn_body, seq_rows=body_rows),
        grid=grid,
        in_specs=[row_spec, _layer_resident(gains.shape, layer), _layer_resident(wq.shape, layer),
                  _layer_resident(wo.shape, layer), kv_spec, kv_spec],
        out_specs=row_spec,
        out_shape=jax.ShapeDtypeStruct((rows, d), F32),
        scratch_shapes=[pltpu.VMEM((tile, d), F32)] + score_scratch,
        compiler_params=_params(sem),
        name="xattn",
    )(x2d, gains, wq, wo, mem_k, mem_v)


def _conv_chunk(u, prev, cwb, shift):
    c = cwb[3:4] + cwb[0:1] * shift(u, prev, 2)
    c = c + cwb[1:2] * shift(u, prev, 1)
    return c + cwb[2:3] * u


def _ffn_gate(ua, ug, prev_a, prev_g, cwb_a, cwb_g, shift):
    return (_conv_chunk(ua, prev_a, cwb_a, shift) * _silu(_conv_chunk(ug, prev_g, cwb_g, shift))).astype(BF16)


def _ffn_chunk(h, wa, wg, wd, prev_a, prev_g, cwb_a, cwb_g, shift):
    ua = _dot(h, wa)
    ug = _dot(h, wg)
    return _dot(_ffn_gate(ua, ug, prev_a, prev_g, cwb_a, cwb_g, shift), wd), ua, ug


def _ffn_prompt_body(x_ref, g_ref, gf_ref, wu_ref, wd_ref, cwb_ref, out_ref, c_ref,
                     h_ref, acc_ref, ua0_ref, ug0_ref, ua1_ref, ug1_ref, z0_ref, z1_ref, *, final_norm):
    halo = (CONV_W - 1) * SUBLANES
    tile = acc_ref.shape[0]
    groups = tile // SUBLANES

    @pl.when(pl.program_id(1) == 0)
    def _():
        c_ref[...] = jnp.zeros_like(c_ref)

    x = jnp.swapaxes(x_ref[...].reshape(SUBLANES, groups, D_MODEL), 0, 1).reshape(tile, D_MODEL)
    h_ref[...] = _rms(x, g_ref[0]).astype(BF16)
    acc_ref[...] = x
    slots = ((ua0_ref, ug0_ref), (ua1_ref, ug1_ref))
    starts = range(0, D_FF, FFN_PROMPT_CHUNK)
    n_chunks = len(starts)

    def cols(f):
        lo, hi = starts[f], min(starts[f] + FFN_PROMPT_CHUNK, D_FF)
        return slice(lo, hi), slice(D_FF + lo, D_FF + hi), hi - lo

    def up(f):
        ca, cg, width = cols(f)
        first_sublane = lax.broadcasted_iota(jnp.int32, (SUBLANES, width), 0) == 0
        for u_ref, cc in zip(slots[f % 2], (ca, cg)):
            u = _dot(h_ref[...], wu_ref[0, :, cc])
            u_ref[halo:, :width] = u
            for m in range(CONV_W - 1):
                rows = slice(m * SUBLANES, (m + 1) * SUBLANES)
                src = u[tile - halo + m * SUBLANES: tile - halo + (m + 1) * SUBLANES]
                u_ref[rows, :width] = jnp.where(first_sublane, pltpu.roll(c_ref[0, rows, cc], 1, axis=0),
                                                pltpu.roll(src, 1, axis=0))
                c_ref[0, rows, cc] = src

    def conv(u_ref, cw, r0, n, width):
        c = cw[3:4] + cw[0:1] * u_ref[r0: r0 + n, :width]
        c = c + cw[1:2] * u_ref[r0 + SUBLANES: r0 + SUBLANES + n, :width]
        return c + cw[2:3] * u_ref[r0 + halo: r0 + halo + n, :width]

    def gate(f):
        ca, cg, width = cols(f)
        ua_ref, ug_ref = slots[f % 2]
        cwa, cwg = cwb_ref[0, :, ca], cwb_ref[0, :, cg]
        z_ref = (z0_ref, z1_ref)[f % 2]
        for r0 in range(0, tile, FFN_GATE_ROWS):
            z_ref[r0: r0 + FFN_GATE_ROWS, :width] = (
                conv(ua_ref, cwa, r0, FFN_GATE_ROWS, width)
                * _silu(conv(ug_ref, cwg, r0, FFN_GATE_ROWS, width))).astype(BF16)
        return z_ref[:, :width]

    up(0)
    for f in range(n_chunks):
        if f + 1 < n_chunks:
            up(f + 1)
        acc_ref[...] += _dot(gate(f), wd_ref[0, cols(f)[0], :])
    y = acc_ref[...]
    if final_norm:
        y = _rms(y, gf_ref[...])
    out_ref[...] = jnp.swapaxes(y.reshape(groups, SUBLANES, D_MODEL), 0, 1).reshape(tile, D_MODEL)


def _ffn_prompt(x2d, gains, g_final, w_up, w_down, cwb, layer, n_seq, final_norm):
    rows, d = x2d.shape
    seq_rows = rows // n_seq
    tile = min(seq_rows, ROW_TILE)
    tps = seq_rows // tile
    halo = (CONV_W - 1) * SUBLANES
    row_spec = pl.BlockSpec((tile, d), lambda b, t: (b * tps + t, 0))
    u_scratch = pltpu.VMEM((halo + tile, FFN_PROMPT_CHUNK), F32)
    z_scratch = pltpu.VMEM((tile, FFN_PROMPT_CHUNK), BF16)
    y, carry = pl.pallas_call(
        functools.partial(_ffn_prompt_body, final_norm=final_norm),
        grid=(n_seq, tps),
        in_specs=[row_spec, _layer_resident(gains.shape, layer), _resident((1, d)),
                  _layer_resident(w_up.shape, layer), _layer_resident(w_down.shape, layer),
                  _layer_resident(cwb.shape, layer)],
        out_specs=[row_spec, pl.BlockSpec((1, halo, 2 * D_FF), lambda b, t: (b, 0, 0))],
        out_shape=[jax.ShapeDtypeStruct((rows, d), F32), jax.ShapeDtypeStruct((n_seq, halo, 2 * D_FF), F32)],
        scratch_shapes=[pltpu.VMEM((tile, d), BF16), pltpu.VMEM((tile, d), F32),
                        u_scratch, u_scratch, u_scratch, u_scratch, z_scratch, z_scratch],
        compiler_params=_params(("arbitrary", "arbitrary")),
        name="ffn_prompt",
    )(x2d, gains, g_final.reshape(1, d), w_up, w_down, cwb)
    return y, carry[:, SUBLANES - 1::SUBLANES]


def _ffn_sample_body(x_ref, g_ref, gf_ref, wa_ref, wg_ref, wd_ref, cwa_ref, cwg_ref, ba_ref, bg_ref,
                     out_ref, ca_ref, cg_ref, h_ref, acc_ref, *, n_seq, seq_rows, final_norm):
    f = pl.program_id(0)

    @pl.when(f == 0)
    def _():
        xt = _to_time_major(x_ref[...], n_seq, seq_rows)
        h_ref[...] = _rms(xt, g_ref[0]).astype(BF16)
        acc_ref[...] = xt

    rows = n_seq * seq_rows
    hist = (CONV_W - 1) * n_seq

    def shift(u, prev, j):
        return jnp.concatenate([prev, u], axis=0)[hist - j * n_seq: hist - j * n_seq + rows]

    y, ua, ug = _ffn_chunk(h_ref[...], wa_ref[0], wg_ref[0], wd_ref[0], ba_ref[...], bg_ref[...],
                           cwa_ref[0], cwg_ref[0], shift)
    acc_ref[...] += y
    ca_ref[...] = ua[rows - hist:]
    cg_ref[...] = ug[rows - hist:]

    @pl.when(f == pl.num_programs(0) - 1)
    def _():
        yt = acc_ref[...]
        if final_norm:
            yt = _rms(yt, gf_ref[...])
        out_ref[...] = _from_time_major(yt, n_seq, seq_rows)


def _ffn_sample(x2d, gains, g_final, w_up, w_down, cwb, layer, buf, n_seq, final_norm):
    rows, d = x2d.shape
    seq_rows = rows // n_seq
    hist = (CONV_W - 1) * n_seq
    a_col = lambda f: f
    g_col = lambda f: NF + f
    c_shape = jax.ShapeDtypeStruct((hist, D_FF), F32)
    c_spec = pl.BlockSpec((hist, FF_CHUNK), lambda f: (0, f))
    return pl.pallas_call(
        functools.partial(_ffn_sample_body, n_seq=n_seq, seq_rows=seq_rows, final_norm=final_norm),
        grid=(NF,),
        in_specs=[_resident((rows, d)), _layer_resident(gains.shape, layer), _resident((1, d)),
                  pl.BlockSpec((1, d, FF_CHUNK), lambda f: (layer, 0, a_col(f))),
                  pl.BlockSpec((1, d, FF_CHUNK), lambda f: (layer, 0, g_col(f))),
                  pl.BlockSpec((1, FF_CHUNK, d), lambda f: (layer, f, 0)),
                  pl.BlockSpec((1, 4, FF_CHUNK), lambda f: (layer, 0, a_col(f))),
                  pl.BlockSpec((1, 4, FF_CHUNK), lambda f: (layer, 0, g_col(f))),
                  pl.BlockSpec((hist, FF_CHUNK), lambda f: (0, a_col(f))),
                  pl.BlockSpec((hist, FF_CHUNK), lambda f: (0, g_col(f)))],
        out_specs=[pl.BlockSpec((rows, d), lambda f: (0, 0)), c_spec, c_spec],
        out_shape=[jax.ShapeDtypeStruct((rows, d), F32), c_shape, c_shape],
        scratch_shapes=[pltpu.VMEM((rows, d), BF16), pltpu.VMEM((rows, d), F32)],
        compiler_params=_params(("arbitrary",)),
        name="ffn_sample",
    )(x2d, gains, g_final.reshape(1, d), w_up, w_up, w_down, cwb, cwb, buf, buf)


def _pool_mix(pooled, w_ref, scale_ref, x):
    mixed = [_dot(pooled[g].astype(BF16), w_ref[0, g]) for g in range(len(POOL_WINDOWS))]
    return x + jnp.concatenate(mixed, axis=-1) * scale_ref[0]


def _pool_prompt_body(x_ref, g_ref, w_ref, scale_ref, out_ref, hist_ref):
    t = pl.program_id(1)

    @pl.when(t == 0)
    def _():
        hist_ref[...] = jnp.zeros_like(hist_ref)

    x = x_ref[...]
    tile = x.shape[0]
    h = _rms(x, g_ref[0])
    ext = jnp.concatenate([hist_ref[0], h], axis=0)
    hist_ref[0] = h[tile - POOL_CARRY:]
    pos1 = (t * tile + 1 + lax.broadcasted_iota(jnp.int32, (tile, POOL_GC), 0)).astype(F32)
    pooled = []
    s = ext
    for g, w in enumerate(POOL_WINDOWS):
        s = s[:, POOL_GC * (1 if g else 0):]
        s = s + pltpu.roll(s, w // 2, axis=0)
        cnt = jnp.minimum(pos1, float(w))
        pooled.append(s[POOL_CARRY:, :POOL_GC] / cnt - h[:, g * POOL_GC:(g + 1) * POOL_GC])
    out_ref[...] = _pool_mix(pooled, w_ref, scale_ref, x)


def _pool_prompt(x2d, gains, layer, pool_w, pool_scale, pool_layer, n_seq):
    rows, d = x2d.shape
    seq_rows = rows // n_seq
    tile = min(seq_rows, ROW_TILE)
    tps = seq_rows // tile
    row_spec = pl.BlockSpec((tile, d), lambda b, t: (b * tps + t, 0))
    return pl.pallas_call(
        _pool_prompt_body,
        grid=(n_seq, tps),
        in_specs=[row_spec, _layer_resident(gains.shape, layer), _layer_resident(pool_w.shape, pool_layer),
                  _layer_resident(pool_scale.shape, pool_layer)],
        out_specs=[row_spec, pl.BlockSpec((1, POOL_CARRY, d), lambda b, t: (b, 0, 0))],
        out_shape=[jax.ShapeDtypeStruct((rows, d), F32), jax.ShapeDtypeStruct((n_seq, POOL_CARRY, d), F32)],
        compiler_params=_params(("arbitrary", "arbitrary")),
        name="pool_prompt",
    )(x2d, gains, pool_w, pool_scale)


def _pool_sample_body(x_ref, g_ref, w_ref, scale_ref, buf_ref, out_ref, h_ref, *, n_seq, seq_rows, pos0):
    xt = _to_time_major(x_ref[...], n_seq, seq_rows)
    h = _rms(xt, g_ref[0])
    s = jnp.concatenate([buf_ref[...], h], axis=0)
    rows = n_seq * seq_rows
    first = POOL_BUF
    pooled = []
    for g, w in enumerate(POOL_WINDOWS):
        s = s[:, POOL_GC * (1 if g else 0):]
        step = (w // 2) * n_seq
        s = s[step:] + s[:-step]
        first -= w // 2
        win = s[first * n_seq: first * n_seq + rows, :POOL_GC]
        inv = [1.0 / min(pos0 + t + 1, w) for t in range(seq_rows)]
        if len(set(inv)) == 1:
            win = win * inv[0]
        else:
            win = jnp.concatenate([win[t * n_seq:(t + 1) * n_seq] * inv[t] for t in range(seq_rows)], axis=0)
        pooled.append(win - h[:, g * POOL_GC:(g + 1) * POOL_GC])
    out_ref[...] = _from_time_major(_pool_mix(pooled, w_ref, scale_ref, xt), n_seq, seq_rows)
    h_ref[...] = _from_time_major(h, n_seq, seq_rows)


def _pool_sample(x2d, gains, layer, pool_w, pool_scale, pool_layer, buf_t, n_seq, pos0):
    rows, d = x2d.shape
    seq_rows = rows // n_seq
    full = lambda s: pl.BlockSpec(s, lambda i: (0,) * len(s))
    return pl.pallas_call(
        functools.partial(_pool_sample_body, n_seq=n_seq, seq_rows=seq_rows, pos0=pos0),
        grid=(1,),
        in_specs=[full((rows, d)), _layer_resident(gains.shape, layer), _layer_resident(pool_w.shape, pool_layer),
                  _layer_resident(pool_scale.shape, pool_layer), full(buf_t.shape)],
        out_specs=[full((rows, d)), full((rows, d))],
        out_shape=[jax.ShapeDtypeStruct((rows, d), F32)] * 2,
        compiler_params=_params(("arbitrary",)),
        name="pool_sample",
    )(x2d, gains, pool_w, pool_scale, buf_t)


def _rope_tables(pos):
    inv = 1.0 / (ROPE_BASE ** (jnp.arange(0, RET_DK, 2, dtype=F32) / RET_DK))
    ang = pos[:, None] * inv[None, :]
    return jnp.cos(ang), jnp.sin(ang)


def kernel(x_prompt, x_sample, mem_prompt, cache_mem_k, cache_mem_v, state_ret, cache_pool, cache_ffn_conv, w_ret_in, ret_gn, w_ret_out, pool_w, pool_scale, norm_mem, w_xq, w_xk, w_xv, w_xo, w_up, conv_w, conv_b, w_down, norm_mix, norm_xattn, norm_ffn, norm_final):
    bp, lp, d = x_prompt.shape
    bs, ls, _ = x_sample.shape
    depth = w_up.shape[0]
    n_mem = mem_prompt.shape[1]

    w_in_b, w_out_b, pool_w_b, wq_b, wk_b, wv_b, wo_b, w_up_b, w_down_b = (
        w.astype(BF16) for w in (w_ret_in, w_ret_out, pool_w, w_xq, w_xk, w_xv, w_xo, w_up, w_down))
    cwb = jnp.concatenate([conv_w, conv_b[:, None, :]], axis=1)
    g_mix, g_x, g_ffn = _gain(norm_mix), _gain(norm_xattn), _gain(norm_ffn)
    gn_g, p_scale = _gain(ret_gn), _gain(pool_scale)

    mem_k2, mem_v2 = _mem_kv(mem_prompt.reshape(bp * n_mem, d), norm_mem, wk_b, wv_b)
    kv_p = tuple(a.reshape(depth * bp, n_mem * KV_ROW_STRIDE, LANES) for a in (mem_k2, mem_v2))
    kv_s = (_kv_rows(cache_mem_k), _kv_rows(cache_mem_v))
    states = state_ret.reshape((state_ret.shape[0] * bs,) + state_ret.shape[2:])

    cos_p, sin_p = _rope_tables(jnp.arange(lp, dtype=F32))
    cos_s, sin_s = _rope_tables(PAST_LEN + jnp.arange(ls, dtype=F32))
    cos_s, sin_s = jnp.tile(cos_s, (bs, 1)), jnp.tile(sin_s, (bs, 1))

    xp = x_prompt.reshape(bp * lp, d)
    xs = x_sample.reshape(bs * ls, d)
    ret_p, ret_s, pool_p, pool_s, conv_p, conv_s = [], [], [], [], [], []
    for i in range(depth):
        j = i // 2
        last = i == depth - 1
        if i % 2 == 0:
            q, k, v, gate = _ret_proj(xp, g_mix, i, cos_p, sin_p, w_in_b, j, bp, BF16)
            xp, s = _ret_prompt(q, k, v, gate, xp, gn_g, w_out_b, j, bp)
            ret_p.append(s)
            q, k, v, gate = _ret_proj(xs, g_mix, i, cos_s, sin_s, w_in_b, j, 1, F32)
            o, s = _ret_sample(q, k, v, states, j, bs, ls)
            xs = _ret_out(o, gate, xs, gn_g, w_out_b, j)
            ret_s.append(s)
        else:
            xp, hist = _pool_prompt(xp, g_mix, i, pool_w_b, p_scale, j, bp)
            pool_p.append(hist[:, POOL_CARRY - POOL_BUF:])
            buf_t = jnp.swapaxes(cache_pool[j], 0, 1).reshape(POOL_BUF * bs, d)
            xs, hs = _pool_sample(xs, g_mix, i, pool_w_b, p_scale, j, buf_t, bs, PAST_LEN)
            pool_s.append(jnp.concatenate([cache_pool[j], hs.reshape(bs, ls, d)], axis=1)[:, -POOL_BUF:])

        xp = _xattn(xp, g_x, wq_b, wo_b, i, *kv_p, bp, lp)
        xs = _xattn(xs, g_x, wq_b, wo_b, i, *kv_s, bs, ls)

        xp, tail = _ffn_prompt(xp, g_ffn, norm_final, w_up_b, w_down_b, cwb, i, bp, last)
        conv_p.append(tail)
        buf = jnp.swapaxes(cache_ffn_conv[i], 0, 1).reshape((CONV_W - 1) * bs, 2 * D_FF)
        xs, ca, cg = _ffn_sample(xs, g_ffn, norm_final, w_up_b, w_down_b, cwb, i, buf, bs, last)
        tail = jnp.concatenate([ca, cg], axis=-1)
        conv_s.append(jnp.swapaxes(tail.reshape(CONV_W - 1, bs, 2 * D_FF), 0, 1))

    mk = _kv_unrows(mem_k2, (depth, bp), n_mem)
    mv = _kv_unrows(mem_v2, (depth, bp), n_mem)
    return (xp.reshape(bp, lp, d), xs.reshape(bs, ls, d), jnp.stack(ret_p), jnp.stack(ret_s).astype(state_ret.dtype),
            jnp.stack(pool_p), jnp.stack(pool_s), jnp.stack(conv_p), jnp.stack(conv_s), mk, mv)
```

```python
import functools

import jax
import jax.numpy as jnp
from jax import lax
from jax.experimental import pallas as pl
from jax.experimental.pallas import tpu as pltpu

F32 = jnp.float32
BF16 = jnp.bfloat16

D_MODEL = 1024
PAST_LEN = 16384
RET_HEADS = 4
RET_DK = D_MODEL // RET_HEADS
RET_DV = 2 * D_MODEL // RET_HEADS
HK = RET_HEADS * RET_DK
HV = RET_HEADS * RET_DV
ROPE_BASE = 10000.0
POOL_WINDOWS = (2, 4, 8, 16)
POOL_GC = D_MODEL // len(POOL_WINDOWS)
POOL_BUF = max(POOL_WINDOWS) - 1
X_HEADS = 4
X_HEAD_DIM = D_MODEL // X_HEADS
D_FF = 2816
CONV_W = 3
NORM_EPS = 1e-6
GN_EPS = 1e-5

SUBLANES = 8
LANES = 128
KV_LANE_TILES = X_HEAD_DIM // LANES
KV_ROW_STRIDE = X_HEADS * KV_LANE_TILES
assert KV_ROW_STRIDE == SUBLANES
ROW_TILE = 512
RET_CHUNK_PROMPT = 256
FF_CHUNK = 256
NF = D_FF // FF_CHUNK
FFN_PROMPT_CHUNK = 512
FFN_GATE_ROWS = 64
POOL_CARRY = 16
VMEM_LIMIT = 56 * 1024 * 1024


def _params(sem):
    return pltpu.CompilerParams(dimension_semantics=sem, vmem_limit_bytes=VMEM_LIMIT)


def _resident(shape):
    zeros = (0,) * len(shape)
    return pl.BlockSpec(shape, lambda *_: zeros, pipeline_mode=pl.Buffered(1))


def _layer_resident(stacked_shape, layer):
    idx = (layer,) + (0,) * (len(stacked_shape) - 1)
    return pl.BlockSpec((1,) + tuple(stacked_shape[1:]), lambda *_: idx, pipeline_mode=pl.Buffered(1))


def _gain(g):
    return g.reshape(g.shape[0], 1, g.shape[1])


def _dot(a, b):
    return jnp.dot(a, b, preferred_element_type=F32)


def _dot_nt(a, b):
    return lax.dot_general(a, b, (((1,), (1,)), ((), ())), preferred_element_type=F32)


def _dot_tn(a, b):
    return lax.dot_general(a, b, (((0,), (0,)), ((), ())), preferred_element_type=F32)


def _rms(x, g):
    return x * lax.rsqrt(jnp.mean(x * x, axis=-1, keepdims=True) + NORM_EPS) * g


def _silu(x):
    return x * (1.0 / (1.0 + jnp.exp(-x)))


def _to_time_major(x, n_seq, seq_rows):
    return jnp.swapaxes(x.reshape(n_seq, seq_rows, x.shape[-1]), 0, 1).reshape(x.shape)


def _from_time_major(x, n_seq, seq_rows):
    return jnp.swapaxes(x.reshape(seq_rows, n_seq, x.shape[-1]), 0, 1).reshape(x.shape)


def _memkv_body(mem_ref, g_ref, wk_ref, wv_ref, k_ref, v_ref):
    h = _rms(mem_ref[...], g_ref[0]).astype(BF16)
    for w_ref, o_ref in ((wk_ref, k_ref), (wv_ref, v_ref)):
        y = _dot(h, w_ref[0])
        for hd in range(X_HEADS):
            for c in range(KV_LANE_TILES):
                col = hd * X_HEAD_DIM + c * LANES
                o_ref[0, pl.ds(c * X_HEADS + hd, y.shape[0], stride=KV_ROW_STRIDE), :] = y[:, col:col + LANES]


def _mem_kv(mem2d, norm_mem, wk, wv):
    rows, d = mem2d.shape
    depth = wk.shape[0]
    tile = min(rows, 1024)
    w_spec = pl.BlockSpec((1, d, d), lambda i, r: (i, 0, 0))
    o_spec = pl.BlockSpec((1, tile * KV_ROW_STRIDE, LANES), lambda i, r: (i, r, 0))
    return pl.pallas_call(
        _memkv_body,
        grid=(depth, rows // tile),
        in_specs=[pl.BlockSpec((tile, d), lambda i, r: (r, 0)),
                  pl.BlockSpec((1, 1, d), lambda i, r: (i, 0, 0)), w_spec, w_spec],
        out_specs=[o_spec, o_spec],
        out_shape=[jax.ShapeDtypeStruct((depth, rows * KV_ROW_STRIDE, LANES), F32)] * 2,
        compiler_params=_params(("arbitrary", "arbitrary")),
        name="mem_kv",
    )(mem2d, _gain(norm_mem), wk, wv)


def _kv_rows(a):
    lead, n_mem = a.shape[:-3], a.shape[-3]
    a = a.reshape(lead + (n_mem, X_HEADS, KV_LANE_TILES, LANES))
    a = jnp.swapaxes(a, -3, -2)
    return a.reshape((-1, n_mem * KV_ROW_STRIDE, LANES))


def _kv_unrows(a, lead, n_mem):
    a = a.reshape(lead + (n_mem, KV_LANE_TILES, X_HEADS, LANES))
    a = jnp.swapaxes(a, -3, -2)
    return a.reshape(lead + (n_mem, X_HEADS, X_HEAD_DIM))


def _head_kv(ref, i, hd, n_mem):
    parts = [ref[i, pl.ds(c * X_HEADS + hd, n_mem, stride=KV_ROW_STRIDE), :] for c in range(KV_LANE_TILES)]
    return jnp.concatenate(parts, axis=1).astype(BF16)


def _retproj_body(x_ref, g_ref, cos_ref, sin_ref, w_ref, q_ref, k_ref, v_ref, gate_ref):
    h = _rms(x_ref[...], g_ref[0]).astype(BF16)
    cos = cos_ref[...]
    sin = sin_ref[...]
    half = RET_DK // 2

    def rope_store(col0, out_ref, scale):
        p = _dot(h, w_ref[0, :, col0:col0 + HK])
        for hd in range(RET_HEADS):
            a = p[:, hd * RET_DK: hd * RET_DK + half]
            b = p[:, hd * RET_DK + half: (hd + 1) * RET_DK]
            out_ref[:, hd * RET_DK: hd * RET_DK + half] = ((a * cos - b * sin) * scale).astype(out_ref.dtype)
            out_ref[:, hd * RET_DK + half: (hd + 1) * RET_DK] = ((b * cos + a * sin) * scale).astype(out_ref.dtype)

    rope_store(0, q_ref, 1.0)
    rope_store(HK, k_ref, RET_DK ** -0.5)
    v_ref[...] = _dot(h, w_ref[0, :, 2 * HK: 2 * HK + HV]).astype(v_ref.dtype)
    gate_ref[...] = _dot(h, w_ref[0, :, 2 * HK + HV:]).astype(gate_ref.dtype)


def _ret_proj(x2d, gains, layer, cos, sin, w_in, ret_layer, n_seq, out_dtype):
    rows, d = x2d.shape
    seq_rows = rows // n_seq
    tile = min(seq_rows, ROW_TILE)
    tps = seq_rows // tile
    row_spec = lambda w: pl.BlockSpec((tile, w), lambda b, t: (b * tps + t, 0))
    tab_spec = pl.BlockSpec((tile, RET_DK // 2), lambda b, t: (t, 0))
    return pl.pallas_call(
        _retproj_body,
        grid=(n_seq, tps),
        in_specs=[row_spec(d), _layer_resident(gains.shape, layer), tab_spec, tab_spec,
                  _layer_resident(w_in.shape, ret_layer)],
        out_specs=[row_spec(HK), row_spec(HK), row_spec(HV), row_spec(HV)],
        out_shape=[jax.ShapeDtypeStruct((rows, HK), out_dtype), jax.ShapeDtypeStruct((rows, HK), out_dtype),
                   jax.ShapeDtypeStruct((rows, HV), out_dtype), jax.ShapeDtypeStruct((rows, HV), out_dtype)],
        compiler_params=_params(("arbitrary", "arbitrary")),
        name="ret_proj",
    )(x2d, gains, cos, sin, w_in)


def _ret_step(qh, kh, vh, s_old, inner, qdec, kdec, cdec):
    qb = qh.astype(BF16)
    kb = kh.astype(BF16)
    vb = vh.astype(BF16)
    p = (_dot_nt(qb, kb) * inner).astype(BF16)
    o = _dot(p, vb) + _dot(qb, s_old.astype(BF16)) * qdec
    kd = (kh.astype(F32) * kdec).astype(BF16)
    s_new = s_old * cdec + _dot_tn(kd, vb)
    return o, s_new


def _decay_tables(chunk):
    lg = jnp.log(1.0 - 2.0 ** (-5.0 - jnp.arange(RET_HEADS, dtype=F32)))
    idx = jnp.arange(chunk, dtype=F32)
    rel = idx[:, None] - idx[None, :]
    inner = jnp.where(rel[None] >= 0, jnp.exp(jnp.maximum(rel, 0.0)[None] * lg[:, None, None]), 0.0)
    q_dec = jnp.exp((idx + 1.0)[None, :] * lg[:, None])[..., None]
    k_dec = jnp.exp((chunk - 1.0 - idx)[None, :] * lg[:, None])[..., None]
    c_dec = jnp.exp(chunk * lg)[:, None, None]
    return (inner, jnp.broadcast_to(q_dec, (RET_HEADS, chunk, RET_DV)),
            jnp.broadcast_to(k_dec, (RET_HEADS, chunk, RET_DK)),
            jnp.broadcast_to(c_dec, (RET_HEADS, 1, RET_DV)))


def _gn_gate(o, gate, gn):
    parts = []
    for h in range(RET_HEADS):
        oh = o[:, h * RET_DV:(h + 1) * RET_DV]
        mu = jnp.mean(oh, axis=-1, keepdims=True)
        dlt = oh - mu
        var = jnp.mean(dlt * dlt, axis=-1, keepdims=True)
        parts.append(dlt * lax.rsqrt(var + GN_EPS))
    on = jnp.concatenate(parts, axis=-1) * gn
    return (_silu(gate.astype(F32)) * on).astype(BF16)


def _ret_prompt_body(q_ref, k_ref, v_ref, gate_ref, x_ref, gn_ref, w_ref, inner_ref, qdec_ref, kdec_ref, cdec_ref,
                     out_ref, s_ref, o_ref, *, chunk):
    @pl.when(pl.program_id(1) == 0)
    def _():
        s_ref[...] = jnp.zeros_like(s_ref)

    chunks = [slice(c * chunk, (c + 1) * chunk) for c in range(q_ref.shape[0] // chunk)]
    for rows in chunks:
        for h in range(RET_HEADS):
            o, s_new = _ret_step(q_ref[rows, h * RET_DK:(h + 1) * RET_DK], k_ref[rows, h * RET_DK:(h + 1) * RET_DK],
                                 v_ref[rows, h * RET_DV:(h + 1) * RET_DV], s_ref[0, h],
                                 inner_ref[h], qdec_ref[h], kdec_ref[h], cdec_ref[h])
            o_ref[rows, h * RET_DV:(h + 1) * RET_DV] = o
            s_ref[0, h] = s_new
    for rows in chunks:
        z = _gn_gate(o_ref[rows], gate_ref[rows], gn_ref[0])
        out_ref[rows] = x_ref[rows] + _dot(z, w_ref[0])


def _ret_prompt(q, k, v, gate, x2d, gn_g, w_out, ret_layer, n_seq):
    rows, d = x2d.shape
    seq_rows = rows // n_seq
    tile = min(seq_rows, ROW_TILE)
    chunk = min(tile, RET_CHUNK_PROMPT)
    tps = seq_rows // tile
    tabs = _decay_tables(chunk)
    row_spec = lambda w: pl.BlockSpec((tile, w), lambda b, t: (b * tps + t, 0))
    return pl.pallas_call(
        functools.partial(_ret_prompt_body, chunk=chunk),
        grid=(n_seq, tps),
        in_specs=[row_spec(HK), row_spec(HK), row_spec(HV), row_spec(HV), row_spec(d),
                  _layer_resident(gn_g.shape, ret_layer), _layer_resident(w_out.shape, ret_layer)]
                 + [_resident(t.shape) for t in tabs],
        out_specs=[row_spec(d), pl.BlockSpec((1, RET_HEADS, RET_DK, RET_DV), lambda b, t: (b, 0, 0, 0))],
        out_shape=[jax.ShapeDtypeStruct((rows, d), F32),
                   jax.ShapeDtypeStruct((n_seq, RET_HEADS, RET_DK, RET_DV), F32)],
        scratch_shapes=[pltpu.VMEM((tile, HV), F32)],
        compiler_params=_params(("arbitrary", "arbitrary")),
        name="ret_prompt",
    )(q, k, v, gate, x2d, gn_g, w_out, *tabs)


def _ret_sample_body(q_ref, k_ref, v_ref, s0_ref, inner_ref, qdec_ref, kdec_ref, cdec_ref, o_ref, s_ref, *, seq_rows):
    for i in range(s0_ref.shape[0]):
        rows = slice(i * seq_rows, (i + 1) * seq_rows)
        for h in range(RET_HEADS):
            o, s_new = _ret_step(q_ref[rows, h * RET_DK:(h + 1) * RET_DK], k_ref[rows, h * RET_DK:(h + 1) * RET_DK],
                                 v_ref[rows, h * RET_DV:(h + 1) * RET_DV], s0_ref[i, h],
                                 inner_ref[h], qdec_ref[h], kdec_ref[h], cdec_ref[h])
            o_ref[rows, h * RET_DV:(h + 1) * RET_DV] = o
            s_ref[i, h] = s_new


def _ret_sample(q, k, v, states, ret_layer, n_seq, seq_rows):
    bb = 4 if n_seq % 4 == 0 else 1
    steps = n_seq // bb
    tabs = _decay_tables(seq_rows)
    row_spec = lambda w: pl.BlockSpec((bb * seq_rows, w), lambda i: (i, 0))
    st_block = (bb, RET_HEADS, RET_DK, RET_DV)
    return pl.pallas_call(
        functools.partial(_ret_sample_body, seq_rows=seq_rows),
        grid=(steps,),
        in_specs=[row_spec(HK), row_spec(HK), row_spec(HV),
                  pl.BlockSpec(st_block, lambda i: (ret_layer * steps + i, 0, 0, 0))] + [_resident(t.shape) for t in tabs],
        out_specs=[row_spec(HV), pl.BlockSpec(st_block, lambda i: (i, 0, 0, 0))],
        out_shape=[jax.ShapeDtypeStruct((n_seq * seq_rows, HV), F32),
                   jax.ShapeDtypeStruct((n_seq, RET_HEADS, RET_DK, RET_DV), F32)],
        compiler_params=_params(("arbitrary",)),
        name="ret_sample",
    )(q, k, v, states, *tabs)


def _retout_body(o_ref, gate_ref, x_ref, gn_ref, w_ref, out_ref):
    out_ref[...] = x_ref[...] + _dot(_gn_gate(o_ref[...], gate_ref[...], gn_ref[0]), w_ref[0])


def _ret_out(o, gate, x2d, gn_g, w_out, ret_layer):
    rows, d = x2d.shape
    tile = min(rows, ROW_TILE)
    row_spec = lambda w: pl.BlockSpec((tile, w), lambda r: (r, 0))
    return pl.pallas_call(
        _retout_body,
        grid=(rows // tile,),
        in_specs=[row_spec(HV), row_spec(HV), row_spec(d), _layer_resident(gn_g.shape, ret_layer),
                  _layer_resident(w_out.shape, ret_layer)],
        out_specs=row_spec(d),
        out_shape=jax.ShapeDtypeStruct((rows, d), F32),
        compiler_params=_params(("arbitrary",)),
        name="ret_out",
    )(o, gate, x2d, gn_g, w_out)


def _softmax(s):
    e = jnp.exp(s - jnp.max(s, axis=-1, keepdims=True))
    return e / jnp.sum(e, axis=-1, keepdims=True)


def _xattn_block(x_ref, g_ref, wq_ref, wo_ref, k_ref, v_ref, out_ref, att_ref, *s_refs, seq_rows):
    n_mem = k_ref.shape[1] // KV_ROW_STRIDE
    head_cols = [slice(hd * X_HEAD_DIM, (hd + 1) * X_HEAD_DIM) for hd in range(X_HEADS)]

    def project_q(rows):
        h = _rms(x_ref[rows], g_ref[0]).astype(BF16)
        return _dot(h, wq_ref[0]) * (X_HEAD_DIM ** -0.5)

    def project_out(rows):
        out_ref[rows] = x_ref[rows] + _dot(att_ref[rows].astype(BF16), wo_ref[0])

    if s_refs:
        s_ref, = s_refs
        q = project_q(slice(None))
        pairs = [(slice(i * seq_rows, (i + 1) * seq_rows), cols, i, hd)
                 for i in range(k_ref.shape[0]) for hd, cols in enumerate(head_cols)]
        for n, (rows, cols, i, hd) in enumerate(pairs):
            s_ref[n * seq_rows:(n + 1) * seq_rows] = _dot_nt(q[rows, cols].astype(BF16), _head_kv(k_ref, i, hd, n_mem))
        s_ref[...] = _softmax(s_ref[...])
        for n, (rows, cols, i, hd) in enumerate(pairs):
            p = s_ref[n * seq_rows:(n + 1) * seq_rows]
            att_ref[rows, cols] = _dot(p.astype(BF16), _head_kv(v_ref, i, hd, n_mem))
        project_out(slice(None))
    else:
        q = project_q(slice(None))
        for hd, cols in enumerate(head_cols):
            p = _softmax(_dot_nt(q[:, cols].astype(BF16), _head_kv(k_ref, 0, hd, n_mem)))
            att_ref[:, cols] = _dot(p.astype(BF16), _head_kv(v_ref, 0, hd, n_mem))
        project_out(slice(None))


def _xattn_body(xl_ref, xs_ref, g_ref, wq_ref, wo_ref, kl_ref, vl_ref, ks_ref, vs_ref, outl_ref, outs_ref,
                attl_ref, atts_ref, score_ref, *, long_steps, short_steps, long_rows, short_rows):
    step = pl.program_id(0)
    idx = step // 2

    @pl.when((step % 2 == 0) & (idx < long_steps))
    def _():
        _xattn_block(xl_ref, g_ref, wq_ref, wo_ref, kl_ref, vl_ref, outl_ref, attl_ref, seq_rows=long_rows)

    @pl.when((step % 2 == 1) & (idx < short_steps))
    def _():
        _xattn_block(xs_ref, g_ref, wq_ref, wo_ref, ks_ref, vs_ref, outs_ref, atts_ref, score_ref,
                     seq_rows=short_rows)


def _xattn(x_long, x_short, gains, wq, wo, layer, kv_long, kv_short, n_long, n_short):
    d = x_long.shape[1]
    long_rows, short_rows = x_long.shape[0] // n_long, x_short.shape[0] // n_short
    assert long_rows % ROW_TILE == 0
    n_mem = kv_long[0].shape[1]
    tps = long_rows // ROW_TILE
    long_steps = n_long * tps
    bb = 4 if n_short % 4 == 0 else 1
    short_steps = n_short // bb
    li = lambda s: jnp.minimum(s // 2, long_steps - 1)
    si = lambda s: jnp.minimum(jnp.maximum(s - 1, 0) // 2, short_steps - 1)
    long_spec = pl.BlockSpec((ROW_TILE, d), lambda s: (li(s), 0))
    short_spec = pl.BlockSpec((bb * short_rows, d), lambda s: (si(s), 0))
    kvl_spec = pl.BlockSpec((1, n_mem, LANES), lambda s: (layer * n_long + li(s) // tps, 0, 0))
    kvs_spec = pl.BlockSpec((bb, n_mem, LANES), lambda s: (layer * short_steps + si(s), 0, 0))
    return pl.pallas_call(
        functools.partial(_xattn_body, long_steps=long_steps, short_steps=short_steps,
                          long_rows=ROW_TILE, short_rows=short_rows),
        grid=(2 * max(long_steps, short_steps),),
        in_specs=[long_spec, short_spec, _layer_resident(gains.shape, layer), _layer_resident(wq.shape, layer),
                  _layer_resident(wo.shape, layer), kvl_spec, kvl_spec, kvs_spec, kvs_spec],
        out_specs=[long_spec, short_spec],
        out_shape=[jax.ShapeDtypeStruct(x_long.shape, F32), jax.ShapeDtypeStruct(x_short.shape, F32)],
        scratch_shapes=[pltpu.VMEM((ROW_TILE, d), F32), pltpu.VMEM((bb * short_rows, d), F32),
                        pltpu.VMEM((bb * X_HEADS * short_rows, n_mem // KV_ROW_STRIDE), F32)],
        compiler_params=_params(("arbitrary",)),
        name="xattn",
    )(x_long, x_short, gains, wq, wo, *kv_long, *kv_short)


def _conv_chunk(u, prev, cwb, shift):
    c = cwb[3:4] + cwb[0:1] * shift(u, prev, 2)
    c = c + cwb[1:2] * shift(u, prev, 1)
    return c + cwb[2:3] * u


def _ffn_gate(ua, ug, prev_a, prev_g, cwb_a, cwb_g, shift):
    return (_conv_chunk(ua, prev_a, cwb_a, shift) * _silu(_conv_chunk(ug, prev_g, cwb_g, shift))).astype(BF16)


def _ffn_chunk(h, wa, wg, wd, prev_a, prev_g, cwb_a, cwb_g, shift):
    ua = _dot(h, wa)
    ug = _dot(h, wg)
    return _dot(_ffn_gate(ua, ug, prev_a, prev_g, cwb_a, cwb_g, shift), wd), ua, ug


def _ffn_prompt_body(x_ref, g_ref, gf_ref, wu_ref, wd_ref, cwb_ref, out_ref, c_ref,
                     h_ref, acc_ref, ua0_ref, ug0_ref, ua1_ref, ug1_ref, z0_ref, z1_ref, *, final_norm):
    halo = (CONV_W - 1) * SUBLANES
    tile = acc_ref.shape[0]
    groups = tile // SUBLANES

    @pl.when(pl.program_id(1) == 0)
    def _():
        c_ref[...] = jnp.zeros_like(c_ref)

    x = jnp.swapaxes(x_ref[...].reshape(SUBLANES, groups, D_MODEL), 0, 1).reshape(tile, D_MODEL)
    h_ref[...] = _rms(x, g_ref[0]).astype(BF16)
    acc_ref[...] = x
    slots = ((ua0_ref, ug0_ref), (ua1_ref, ug1_ref))
    starts = range(0, D_FF, FFN_PROMPT_CHUNK)
    n_chunks = len(starts)

    def cols(f):
        lo, hi = starts[f], min(starts[f] + FFN_PROMPT_CHUNK, D_FF)
        return slice(lo, hi), slice(D_FF + lo, D_FF + hi), hi - lo

    def up(f):
        ca, cg, width = cols(f)
        first_sublane = lax.broadcasted_iota(jnp.int32, (SUBLANES, width), 0) == 0
        for u_ref, cc in zip(slots[f % 2], (ca, cg)):
            u = _dot(h_ref[...], wu_ref[0, :, cc])
            u_ref[halo:, :width] = u
            for m in range(CONV_W - 1):
                rows = slice(m * SUBLANES, (m + 1) * SUBLANES)
                src = u[tile - halo + m * SUBLANES: tile - halo + (m + 1) * SUBLANES]
                u_ref[rows, :width] = jnp.where(first_sublane, pltpu.roll(c_ref[0, rows, cc], 1, axis=0),
                                                pltpu.roll(src, 1, axis=0))
                c_ref[0, rows, cc] = src

    def conv(u_ref, cw, r0, n, width):
        c = cw[3:4] + cw[0:1] * u_ref[r0: r0 + n, :width]
        c = c + cw[1:2] * u_ref[r0 + SUBLANES: r0 + SUBLANES + n, :width]
        return c + cw[2:3] * u_ref[r0 + halo: r0 + halo + n, :width]

    def gate(f):
        ca, cg, width = cols(f)
        ua_ref, ug_ref = slots[f % 2]
        cwa, cwg = cwb_ref[0, :, ca], cwb_ref[0, :, cg]
        z_ref = (z0_ref, z1_ref)[f % 2]
        for r0 in range(0, tile, FFN_GATE_ROWS):
            z_ref[r0: r0 + FFN_GATE_ROWS, :width] = (
                conv(ua_ref, cwa, r0, FFN_GATE_ROWS, width)
                * _silu(conv(ug_ref, cwg, r0, FFN_GATE_ROWS, width))).astype(BF16)
        return z_ref[:, :width]

    up(0)
    for f in range(n_chunks):
        if f + 1 < n_chunks:
            up(f + 1)
        acc_ref[...] += _dot(gate(f), wd_ref[0, cols(f)[0], :])
    y = acc_ref[...]
    if final_norm:
        y = _rms(y, gf_ref[...])
    out_ref[...] = jnp.swapaxes(y.reshape(groups, SUBLANES, D_MODEL), 0, 1).reshape(tile, D_MODEL)


def _ffn_prompt(x2d, gains, g_final, w_up, w_down, cwb, layer, n_seq, final_norm):
    rows, d = x2d.shape
    seq_rows = rows // n_seq
    tile = min(seq_rows, ROW_TILE)
    tps = seq_rows // tile
    halo = (CONV_W - 1) * SUBLANES
    row_spec = pl.BlockSpec((tile, d), lambda b, t: (b * tps + t, 0))
    u_scratch = pltpu.VMEM((halo + tile, FFN_PROMPT_CHUNK), F32)
    z_scratch = pltpu.VMEM((tile, FFN_PROMPT_CHUNK), BF16)
    y, carry = pl.pallas_call(
        functools.partial(_ffn_prompt_body, final_norm=final_norm),
        grid=(n_seq, tps),
        in_specs=[row_spec, _layer_resident(gains.shape, layer), _resident((1, d)),
                  _layer_resident(w_up.shape, layer), _layer_resident(w_down.shape, layer),
                  _layer_resident(cwb.shape, layer)],
        out_specs=[row_spec, pl.BlockSpec((1, halo, 2 * D_FF), lambda b, t: (b, 0, 0))],
        out_shape=[jax.ShapeDtypeStruct((rows, d), F32), jax.ShapeDtypeStruct((n_seq, halo, 2 * D_FF), F32)],
        scratch_shapes=[pltpu.VMEM((tile, d), BF16), pltpu.VMEM((tile, d), F32),
                        u_scratch, u_scratch, u_scratch, u_scratch, z_scratch, z_scratch],
        compiler_params=_params(("arbitrary", "arbitrary")),
        name="ffn_prompt",
    )(x2d, gains, g_final.reshape(1, d), w_up, w_down, cwb)
    return y, carry[:, SUBLANES - 1::SUBLANES]


def _ffn_sample_body(x_ref, g_ref, gf_ref, wa_ref, wg_ref, wd_ref, cwa_ref, cwg_ref, ba_ref, bg_ref,
                     out_ref, ca_ref, cg_ref, h_ref, acc_ref, *, n_seq, seq_rows, final_norm):
    f = pl.program_id(0)

    @pl.when(f == 0)
    def _():
        xt = _to_time_major(x_ref[...], n_seq, seq_rows)
        h_ref[...] = _rms(xt, g_ref[0]).astype(BF16)
        acc_ref[...] = xt

    rows = n_seq * seq_rows
    hist = (CONV_W - 1) * n_seq

    def shift(u, prev, j):
        return jnp.concatenate([prev, u], axis=0)[hist - j * n_seq: hist - j * n_seq + rows]

    y, ua, ug = _ffn_chunk(h_ref[...], wa_ref[0], wg_ref[0], wd_ref[0], ba_ref[...], bg_ref[...],
                           cwa_ref[0], cwg_ref[0], shift)
    acc_ref[...] += y
    ca_ref[...] = ua[rows - hist:]
    cg_ref[...] = ug[rows - hist:]

    @pl.when(f == pl.num_programs(0) - 1)
    def _():
        yt = acc_ref[...]
        if final_norm:
            yt = _rms(yt, gf_ref[...])
        out_ref[...] = _from_time_major(yt, n_seq, seq_rows)


def _ffn_sample(x2d, gains, g_final, w_up, w_down, cwb, layer, buf, n_seq, final_norm):
    rows, d = x2d.shape
    seq_rows = rows // n_seq
    hist = (CONV_W - 1) * n_seq
    a_col = lambda f: f
    g_col = lambda f: NF + f
    c_shape = jax.ShapeDtypeStruct((hist, D_FF), F32)
    c_spec = pl.BlockSpec((hist, FF_CHUNK), lambda f: (0, f))
    return pl.pallas_call(
        functools.partial(_ffn_sample_body, n_seq=n_seq, seq_rows=seq_rows, final_norm=final_norm),
        grid=(NF,),
        in_specs=[_resident((rows, d)), _layer_resident(gains.shape, layer), _resident((1, d)),
                  pl.BlockSpec((1, d, FF_CHUNK), lambda f: (layer, 0, a_col(f))),
                  pl.BlockSpec((1, d, FF_CHUNK), lambda f: (layer, 0, g_col(f))),
                  pl.BlockSpec((1, FF_CHUNK, d), lambda f: (layer, f, 0)),
                  pl.BlockSpec((1, 4, FF_CHUNK), lambda f: (layer, 0, a_col(f))),
                  pl.BlockSpec((1, 4, FF_CHUNK), lambda f: (layer, 0, g_col(f))),
                  pl.BlockSpec((hist, FF_CHUNK), lambda f: (0, a_col(f))),
                  pl.BlockSpec((hist, FF_CHUNK), lambda f: (0, g_col(f)))],
        out_specs=[pl.BlockSpec((rows, d), lambda f: (0, 0)), c_spec, c_spec],
        out_shape=[jax.ShapeDtypeStruct((rows, d), F32), c_shape, c_shape],
        scratch_shapes=[pltpu.VMEM((rows, d), BF16), pltpu.VMEM((rows, d), F32)],
        compiler_params=_params(("arbitrary",)),
        name="ffn_sample",
    )(x2d, gains, g_final.reshape(1, d), w_up, w_up, w_down, cwb, cwb, buf, buf)


def _pool_mix(pooled, w_ref, scale_ref, x):
    mixed = [_dot(pooled[g].astype(BF16), w_ref[0, g]) for g in range(len(POOL_WINDOWS))]
    return x + jnp.concatenate(mixed, axis=-1) * scale_ref[0]


def _pool_prompt_body(x_ref, g_ref, w_ref, scale_ref, out_ref, hist_ref):
    t = pl.program_id(1)

    @pl.when(t == 0)
    def _():
        hist_ref[...] = jnp.zeros_like(hist_ref)

    x = x_ref[...]
    tile = x.shape[0]
    h = _rms(x, g_ref[0])
    ext = jnp.concatenate([hist_ref[0], h], axis=0)
    hist_ref[0] = h[tile - POOL_CARRY:]
    pos1 = (t * tile + 1 + lax.broadcasted_iota(jnp.int32, (tile, POOL_GC), 0)).astype(F32)
    pooled = []
    s = ext
    for g, w in enumerate(POOL_WINDOWS):
        s = s[:, POOL_GC * (1 if g else 0):]
        s = s + pltpu.roll(s, w // 2, axis=0)
        cnt = jnp.minimum(pos1, float(w))
        pooled.append(s[POOL_CARRY:, :POOL_GC] / cnt - h[:, g * POOL_GC:(g + 1) * POOL_GC])
    out_ref[...] = _pool_mix(pooled, w_ref, scale_ref, x)


def _pool_prompt(x2d, gains, layer, pool_w, pool_scale, pool_layer, n_seq):
    rows, d = x2d.shape
    seq_rows = rows // n_seq
    tile = min(seq_rows, ROW_TILE)
    tps = seq_rows // tile
    row_spec = pl.BlockSpec((tile, d), lambda b, t: (b * tps + t, 0))
    return pl.pallas_call(
        _pool_prompt_body,
        grid=(n_seq, tps),
        in_specs=[row_spec, _layer_resident(gains.shape, layer), _layer_resident(pool_w.shape, pool_layer),
                  _layer_resident(pool_scale.shape, pool_layer)],
        out_specs=[row_spec, pl.BlockSpec((1, POOL_CARRY, d), lambda b, t: (b, 0, 0))],
        out_shape=[jax.ShapeDtypeStruct((rows, d), F32), jax.ShapeDtypeStruct((n_seq, POOL_CARRY, d), F32)],
        compiler_params=_params(("arbitrary", "arbitrary")),
        name="pool_prompt",
    )(x2d, gains, pool_w, pool_scale)


def _pool_sample_body(x_ref, g_ref, w_ref, scale_ref, buf_ref, out_ref, h_ref, *, n_seq, seq_rows, pos0):
    xt = _to_time_major(x_ref[...], n_seq, seq_rows)
    h = _rms(xt, g_ref[0])
    s = jnp.concatenate([buf_ref[...], h], axis=0)
    rows = n_seq * seq_rows
    first = POOL_BUF
    pooled = []
    for g, w in enumerate(POOL_WINDOWS):
        s = s[:, POOL_GC * (1 if g else 0):]
        step = (w // 2) * n_seq
        s = s[step:] + s[:-step]
        first -= w // 2
        win = s[first * n_seq: first * n_seq + rows, :POOL_GC]
        inv = [1.0 / min(pos0 + t + 1, w) for t in range(seq_rows)]
        if len(set(inv)) == 1:
            win = win * inv[0]
        else:
            win = jnp.concatenate([win[t * n_seq:(t + 1) * n_seq] * inv[t] for t in range(seq_rows)], axis=0)
        pooled.append(win - h[:, g * POOL_GC:(g + 1) * POOL_GC])
    out_ref[...] = _from_time_major(_pool_mix(pooled, w_ref, scale_ref, xt), n_seq, seq_rows)
    h_ref[...] = _from_time_major(h, n_seq, seq_rows)


def _pool_sample(x2d, gains, layer, pool_w, pool_scale, pool_layer, buf_t, n_seq, pos0):
    rows, d = x2d.shape
    seq_rows = rows // n_seq
    full = lambda s: pl.BlockSpec(s, lambda i: (0,) * len(s))
    return pl.pallas_call(
        functools.partial(_pool_sample_body, n_seq=n_seq, seq_rows=seq_rows, pos0=pos0),
        grid=(1,),
        in_specs=[full((rows, d)), _layer_resident(gains.shape, layer), _layer_resident(pool_w.shape, pool_layer),
                  _layer_resident(pool_scale.shape, pool_layer), full(buf_t.shape)],
        out_specs=[full((rows, d)), full((rows, d))],
        out_shape=[jax.ShapeDtypeStruct((rows, d), F32)] * 2,
        compiler_params=_params(("arbitrary",)),
        name="pool_sample",
    )(x2d, gains, pool_w, pool_scale, buf_t)


def _rope_tables(pos):
    inv = 1.0 / (ROPE_BASE ** (jnp.arange(0, RET_DK, 2, dtype=F32) / RET_DK))
    ang = pos[:, None] * inv[None, :]
    return jnp.cos(ang), jnp.sin(ang)


def kernel(x_prompt, x_sample, mem_prompt, cache_mem_k, cache_mem_v, state_ret, cache_pool, cache_ffn_conv, w_ret_in, ret_gn, w_ret_out, pool_w, pool_scale, norm_mem, w_xq, w_xk, w_xv, w_xo, w_up, conv_w, conv_b, w_down, norm_mix, norm_xattn, norm_ffn, norm_final):
    bp, lp, d = x_prompt.shape
    bs, ls, _ = x_sample.shape
    depth = w_up.shape[0]
    n_mem = mem_prompt.shape[1]

    w_in_b, w_out_b, pool_w_b, wq_b, wk_b, wv_b, wo_b, w_up_b, w_down_b = (
        w.astype(BF16) for w in (w_ret_in, w_ret_out, pool_w, w_xq, w_xk, w_xv, w_xo, w_up, w_down))
    cwb = jnp.concatenate([conv_w, conv_b[:, None, :]], axis=1)
    g_mix, g_x, g_ffn = _gain(norm_mix), _gain(norm_xattn), _gain(norm_ffn)
    gn_g, p_scale = _gain(ret_gn), _gain(pool_scale)

    mem_k2, mem_v2 = _mem_kv(mem_prompt.reshape(bp * n_mem, d), norm_mem, wk_b, wv_b)
    kv_p = tuple(a.reshape(depth * bp, n_mem * KV_ROW_STRIDE, LANES) for a in (mem_k2, mem_v2))
    kv_s = (_kv_rows(cache_mem_k), _kv_rows(cache_mem_v))
    states = state_ret.reshape((state_ret.shape[0] * bs,) + state_ret.shape[2:])

    cos_p, sin_p = _rope_tables(jnp.arange(lp, dtype=F32))
    cos_s, sin_s = _rope_tables(PAST_LEN + jnp.arange(ls, dtype=F32))
    cos_s, sin_s = jnp.tile(cos_s, (bs, 1)), jnp.tile(sin_s, (bs, 1))

    xp = x_prompt.reshape(bp * lp, d)
    xs = x_sample.reshape(bs * ls, d)
    ret_p, ret_s, pool_p, pool_s, conv_p, conv_s = [], [], [], [], [], []
    for i in range(depth):
        j = i // 2
        last = i == depth - 1
        if i % 2 == 0:
            q, k, v, gate = _ret_proj(xp, g_mix, i, cos_p, sin_p, w_in_b, j, bp, BF16)
            xp, s = _ret_prompt(q, k, v, gate, xp, gn_g, w_out_b, j, bp)
            ret_p.append(s)
            q, k, v, gate = _ret_proj(xs, g_mix, i, cos_s, sin_s, w_in_b, j, 1, F32)
            o, s = _ret_sample(q, k, v, states, j, bs, ls)
            xs = _ret_out(o, gate, xs, gn_g, w_out_b, j)
            ret_s.append(s)
        else:
            xp, hist = _pool_prompt(xp, g_mix, i, pool_w_b, p_scale, j, bp)
            pool_p.append(hist[:, POOL_CARRY - POOL_BUF:])
            buf_t = jnp.swapaxes(cache_pool[j], 0, 1).reshape(POOL_BUF * bs, d)
            xs, hs = _pool_sample(xs, g_mix, i, pool_w_b, p_scale, j, buf_t, bs, PAST_LEN)
            pool_s.append(jnp.concatenate([cache_pool[j], hs.reshape(bs, ls, d)], axis=1)[:, -POOL_BUF:])

        xp, xs = _xattn(xp, xs, g_x, wq_b, wo_b, i, kv_p, kv_s, bp, bs)

        xp, tail = _ffn_prompt(xp, g_ffn, norm_final, w_up_b, w_down_b, cwb, i, bp, last)
        conv_p.append(tail)
        buf = jnp.swapaxes(cache_ffn_conv[i], 0, 1).reshape((CONV_W - 1) * bs, 2 * D_FF)
        xs, ca, cg = _ffn_sample(xs, g_ffn, norm_final, w_up_b, w_down_b, cwb, i, buf, bs, last)
        tail = jnp.concatenate([ca, cg], axis=-1)
        conv_s.append(jnp.swapaxes(tail.reshape(CONV_W - 1, bs, 2 * D_FF), 0, 1))

    mk = _kv_unrows(mem_k2, (depth, bp), n_mem)
    mv = _kv_unrows(mem_v2, (depth, bp), n_mem)
    return (xp.reshape(bp, lp, d), xs.reshape(bs, ls, d), jnp.stack(ret_p), jnp.stack(ret_s).astype(state_ret.dtype),
            jnp.stack(pool_p), jnp.stack(pool_s), jnp.stack(conv_p), jnp.stack(conv_s), mk, mv)
```

```python
import functools

import jax
import jax.numpy as jnp
from jax import lax
from jax.experimental import pallas as pl
from jax.experimental.pallas import tpu as pltpu

F32 = jnp.float32
BF16 = jnp.bfloat16

D_MODEL = 1024
PAST_LEN = 16384
RET_HEADS = 4
RET_DK = D_MODEL // RET_HEADS
RET_DV = 2 * D_MODEL // RET_HEADS
HK = RET_HEADS * RET_DK
HV = RET_HEADS * RET_DV
ROPE_BASE = 10000.0
POOL_WINDOWS = (2, 4, 8, 16)
POOL_GC = D_MODEL // len(POOL_WINDOWS)
POOL_BUF = max(POOL_WINDOWS) - 1
X_HEADS = 4
X_HEAD_DIM = D_MODEL // X_HEADS
D_FF = 2816
CONV_W = 3
NORM_EPS = 1e-6
GN_EPS = 1e-5

SUBLANES = 8
LANES = 128
KV_LANE_TILES = X_HEAD_DIM // LANES
KV_ROW_STRIDE = X_HEADS * KV_LANE_TILES
assert KV_ROW_STRIDE == SUBLANES
ROW_TILE = 512
FFN_ROW_TILE = 1024
RET_CHUNK_PROMPT = 256
FF_CHUNK = 256
NF = D_FF // FF_CHUNK
FFN_PROMPT_CHUNK = 512
FFN_GATE_ROWS = 64
POOL_CARRY = 16
VMEM_LIMIT = 56 * 1024 * 1024


def _params(sem):
    return pltpu.CompilerParams(dimension_semantics=sem, vmem_limit_bytes=VMEM_LIMIT)


def _resident(shape):
    zeros = (0,) * len(shape)
    return pl.BlockSpec(shape, lambda *_: zeros, pipeline_mode=pl.Buffered(1))


def _layer_resident(stacked_shape, layer):
    idx = (layer,) + (0,) * (len(stacked_shape) - 1)
    return pl.BlockSpec((1,) + tuple(stacked_shape[1:]), lambda *_: idx, pipeline_mode=pl.Buffered(1))


def _gain(g):
    return g.reshape(g.shape[0], 1, g.shape[1])


def _dot(a, b):
    return jnp.dot(a, b, preferred_element_type=F32)


def _dot_nt(a, b):
    return lax.dot_general(a, b, (((1,), (1,)), ((), ())), preferred_element_type=F32)


def _dot_tn(a, b):
    return lax.dot_general(a, b, (((0,), (0,)), ((), ())), preferred_element_type=F32)


def _rms(x, g):
    return x * lax.rsqrt(jnp.mean(x * x, axis=-1, keepdims=True) + NORM_EPS) * g


def _silu(x):
    return x * (1.0 / (1.0 + jnp.exp(-x)))


def _to_time_major(x, n_seq, seq_rows):
    return jnp.swapaxes(x.reshape(n_seq, seq_rows, x.shape[-1]), 0, 1).reshape(x.shape)


def _from_time_major(x, n_seq, seq_rows):
    return jnp.swapaxes(x.reshape(seq_rows, n_seq, x.shape[-1]), 0, 1).reshape(x.shape)


def _memkv_body(mem_ref, g_ref, wk_ref, wv_ref, k_ref, v_ref):
    h = _rms(mem_ref[...], g_ref[0]).astype(BF16)
    for w_ref, o_ref in ((wk_ref, k_ref), (wv_ref, v_ref)):
        y = _dot(h, w_ref[0])
        for hd in range(X_HEADS):
            for c in range(KV_LANE_TILES):
                col = hd * X_HEAD_DIM + c * LANES
                o_ref[0, pl.ds(c * X_HEADS + hd, y.shape[0], stride=KV_ROW_STRIDE), :] = y[:, col:col + LANES]


def _mem_kv(mem2d, norm_mem, wk, wv):
    rows, d = mem2d.shape
    depth = wk.shape[0]
    tile = min(rows, 1024)
    w_spec = pl.BlockSpec((1, d, d), lambda i, r: (i, 0, 0))
    o_spec = pl.BlockSpec((1, tile * KV_ROW_STRIDE, LANES), lambda i, r: (i, r, 0))
    return pl.pallas_call(
        _memkv_body,
        grid=(depth, rows // tile),
        in_specs=[pl.BlockSpec((tile, d), lambda i, r: (r, 0)),
                  pl.BlockSpec((1, 1, d), lambda i, r: (i, 0, 0)), w_spec, w_spec],
        out_specs=[o_spec, o_spec],
        out_shape=[jax.ShapeDtypeStruct((depth, rows * KV_ROW_STRIDE, LANES), F32)] * 2,
        compiler_params=_params(("arbitrary", "arbitrary")),
        name="mem_kv",
    )(mem2d, _gain(norm_mem), wk, wv)


def _kv_rows(a):
    lead, n_mem = a.shape[:-3], a.shape[-3]
    a = a.reshape(lead + (n_mem, X_HEADS, KV_LANE_TILES, LANES))
    a = jnp.swapaxes(a, -3, -2)
    return a.reshape((-1, n_mem * KV_ROW_STRIDE, LANES))


def _kv_unrows(a, lead, n_mem):
    a = a.reshape(lead + (n_mem, KV_LANE_TILES, X_HEADS, LANES))
    a = jnp.swapaxes(a, -3, -2)
    return a.reshape(lead + (n_mem, X_HEADS, X_HEAD_DIM))


def _head_kv(ref, i, hd, n_mem):
    parts = [ref[i, pl.ds(c * X_HEADS + hd, n_mem, stride=KV_ROW_STRIDE), :] for c in range(KV_LANE_TILES)]
    return jnp.concatenate(parts, axis=1).astype(BF16)


def _retproj_body(x_ref, g_ref, cos_ref, sin_ref, w_ref, q_ref, k_ref, v_ref, gate_ref):
    h = _rms(x_ref[...], g_ref[0]).astype(BF16)
    cos = cos_ref[...]
    sin = sin_ref[...]
    half = RET_DK // 2

    def rope_store(col0, out_ref, scale):
        p = _dot(h, w_ref[0, :, col0:col0 + HK])
        for hd in range(RET_HEADS):
            a = p[:, hd * RET_DK: hd * RET_DK + half]
            b = p[:, hd * RET_DK + half: (hd + 1) * RET_DK]
            out_ref[:, hd * RET_DK: hd * RET_DK + half] = ((a * cos - b * sin) * scale).astype(out_ref.dtype)
            out_ref[:, hd * RET_DK + half: (hd + 1) * RET_DK] = ((b * cos + a * sin) * scale).astype(out_ref.dtype)

    rope_store(0, q_ref, 1.0)
    rope_store(HK, k_ref, RET_DK ** -0.5)
    v_ref[...] = _dot(h, w_ref[0, :, 2 * HK: 2 * HK + HV]).astype(v_ref.dtype)
    gate_ref[...] = _dot(h, w_ref[0, :, 2 * HK + HV:]).astype(gate_ref.dtype)


def _ret_proj(x2d, gains, layer, cos, sin, w_in, ret_layer, n_seq, out_dtype):
    rows, d = x2d.shape
    seq_rows = rows // n_seq
    tile = min(seq_rows, ROW_TILE)
    tps = seq_rows // tile
    row_spec = lambda w: pl.BlockSpec((tile, w), lambda b, t: (b * tps + t, 0))
    tab_spec = pl.BlockSpec((tile, RET_DK // 2), lambda b, t: (t, 0))
    return pl.pallas_call(
        _retproj_body,
        grid=(n_seq, tps),
        in_specs=[row_spec(d), _layer_resident(gains.shape, layer), tab_spec, tab_spec,
                  _layer_resident(w_in.shape, ret_layer)],
        out_specs=[row_spec(HK), row_spec(HK), row_spec(HV), row_spec(HV)],
        out_shape=[jax.ShapeDtypeStruct((rows, HK), out_dtype), jax.ShapeDtypeStruct((rows, HK), out_dtype),
                   jax.ShapeDtypeStruct((rows, HV), out_dtype), jax.ShapeDtypeStruct((rows, HV), out_dtype)],
        compiler_params=_params(("arbitrary", "arbitrary")),
        name="ret_proj",
    )(x2d, gains, cos, sin, w_in)


def _ret_step(qh, kh, vh, s_old, inner, qdec, kdec, cdec):
    qb = qh.astype(BF16)
    kb = kh.astype(BF16)
    vb = vh.astype(BF16)
    p = (_dot_nt(qb, kb) * inner).astype(BF16)
    o = _dot(p, vb) + _dot(qb, s_old.astype(BF16)) * qdec
    kd = (kh.astype(F32) * kdec).astype(BF16)
    s_new = s_old * cdec + _dot_tn(kd, vb)
    return o, s_new


def _decay_tables(chunk):
    lg = jnp.log(1.0 - 2.0 ** (-5.0 - jnp.arange(RET_HEADS, dtype=F32)))
    idx = jnp.arange(chunk, dtype=F32)
    rel = idx[:, None] - idx[None, :]
    inner = jnp.where(rel[None] >= 0, jnp.exp(jnp.maximum(rel, 0.0)[None] * lg[:, None, None]), 0.0)
    q_dec = jnp.exp((idx + 1.0)[None, :] * lg[:, None])[..., None]
    k_dec = jnp.exp((chunk - 1.0 - idx)[None, :] * lg[:, None])[..., None]
    c_dec = jnp.exp(chunk * lg)[:, None, None]
    return (inner, jnp.broadcast_to(q_dec, (RET_HEADS, chunk, RET_DV)),
            jnp.broadcast_to(k_dec, (RET_HEADS, chunk, RET_DK)),
            jnp.broadcast_to(c_dec, (RET_HEADS, 1, RET_DV)))


def _gn_gate(o, gate, gn):
    parts = []
    for h in range(RET_HEADS):
        oh = o[:, h * RET_DV:(h + 1) * RET_DV]
        mu = jnp.mean(oh, axis=-1, keepdims=True)
        dlt = oh - mu
        var = jnp.mean(dlt * dlt, axis=-1, keepdims=True)
        parts.append(dlt * lax.rsqrt(var + GN_EPS))
    on = jnp.concatenate(parts, axis=-1) * gn
    return (_silu(gate.astype(F32)) * on).astype(BF16)


def _ret_prompt_body(q_ref, k_ref, v_ref, gate_ref, x_ref, gn_ref, w_ref, inner_ref, qdec_ref, kdec_ref, cdec_ref,
                     out_ref, s_ref, o_ref, *, chunk):
    @pl.when(pl.program_id(1) == 0)
    def _():
        s_ref[...] = jnp.zeros_like(s_ref)

    chunks = [slice(c * chunk, (c + 1) * chunk) for c in range(q_ref.shape[0] // chunk)]
    for rows in chunks:
        for h in range(RET_HEADS):
            o, s_new = _ret_step(q_ref[rows, h * RET_DK:(h + 1) * RET_DK], k_ref[rows, h * RET_DK:(h + 1) * RET_DK],
                                 v_ref[rows, h * RET_DV:(h + 1) * RET_DV], s_ref[0, h],
                                 inner_ref[h], qdec_ref[h], kdec_ref[h], cdec_ref[h])
            o_ref[rows, h * RET_DV:(h + 1) * RET_DV] = o
            s_ref[0, h] = s_new
    for rows in chunks:
        z = _gn_gate(o_ref[rows], gate_ref[rows], gn_ref[0])
        out_ref[rows] = x_ref[rows] + _dot(z, w_ref[0])


def _ret_prompt(q, k, v, gate, x2d, gn_g, w_out, ret_layer, n_seq):
    rows, d = x2d.shape
    seq_rows = rows // n_seq
    tile = min(seq_rows, ROW_TILE)
    chunk = min(tile, RET_CHUNK_PROMPT)
    tps = seq_rows // tile
    tabs = _decay_tables(chunk)
    row_spec = lambda w: pl.BlockSpec((tile, w), lambda b, t: (b * tps + t, 0))
    return pl.pallas_call(
        functools.partial(_ret_prompt_body, chunk=chunk),
        grid=(n_seq, tps),
        in_specs=[row_spec(HK), row_spec(HK), row_spec(HV), row_spec(HV), row_spec(d),
                  _layer_resident(gn_g.shape, ret_layer), _layer_resident(w_out.shape, ret_layer)]
                 + [_resident(t.shape) for t in tabs],
        out_specs=[row_spec(d), pl.BlockSpec((1, RET_HEADS, RET_DK, RET_DV), lambda b, t: (b, 0, 0, 0))],
        out_shape=[jax.ShapeDtypeStruct((rows, d), F32),
                   jax.ShapeDtypeStruct((n_seq, RET_HEADS, RET_DK, RET_DV), F32)],
        scratch_shapes=[pltpu.VMEM((tile, HV), F32)],
        compiler_params=_params(("arbitrary", "arbitrary")),
        name="ret_prompt",
    )(q, k, v, gate, x2d, gn_g, w_out, *tabs)


def _ret_sample_body(q_ref, k_ref, v_ref, s0_ref, inner_ref, qdec_ref, kdec_ref, cdec_ref, o_ref, s_ref, *, seq_rows):
    for i in range(s0_ref.shape[0]):
        rows = slice(i * seq_rows, (i + 1) * seq_rows)
        for h in range(RET_HEADS):
            o, s_new = _ret_step(q_ref[rows, h * RET_DK:(h + 1) * RET_DK], k_ref[rows, h * RET_DK:(h + 1) * RET_DK],
                                 v_ref[rows, h * RET_DV:(h + 1) * RET_DV], s0_ref[i, h],
                                 inner_ref[h], qdec_ref[h], kdec_ref[h], cdec_ref[h])
            o_ref[rows, h * RET_DV:(h + 1) * RET_DV] = o
            s_ref[i, h] = s_new


def _proj_and_sample_body(x_ref, g_ref, cos_ref, sin_ref, w_ref, qs_ref, ks_ref, vs_ref, s0_ref,
                          inner_ref, qdec_ref, kdec_ref, cdec_ref, q_ref, k_ref, v_ref, gate_ref, os_ref, ss_ref, *,
                          period, short_steps, seq_rows):
    step = pl.program_id(0)
    group, member = step // period, step % period

    @pl.when(member == 0)
    def _():
        _retproj_body(x_ref, g_ref, cos_ref, sin_ref, w_ref, q_ref, k_ref, v_ref, gate_ref)

    @pl.when((member >= 1) & (group * (period - 1) + member - 1 < short_steps))
    def _():
        _ret_sample_body(qs_ref, ks_ref, vs_ref, s0_ref, inner_ref, qdec_ref, kdec_ref, cdec_ref, os_ref, ss_ref,
                         seq_rows=seq_rows)


def _proj_long_and_ret_short(x_long, gains, layer, cos, sin, w_in, ret_layer, n_long,
                             q_s, k_s, v_s, states, n_short):
    rows, d = x_long.shape
    long_rows = rows // n_long
    tile = min(long_rows, ROW_TILE)
    tps = long_rows // tile
    long_steps = rows // tile
    short_rows = q_s.shape[0] // n_short
    bb = 2 if n_short % 2 == 0 else 1
    short_steps = n_short // bb
    per_long = -(-short_steps // long_steps)
    period = 1 + per_long
    li = lambda s: s // period
    si = lambda s: jnp.clip((s // period) * per_long + s % period - 1, 0, short_steps - 1)
    tabs = _decay_tables(short_rows)
    long_spec = lambda w: pl.BlockSpec((tile, w), lambda s: (li(s), 0))
    tab_spec = pl.BlockSpec((tile, RET_DK // 2), lambda s: (li(s) % tps, 0))
    short_spec = lambda w: pl.BlockSpec((bb * short_rows, w), lambda s: (si(s), 0))
    st_block = (bb, RET_HEADS, RET_DK, RET_DV)
    return pl.pallas_call(
        functools.partial(_proj_and_sample_body, period=period, short_steps=short_steps, seq_rows=short_rows),
        grid=(long_steps * period,),
        in_specs=[long_spec(d), _layer_resident(gains.shape, layer), tab_spec, tab_spec,
                  _layer_resident(w_in.shape, ret_layer),
                  short_spec(HK), short_spec(HK), short_spec(HV),
                  pl.BlockSpec(st_block, lambda s: (ret_layer * short_steps + si(s), 0, 0, 0))]
                 + [_resident(t.shape) for t in tabs],
        out_specs=[long_spec(HK), long_spec(HK), long_spec(HV), long_spec(HV),
                   short_spec(HV), pl.BlockSpec(st_block, lambda s: (si(s), 0, 0, 0))],
        out_shape=[jax.ShapeDtypeStruct((rows, HK), BF16), jax.ShapeDtypeStruct((rows, HK), BF16),
                   jax.ShapeDtypeStruct((rows, HV), BF16), jax.ShapeDtypeStruct((rows, HV), BF16),
                   jax.ShapeDtypeStruct((n_short * short_rows, HV), F32),
                   jax.ShapeDtypeStruct((n_short, RET_HEADS, RET_DK, RET_DV), F32)],
        compiler_params=_params(("arbitrary",)),
        name="proj_long_ret_short",
    )(x_long, gains, cos, sin, w_in, q_s, k_s, v_s, states, *tabs)


def _retout_body(o_ref, gate_ref, x_ref, gn_ref, w_ref, out_ref):
    out_ref[...] = x_ref[...] + _dot(_gn_gate(o_ref[...], gate_ref[...], gn_ref[0]), w_ref[0])


def _ret_out(o, gate, x2d, gn_g, w_out, ret_layer):
    rows, d = x2d.shape
    tile = min(rows, ROW_TILE)
    row_spec = lambda w: pl.BlockSpec((tile, w), lambda r: (r, 0))
    return pl.pallas_call(
        _retout_body,
        grid=(rows // tile,),
        in_specs=[row_spec(HV), row_spec(HV), row_spec(d), _layer_resident(gn_g.shape, ret_layer),
                  _layer_resident(w_out.shape, ret_layer)],
        out_specs=row_spec(d),
        out_shape=jax.ShapeDtypeStruct((rows, d), F32),
        compiler_params=_params(("arbitrary",)),
        name="ret_out",
    )(o, gate, x2d, gn_g, w_out)


def _softmax(s):
    e = jnp.exp(s - jnp.max(s, axis=-1, keepdims=True))
    return e / jnp.sum(e, axis=-1, keepdims=True)


def _xattn_block(x_ref, g_ref, wq_ref, wo_ref, keys, values, out_ref, att_ref, *s_refs, n_seq, seq_rows):
    head_cols = [slice(hd * X_HEAD_DIM, (hd + 1) * X_HEAD_DIM) for hd in range(X_HEADS)]
    h = _rms(x_ref[...], g_ref[0]).astype(BF16)
    q = _dot(h, wq_ref[...]) * (X_HEAD_DIM ** -0.5)
    if s_refs:
        s_ref, = s_refs
        pairs = [(slice(i * seq_rows, (i + 1) * seq_rows), cols, i, hd)
                 for i in range(n_seq) for hd, cols in enumerate(head_cols)]
        for n, (rows, cols, i, hd) in enumerate(pairs):
            s_ref[n * seq_rows:(n + 1) * seq_rows] = _dot_nt(q[rows, cols].astype(BF16), keys(i, hd))
        s_ref[...] = _softmax(s_ref[...])
        for n, (rows, cols, i, hd) in enumerate(pairs):
            att_ref[rows, cols] = _dot(s_ref[n * seq_rows:(n + 1) * seq_rows].astype(BF16), values(i, hd))
    else:
        for hd, cols in enumerate(head_cols):
            p = _softmax(_dot_nt(q[:, cols].astype(BF16), keys(0, hd)))
            att_ref[:, cols] = _dot(p.astype(BF16), values(0, hd))
    out_ref[...] = x_ref[...] + _dot(att_ref[...].astype(BF16), wo_ref[...])


def _xattn_body(xl_ref, xs_ref, g_ref, wq_ref, wo_ref, kl_ref, vl_ref, ks_ref, vs_ref, outl_ref, outs_ref,
                attl_ref, atts_ref, score_ref, wqb_ref, wob_ref, kvl_ref, *,
                long_steps, short_steps, tiles_per_seq, short_seqs, short_rows):
    step = pl.program_id(0)
    idx = step // 2
    n_mem = kl_ref.shape[1] // KV_ROW_STRIDE

    @pl.when(step == 0)
    def _():
        wqb_ref[...] = wq_ref[0].astype(BF16)
        wob_ref[...] = wo_ref[0].astype(BF16)

    @pl.when((step % 2 == 0) & (idx < long_steps))
    def _():
        @pl.when(idx % tiles_per_seq == 0)
        def _():
            for hd in range(X_HEADS):
                kvl_ref[hd] = _head_kv(kl_ref, 0, hd, n_mem)
                kvl_ref[X_HEADS + hd] = _head_kv(vl_ref, 0, hd, n_mem)

        _xattn_block(xl_ref, g_ref, wqb_ref, wob_ref, lambda i, hd: kvl_ref[hd], lambda i, hd: kvl_ref[X_HEADS + hd],
                     outl_ref, attl_ref, n_seq=1, seq_rows=xl_ref.shape[0])

    @pl.when((step % 2 == 1) & (idx < short_steps))
    def _():
        _xattn_block(xs_ref, g_ref, wqb_ref, wob_ref, lambda i, hd: _head_kv(ks_ref, i, hd, n_mem),
                     lambda i, hd: _head_kv(vs_ref, i, hd, n_mem), outs_ref, atts_ref, score_ref,
                     n_seq=short_seqs, seq_rows=short_rows)


def _xattn(x_long, x_short, gains, wq, wo, layer, kv_long, kv_short, n_long, n_short):
    d = x_long.shape[1]
    long_rows, short_rows = x_long.shape[0] // n_long, x_short.shape[0] // n_short
    assert long_rows % ROW_TILE == 0
    n_mem = kv_long[0].shape[1]
    tps = long_rows // ROW_TILE
    long_steps = n_long * tps
    bb = 4 if n_short % 4 == 0 else 1
    short_steps = n_short // bb
    li = lambda s: jnp.minimum(s // 2, long_steps - 1)
    si = lambda s: jnp.minimum(jnp.maximum(s - 1, 0) // 2, short_steps - 1)
    long_spec = pl.BlockSpec((ROW_TILE, d), lambda s: (li(s), 0))
    short_spec = pl.BlockSpec((bb * short_rows, d), lambda s: (si(s), 0))
    kvl_spec = pl.BlockSpec((1, n_mem, LANES), lambda s: (layer * n_long + li(s) // tps, 0, 0))
    kvs_spec = pl.BlockSpec((bb, n_mem, LANES), lambda s: (layer * short_steps + si(s), 0, 0))
    mem_tokens = n_mem // KV_ROW_STRIDE
    return pl.pallas_call(
        functools.partial(_xattn_body, long_steps=long_steps, short_steps=short_steps, tiles_per_seq=tps,
                          short_seqs=bb, short_rows=short_rows),
        grid=(2 * max(long_steps, short_steps),),
        in_specs=[long_spec, short_spec, _layer_resident(gains.shape, layer), _layer_resident(wq.shape, layer),
                  _layer_resident(wo.shape, layer), kvl_spec, kvl_spec, kvs_spec, kvs_spec],
        out_specs=[long_spec, short_spec],
        out_shape=[jax.ShapeDtypeStruct(x_long.shape, F32), jax.ShapeDtypeStruct(x_short.shape, F32)],
        scratch_shapes=[pltpu.VMEM((ROW_TILE, d), F32), pltpu.VMEM((bb * short_rows, d), F32),
                        pltpu.VMEM((bb * X_HEADS * short_rows, mem_tokens), F32),
                        pltpu.VMEM((d, d), BF16), pltpu.VMEM((d, d), BF16),
                        pltpu.VMEM((2 * X_HEADS, mem_tokens, X_HEAD_DIM), BF16)],
        compiler_params=_params(("arbitrary",)),
        name="xattn",
    )(x_long, x_short, gains, wq, wo, *kv_long, *kv_short)


def _conv_chunk(u, prev, cwb, shift):
    c = cwb[3:4] + cwb[0:1] * shift(u, prev, 2)
    c = c + cwb[1:2] * shift(u, prev, 1)
    return c + cwb[2:3] * u


def _ffn_gate(ua, ug, prev_a, prev_g, cwb_a, cwb_g, shift):
    return (_conv_chunk(ua, prev_a, cwb_a, shift) * _silu(_conv_chunk(ug, prev_g, cwb_g, shift))).astype(BF16)


def _ffn_chunk(h, wa, wg, wd, prev_a, prev_g, cwb_a, cwb_g, shift):
    ua = _dot(h, wa)
    ug = _dot(h, wg)
    return _dot(_ffn_gate(ua, ug, prev_a, prev_g, cwb_a, cwb_g, shift), wd), ua, ug


def _ffn_prompt_body(x_ref, g_ref, gf_ref, wu_ref, wd_ref, cwb_ref, out_ref, c_ref,
                     h_ref, acc_ref, ua0_ref, ug0_ref, ua1_ref, ug1_ref, z0_ref, z1_ref, *, final_norm):
    halo = (CONV_W - 1) * SUBLANES
    tile = acc_ref.shape[0]
    groups = tile // SUBLANES

    @pl.when(pl.program_id(1) == 0)
    def _():
        c_ref[...] = jnp.zeros_like(c_ref)

    x = jnp.swapaxes(x_ref[...].reshape(SUBLANES, groups, D_MODEL), 0, 1).reshape(tile, D_MODEL)
    h_ref[...] = _rms(x, g_ref[0]).astype(BF16)
    acc_ref[...] = x
    slots = ((ua0_ref, ug0_ref), (ua1_ref, ug1_ref))
    starts = range(0, D_FF, FFN_PROMPT_CHUNK)
    n_chunks = len(starts)

    def cols(f):
        lo, hi = starts[f], min(starts[f] + FFN_PROMPT_CHUNK, D_FF)
        return slice(lo, hi), slice(D_FF + lo, D_FF + hi), hi - lo

    def up(f):
        ca, cg, width = cols(f)
        first_sublane = lax.broadcasted_iota(jnp.int32, (SUBLANES, width), 0) == 0
        for u_ref, cc in zip(slots[f % 2], (ca, cg)):
            u = _dot(h_ref[...], wu_ref[0, :, cc])
            u_ref[halo:, :width] = u
            for m in range(CONV_W - 1):
                rows = slice(m * SUBLANES, (m + 1) * SUBLANES)
                src = u[tile - halo + m * SUBLANES: tile - halo + (m + 1) * SUBLANES]
                u_ref[rows, :width] = jnp.where(first_sublane, pltpu.roll(c_ref[0, rows, cc], 1, axis=0),
                                                pltpu.roll(src, 1, axis=0))
                c_ref[0, rows, cc] = src

    def conv(u_ref, cw, r0, n, width):
        c = cw[3:4] + cw[0:1] * u_ref[r0: r0 + n, :width]
        c = c + cw[1:2] * u_ref[r0 + SUBLANES: r0 + SUBLANES + n, :width]
        return c + cw[2:3] * u_ref[r0 + halo: r0 + halo + n, :width]

    def gate(f):
        ca, cg, width = cols(f)
        ua_ref, ug_ref = slots[f % 2]
        cwa, cwg = cwb_ref[0, :, ca], cwb_ref[0, :, cg]
        z_ref = (z0_ref, z1_ref)[f % 2]
        for r0 in range(0, tile, FFN_GATE_ROWS):
            z_ref[r0: r0 + FFN_GATE_ROWS, :width] = (
                conv(ua_ref, cwa, r0, FFN_GATE_ROWS, width)
                * _silu(conv(ug_ref, cwg, r0, FFN_GATE_ROWS, width))).astype(BF16)
        return z_ref[:, :width]

    up(0)
    for f in range(n_chunks):
        if f + 1 < n_chunks:
            up(f + 1)
        acc_ref[...] += _dot(gate(f), wd_ref[0, cols(f)[0], :])
    y = acc_ref[...]
    if final_norm:
        y = _rms(y, gf_ref[...])
    out_ref[...] = jnp.swapaxes(y.reshape(groups, SUBLANES, D_MODEL), 0, 1).reshape(tile, D_MODEL)


def _ffn_prompt(x2d, gains, g_final, w_up, w_down, cwb, layer, n_seq, final_norm):
    rows, d = x2d.shape
    seq_rows = rows // n_seq
    tile = min(seq_rows, FFN_ROW_TILE)
    tps = seq_rows // tile
    halo = (CONV_W - 1) * SUBLANES
    row_spec = pl.BlockSpec((tile, d), lambda b, t: (b * tps + t, 0))
    u_scratch = pltpu.VMEM((halo + tile, FFN_PROMPT_CHUNK), F32)
    z_scratch = pltpu.VMEM((tile, FFN_PROMPT_CHUNK), BF16)
    y, carry = pl.pallas_call(
        functools.partial(_ffn_prompt_body, final_norm=final_norm),
        grid=(n_seq, tps),
        in_specs=[row_spec, _layer_resident(gains.shape, layer), _resident((1, d)),
                  _layer_resident(w_up.shape, layer), _layer_resident(w_down.shape, layer),
                  _layer_resident(cwb.shape, layer)],
        out_specs=[row_spec, pl.BlockSpec((1, halo, 2 * D_FF), lambda b, t: (b, 0, 0))],
        out_shape=[jax.ShapeDtypeStruct((rows, d), F32), jax.ShapeDtypeStruct((n_seq, halo, 2 * D_FF), F32)],
        scratch_shapes=[pltpu.VMEM((tile, d), BF16), pltpu.VMEM((tile, d), F32),
                        u_scratch, u_scratch, u_scratch, u_scratch, z_scratch, z_scratch],
        compiler_params=_params(("arbitrary", "arbitrary")),
        name="ffn_prompt",
    )(x2d, gains, g_final.reshape(1, d), w_up, w_down, cwb)
    return y, carry[:, SUBLANES - 1::SUBLANES]


def _ffn_sample_body(x_ref, g_ref, gf_ref, wa_ref, wg_ref, wd_ref, cwa_ref, cwg_ref, ba_ref, bg_ref,
                     out_ref, ca_ref, cg_ref, h_ref, acc_ref, *, n_seq, seq_rows, final_norm):
    f = pl.program_id(0)

    @pl.when(f == 0)
    def _():
        xt = _to_time_major(x_ref[...], n_seq, seq_rows)
        h_ref[...] = _rms(xt, g_ref[0]).astype(BF16)
        acc_ref[...] = xt

    rows = n_seq * seq_rows
    hist = (CONV_W - 1) * n_seq

    def shift(u, prev, j):
        return jnp.concatenate([prev, u], axis=0)[hist - j * n_seq: hist - j * n_seq + rows]

    y, ua, ug = _ffn_chunk(h_ref[...], wa_ref[0], wg_ref[0], wd_ref[0], ba_ref[...], bg_ref[...],
                           cwa_ref[0], cwg_ref[0], shift)
    acc_ref[...] += y
    ca_ref[...] = ua[rows - hist:]
    cg_ref[...] = ug[rows - hist:]

    @pl.when(f == pl.num_programs(0) - 1)
    def _():
        yt = acc_ref[...]
        if final_norm:
            yt = _rms(yt, gf_ref[...])
        out_ref[...] = _from_time_major(yt, n_seq, seq_rows)


def _ffn_sample(x2d, gains, g_final, w_up, w_down, cwb, layer, buf, n_seq, final_norm):
    rows, d = x2d.shape
    seq_rows = rows // n_seq
    hist = (CONV_W - 1) * n_seq
    a_col = lambda f: f
    g_col = lambda f: NF + f
    c_shape = jax.ShapeDtypeStruct((hist, D_FF), F32)
    c_spec = pl.BlockSpec((hist, FF_CHUNK), lambda f: (0, f))
    return pl.pallas_call(
        functools.partial(_ffn_sample_body, n_seq=n_seq, seq_rows=seq_rows, final_norm=final_norm),
        grid=(NF,),
        in_specs=[_resident((rows, d)), _layer_resident(gains.shape, layer), _resident((1, d)),
                  pl.BlockSpec((1, d, FF_CHUNK), lambda f: (layer, 0, a_col(f))),
                  pl.BlockSpec((1, d, FF_CHUNK), lambda f: (layer, 0, g_col(f))),
                  pl.BlockSpec((1, FF_CHUNK, d), lambda f: (layer, f, 0)),
                  pl.BlockSpec((1, 4, FF_CHUNK), lambda f: (layer, 0, a_col(f))),
                  pl.BlockSpec((1, 4, FF_CHUNK), lambda f: (layer, 0, g_col(f))),
                  pl.BlockSpec((hist, FF_CHUNK), lambda f: (0, a_col(f))),
                  pl.BlockSpec((hist, FF_CHUNK), lambda f: (0, g_col(f)))],
        out_specs=[pl.BlockSpec((rows, d), lambda f: (0, 0)), c_spec, c_spec],
        out_shape=[jax.ShapeDtypeStruct((rows, d), F32), c_shape, c_shape],
        scratch_shapes=[pltpu.VMEM((rows, d), BF16), pltpu.VMEM((rows, d), F32)],
        compiler_params=_params(("arbitrary",)),
        name="ffn_sample",
    )(x2d, gains, g_final.reshape(1, d), w_up, w_up, w_down, cwb, cwb, buf, buf)


def _pool_mix(pooled, w_ref, scale_ref, x):
    mixed = [_dot(pooled[g].astype(BF16), w_ref[0, g]) for g in range(len(POOL_WINDOWS))]
    return x + jnp.concatenate(mixed, axis=-1) * scale_ref[0]


def _pool_prompt_body(x_ref, g_ref, w_ref, scale_ref, out_ref, hist_ref):
    t = pl.program_id(1)

    @pl.when(t == 0)
    def _():
        hist_ref[...] = jnp.zeros_like(hist_ref)

    x = x_ref[...]
    tile = x.shape[0]
    h = _rms(x, g_ref[0])
    ext = jnp.concatenate([hist_ref[0], h], axis=0)
    hist_ref[0] = h[tile - POOL_CARRY:]
    pos1 = (t * tile + 1 + lax.broadcasted_iota(jnp.int32, (tile, POOL_GC), 0)).astype(F32)
    pooled = []
    s = ext
    for g, w in enumerate(POOL_WINDOWS):
        s = s[:, POOL_GC * (1 if g else 0):]
        s = s + pltpu.roll(s, w // 2, axis=0)
        cnt = jnp.minimum(pos1, float(w))
        pooled.append(s[POOL_CARRY:, :POOL_GC] / cnt - h[:, g * POOL_GC:(g + 1) * POOL_GC])
    out_ref[...] = _pool_mix(pooled, w_ref, scale_ref, x)


def _pool_prompt(x2d, gains, layer, pool_w, pool_scale, pool_layer, n_seq):
    rows, d = x2d.shape
    seq_rows = rows // n_seq
    tile = min(seq_rows, ROW_TILE)
    tps = seq_rows // tile
    row_spec = pl.BlockSpec((tile, d), lambda b, t: (b * tps + t, 0))
    return pl.pallas_call(
        _pool_prompt_body,
        grid=(n_seq, tps),
        in_specs=[row_spec, _layer_resident(gains.shape, layer), _layer_resident(pool_w.shape, pool_layer),
                  _layer_resident(pool_scale.shape, pool_layer)],
        out_specs=[row_spec, pl.BlockSpec((1, POOL_CARRY, d), lambda b, t: (b, 0, 0))],
        out_shape=[jax.ShapeDtypeStruct((rows, d), F32), jax.ShapeDtypeStruct((n_seq, POOL_CARRY, d), F32)],
        compiler_params=_params(("arbitrary", "arbitrary")),
        name="pool_prompt",
    )(x2d, gains, pool_w, pool_scale)


def _pool_sample_body(x_ref, g_ref, w_ref, scale_ref, buf_ref, out_ref, h_ref, *, n_seq, seq_rows, pos0):
    xt = _to_time_major(x_ref[...], n_seq, seq_rows)
    h = _rms(xt, g_ref[0])
    s = jnp.concatenate([buf_ref[...], h], axis=0)
    rows = n_seq * seq_rows
    first = POOL_BUF
    pooled = []
    for g, w in enumerate(POOL_WINDOWS):
        s = s[:, POOL_GC * (1 if g else 0):]
        step = (w // 2) * n_seq
        s = s[step:] + s[:-step]
        first -= w // 2
        win = s[first * n_seq: first * n_seq + rows, :POOL_GC]
        inv = [1.0 / min(pos0 + t + 1, w) for t in range(seq_rows)]
        if len(set(inv)) == 1:
            win = win * inv[0]
        else:
            win = jnp.concatenate([win[t * n_seq:(t + 1) * n_seq] * inv[t] for t in range(seq_rows)], axis=0)
        pooled.append(win - h[:, g * POOL_GC:(g + 1) * POOL_GC])
    out_ref[...] = _from_time_major(_pool_mix(pooled, w_ref, scale_ref, xt), n_seq, seq_rows)
    h_ref[...] = _from_time_major(h, n_seq, seq_rows)


def _pool_sample(x2d, gains, layer, pool_w, pool_scale, pool_layer, buf_t, n_seq, pos0):
    rows, d = x2d.shape
    seq_rows = rows // n_seq
    full = lambda s: pl.BlockSpec(s, lambda i: (0,) * len(s))
    return pl.pallas_call(
        functools.partial(_pool_sample_body, n_seq=n_seq, seq_rows=seq_rows, pos0=pos0),
        grid=(1,),
        in_specs=[full((rows, d)), _layer_resident(gains.shape, layer), _layer_resident(pool_w.shape, pool_layer),
                  _layer_resident(pool_scale.shape, pool_layer), full(buf_t.shape)],
        out_specs=[full((rows, d)), full((rows, d))],
        out_shape=[jax.ShapeDtypeStruct((rows, d), F32)] * 2,
        compiler_params=_params(("arbitrary",)),
        name="pool_sample",
    )(x2d, gains, pool_w, pool_scale, buf_t)


def _rope_tables(pos):
    inv = 1.0 / (ROPE_BASE ** (jnp.arange(0, RET_DK, 2, dtype=F32) / RET_DK))
    ang = pos[:, None] * inv[None, :]
    return jnp.cos(ang), jnp.sin(ang)


def kernel(x_prompt, x_sample, mem_prompt, cache_mem_k, cache_mem_v, state_ret, cache_pool, cache_ffn_conv, w_ret_in, ret_gn, w_ret_out, pool_w, pool_scale, norm_mem, w_xq, w_xk, w_xv, w_xo, w_up, conv_w, conv_b, w_down, norm_mix, norm_xattn, norm_ffn, norm_final):
    bp, lp, d = x_prompt.shape
    bs, ls, _ = x_sample.shape
    depth = w_up.shape[0]
    n_mem = mem_prompt.shape[1]

    w_in_b, w_out_b, pool_w_b, wk_b, wv_b, w_up_b, w_down_b = (
        w.astype(BF16) for w in (w_ret_in, w_ret_out, pool_w, w_xk, w_xv, w_up, w_down))
    cwb = jnp.concatenate([conv_w, conv_b[:, None, :]], axis=1)
    g_mix, g_x, g_ffn = _gain(norm_mix), _gain(norm_xattn), _gain(norm_ffn)
    gn_g, p_scale = _gain(ret_gn), _gain(pool_scale)

    mem_k2, mem_v2 = _mem_kv(mem_prompt.reshape(bp * n_mem, d), norm_mem, wk_b, wv_b)
    kv_p = tuple(a.reshape(depth * bp, n_mem * KV_ROW_STRIDE, LANES) for a in (mem_k2, mem_v2))
    kv_s = (_kv_rows(cache_mem_k), _kv_rows(cache_mem_v))
    states = state_ret.reshape((state_ret.shape[0] * bs,) + state_ret.shape[2:])

    cos_p, sin_p = _rope_tables(jnp.arange(lp, dtype=F32))
    cos_s, sin_s = _rope_tables(PAST_LEN + jnp.arange(ls, dtype=F32))
    cos_s, sin_s = jnp.tile(cos_s, (bs, 1)), jnp.tile(sin_s, (bs, 1))

    xp = x_prompt.reshape(bp * lp, d)
    xs = x_sample.reshape(bs * ls, d)
    ret_p, ret_s, pool_p, pool_s, conv_p, conv_s = [], [], [], [], [], []
    for i in range(depth):
        j = i // 2
        last = i == depth - 1
        if i % 2 == 0:
            qs, ks, vs, gate_s = _ret_proj(xs, g_mix, i, cos_s, sin_s, w_in_b, j, 1, F32)
            q, k, v, gate, o_s, s_s = _proj_long_and_ret_short(xp, g_mix, i, cos_p, sin_p, w_in_b, j, bp,
                                                               qs, ks, vs, states, bs)
            xp, s_p = _ret_prompt(q, k, v, gate, xp, gn_g, w_out_b, j, bp)
            xs = _ret_out(o_s, gate_s, xs, gn_g, w_out_b, j)
            ret_p.append(s_p)
            ret_s.append(s_s)
        else:
            xp, hist = _pool_prompt(xp, g_mix, i, pool_w_b, p_scale, j, bp)
            pool_p.append(hist[:, POOL_CARRY - POOL_BUF:])
            buf_t = jnp.swapaxes(cache_pool[j], 0, 1).reshape(POOL_BUF * bs, d)
            xs, hs = _pool_sample(xs, g_mix, i, pool_w_b, p_scale, j, buf_t, bs, PAST_LEN)
            pool_s.append(jnp.concatenate([cache_pool[j], hs.reshape(bs, ls, d)], axis=1)[:, -POOL_BUF:])

        xp, xs = _xattn(xp, xs, g_x, w_xq, w_xo, i, kv_p, kv_s, bp, bs)

        xp, tail = _ffn_prompt(xp, g_ffn, norm_final, w_up_b, w_down_b, cwb, i, bp, last)
        conv_p.append(tail)
        buf = jnp.swapaxes(cache_ffn_conv[i], 0, 1).reshape((CONV_W - 1) * bs, 2 * D_FF)
        xs, ca, cg = _ffn_sample(xs, g_ffn, norm_final, w_up_b, w_down_b, cwb, i, buf, bs, last)
        tail = jnp.concatenate([ca, cg], axis=-1)
        conv_s.append(jnp.swapaxes(tail.reshape(CONV_W - 1, bs, 2 * D_FF), 0, 1))

    mk = _kv_unrows(mem_k2, (depth, bp), n_mem)
    mv = _kv_unrows(mem_v2, (depth, bp), n_mem)
    return (xp.reshape(bp, lp, d), xs.reshape(bs, ls, d), jnp.stack(ret_p), jnp.stack(ret_s).astype(state_ret.dtype),
            jnp.stack(pool_p), jnp.stack(pool_s), jnp.stack(conv_p), jnp.stack(conv_s), mk, mv)
```

```python
import functools

import jax
import jax.numpy as jnp
from jax import lax
from jax.experimental import pallas as pl
from jax.experimental.pallas import tpu as pltpu

F32 = jnp.float32
BF16 = jnp.bfloat16

D_MODEL = 1024
PAST_LEN = 16384
RET_HEADS = 4
RET_DK = D_MODEL // RET_HEADS
RET_DV = 2 * D_MODEL // RET_HEADS
HK = RET_HEADS * RET_DK
HV = RET_HEADS * RET_DV
ROPE_BASE = 10000.0
POOL_WINDOWS = (2, 4, 8, 16)
POOL_GC = D_MODEL // len(POOL_WINDOWS)
POOL_BUF = max(POOL_WINDOWS) - 1
X_HEADS = 4
X_HEAD_DIM = D_MODEL // X_HEADS
D_FF = 2816
CONV_W = 3
NORM_EPS = 1e-6
GN_EPS = 1e-5

SUBLANES = 8
LANES = 128
KV_LANE_TILES = X_HEAD_DIM // LANES
KV_ROW_STRIDE = X_HEADS * KV_LANE_TILES
assert KV_ROW_STRIDE == SUBLANES
ROW_TILE = 512
FFN_ROW_TILE = 512
RET_CHUNK_PROMPT = 256
FF_CHUNK = 256
NF = D_FF // FF_CHUNK
FFN_PROMPT_CHUNK = 1536
FFN_GATE_ROWS = 64
POOL_CARRY = 16
VMEM_LIMIT = 56 * 1024 * 1024


def _params(sem):
    return pltpu.CompilerParams(dimension_semantics=sem, vmem_limit_bytes=VMEM_LIMIT)


def _resident(shape):
    zeros = (0,) * len(shape)
    return pl.BlockSpec(shape, lambda *_: zeros, pipeline_mode=pl.Buffered(1))


def _layer_resident(stacked_shape, layer):
    idx = (layer,) + (0,) * (len(stacked_shape) - 1)
    return pl.BlockSpec((1,) + tuple(stacked_shape[1:]), lambda *_: idx, pipeline_mode=pl.Buffered(1))


def _gain(g):
    return g.reshape(g.shape[0], 1, g.shape[1])


def _dot(a, b):
    return jnp.dot(a, b, preferred_element_type=F32)


def _dot_nt(a, b):
    return lax.dot_general(a, b, (((1,), (1,)), ((), ())), preferred_element_type=F32)


def _dot_tn(a, b):
    return lax.dot_general(a, b, (((0,), (0,)), ((), ())), preferred_element_type=F32)


def _rms(x, g):
    return x * lax.rsqrt(jnp.mean(x * x, axis=-1, keepdims=True) + NORM_EPS) * g


def _silu(x):
    return x * (1.0 / (1.0 + jnp.exp(-x)))


def _to_time_major(x, n_seq, seq_rows):
    return jnp.swapaxes(x.reshape(n_seq, seq_rows, x.shape[-1]), 0, 1).reshape(x.shape)


def _from_time_major(x, n_seq, seq_rows):
    return jnp.swapaxes(x.reshape(seq_rows, n_seq, x.shape[-1]), 0, 1).reshape(x.shape)


def _memkv_body(mem_ref, g_ref, wk_ref, wv_ref, k_ref, v_ref):
    h = _rms(mem_ref[...], g_ref[0]).astype(BF16)
    for w_ref, o_ref in ((wk_ref, k_ref), (wv_ref, v_ref)):
        y = _dot(h, w_ref[0])
        for hd in range(X_HEADS):
            for c in range(KV_LANE_TILES):
                col = hd * X_HEAD_DIM + c * LANES
                o_ref[0, pl.ds(c * X_HEADS + hd, y.shape[0], stride=KV_ROW_STRIDE), :] = y[:, col:col + LANES]


def _mem_kv(mem2d, norm_mem, wk, wv):
    rows, d = mem2d.shape
    depth = wk.shape[0]
    tile = min(rows, 1024)
    w_spec = pl.BlockSpec((1, d, d), lambda i, r: (i, 0, 0))
    o_spec = pl.BlockSpec((1, tile * KV_ROW_STRIDE, LANES), lambda i, r: (i, r, 0))
    return pl.pallas_call(
        _memkv_body,
        grid=(depth, rows // tile),
        in_specs=[pl.BlockSpec((tile, d), lambda i, r: (r, 0)),
                  pl.BlockSpec((1, 1, d), lambda i, r: (i, 0, 0)), w_spec, w_spec],
        out_specs=[o_spec, o_spec],
        out_shape=[jax.ShapeDtypeStruct((depth, rows * KV_ROW_STRIDE, LANES), F32)] * 2,
        compiler_params=_params(("arbitrary", "arbitrary")),
        name="mem_kv",
    )(mem2d, _gain(norm_mem), wk, wv)


def _kv_rows(a):
    lead, n_mem = a.shape[:-3], a.shape[-3]
    a = a.reshape(lead + (n_mem, X_HEADS, KV_LANE_TILES, LANES))
    a = jnp.swapaxes(a, -3, -2)
    return a.reshape((-1, n_mem * KV_ROW_STRIDE, LANES))


def _kv_unrows(a, lead, n_mem):
    a = a.reshape(lead + (n_mem, KV_LANE_TILES, X_HEADS, LANES))
    a = jnp.swapaxes(a, -3, -2)
    return a.reshape(lead + (n_mem, X_HEADS, X_HEAD_DIM))


def _head_kv(ref, i, hd, n_mem):
    parts = [ref[i, pl.ds(c * X_HEADS + hd, n_mem, stride=KV_ROW_STRIDE), :] for c in range(KV_LANE_TILES)]
    return jnp.concatenate(parts, axis=1).astype(BF16)


def _retproj_body(x_ref, g_ref, cos_ref, sin_ref, w_ref, q_ref, k_ref, v_ref, gate_ref):
    h = _rms(x_ref[...], g_ref[0]).astype(BF16)
    cos = cos_ref[...]
    sin = sin_ref[...]
    half = RET_DK // 2

    def rope_store(col0, out_ref, scale):
        p = _dot(h, w_ref[0, :, col0:col0 + HK])
        for hd in range(RET_HEADS):
            a = p[:, hd * RET_DK: hd * RET_DK + half]
            b = p[:, hd * RET_DK + half: (hd + 1) * RET_DK]
            out_ref[:, hd * RET_DK: hd * RET_DK + half] = ((a * cos - b * sin) * scale).astype(out_ref.dtype)
            out_ref[:, hd * RET_DK + half: (hd + 1) * RET_DK] = ((b * cos + a * sin) * scale).astype(out_ref.dtype)

    rope_store(0, q_ref, 1.0)
    rope_store(HK, k_ref, RET_DK ** -0.5)
    v_ref[...] = _dot(h, w_ref[0, :, 2 * HK: 2 * HK + HV]).astype(v_ref.dtype)
    gate_ref[...] = _dot(h, w_ref[0, :, 2 * HK + HV:]).astype(gate_ref.dtype)


def _ret_proj(x2d, gains, layer, cos, sin, w_in, ret_layer, n_seq, out_dtype):
    rows, d = x2d.shape
    seq_rows = rows // n_seq
    tile = min(seq_rows, ROW_TILE)
    tps = seq_rows // tile
    row_spec = lambda w: pl.BlockSpec((tile, w), lambda b, t: (b * tps + t, 0))
    tab_spec = pl.BlockSpec((tile, RET_DK // 2), lambda b, t: (t, 0))
    return pl.pallas_call(
        _retproj_body,
        grid=(n_seq, tps),
        in_specs=[row_spec(d), _layer_resident(gains.shape, layer), tab_spec, tab_spec,
                  _layer_resident(w_in.shape, ret_layer)],
        out_specs=[row_spec(HK), row_spec(HK), row_spec(HV), row_spec(HV)],
        out_shape=[jax.ShapeDtypeStruct((rows, HK), out_dtype), jax.ShapeDtypeStruct((rows, HK), out_dtype),
                   jax.ShapeDtypeStruct((rows, HV), out_dtype), jax.ShapeDtypeStruct((rows, HV), out_dtype)],
        compiler_params=_params(("arbitrary", "arbitrary")),
        name="ret_proj",
    )(x2d, gains, cos, sin, w_in)


def _ret_units(q_ref, k_ref, v_ref, s_in, s_out, o_ref, tables, row_blocks, lookahead):
    inner_ref, qdec_ref, kdec_ref, cdec_ref = tables
    units = [(rows, i, h) for rows, i in row_blocks for h in range(RET_HEADS)]

    def issue(rows, i, h):
        qb = q_ref[rows, h * RET_DK:(h + 1) * RET_DK].astype(BF16)
        kb = k_ref[rows, h * RET_DK:(h + 1) * RET_DK].astype(BF16)
        return _dot_nt(qb, kb), _dot(qb, s_in[i, h].astype(BF16))

    def finish(rows, i, h, qk, qs):
        kh = k_ref[rows, h * RET_DK:(h + 1) * RET_DK]
        vb = v_ref[rows, h * RET_DV:(h + 1) * RET_DV].astype(BF16)
        p = (qk * inner_ref[h]).astype(BF16)
        o_ref[rows, h * RET_DV:(h + 1) * RET_DV] = _dot(p, vb) + qs * qdec_ref[h]
        kd = (kh.astype(F32) * kdec_ref[h]).astype(BF16)
        s_out[i, h] = s_in[i, h] * cdec_ref[h] + _dot_tn(kd, vb)

    if not lookahead:
        for unit in units:
            finish(*unit, *issue(*unit))
        return
    nxt = issue(*units[0])
    for n, unit in enumerate(units):
        cur = nxt
        if n + 1 < len(units):
            nxt = issue(*units[n + 1])
        finish(*unit, *cur)


def _decay_tables(chunk):
    lg = jnp.log(1.0 - 2.0 ** (-5.0 - jnp.arange(RET_HEADS, dtype=F32)))
    idx = jnp.arange(chunk, dtype=F32)
    rel = idx[:, None] - idx[None, :]
    inner = jnp.where(rel[None] >= 0, jnp.exp(jnp.maximum(rel, 0.0)[None] * lg[:, None, None]), 0.0)
    q_dec = jnp.exp((idx + 1.0)[None, :] * lg[:, None])[..., None]
    k_dec = jnp.exp((chunk - 1.0 - idx)[None, :] * lg[:, None])[..., None]
    c_dec = jnp.exp(chunk * lg)[:, None, None]
    return (inner, jnp.broadcast_to(q_dec, (RET_HEADS, chunk, RET_DV)),
            jnp.broadcast_to(k_dec, (RET_HEADS, chunk, RET_DK)),
            jnp.broadcast_to(c_dec, (RET_HEADS, 1, RET_DV)))


def _gn_gate(o, gate, gn):
    parts = []
    for h in range(RET_HEADS):
        oh = o[:, h * RET_DV:(h + 1) * RET_DV]
        mu = jnp.mean(oh, axis=-1, keepdims=True)
        dlt = oh - mu
        var = jnp.mean(dlt * dlt, axis=-1, keepdims=True)
        parts.append(dlt * lax.rsqrt(var + GN_EPS))
    on = jnp.concatenate(parts, axis=-1) * gn
    return (_silu(gate.astype(F32)) * on).astype(BF16)


def _ret_prompt_body(q_ref, k_ref, v_ref, gate_ref, x_ref, gn_ref, w_ref, inner_ref, qdec_ref, kdec_ref, cdec_ref,
                     out_ref, s_ref, o_ref, *, chunk):
    @pl.when(pl.program_id(1) == 0)
    def _():
        s_ref[...] = jnp.zeros_like(s_ref)

    chunks = [slice(c * chunk, (c + 1) * chunk) for c in range(q_ref.shape[0] // chunk)]
    _ret_units(q_ref, k_ref, v_ref, s_ref, s_ref, o_ref, (inner_ref, qdec_ref, kdec_ref, cdec_ref),
               [(rows, 0) for rows in chunks], lookahead=False)
    for rows in chunks:
        z = _gn_gate(o_ref[rows], gate_ref[rows], gn_ref[0])
        out_ref[rows] = x_ref[rows] + _dot(z, w_ref[0])


def _ret_prompt(q, k, v, gate, x2d, gn_g, w_out, ret_layer, n_seq):
    rows, d = x2d.shape
    seq_rows = rows // n_seq
    tile = min(seq_rows, ROW_TILE)
    chunk = min(tile, RET_CHUNK_PROMPT)
    tps = seq_rows // tile
    tabs = _decay_tables(chunk)
    row_spec = lambda w: pl.BlockSpec((tile, w), lambda b, t: (b * tps + t, 0))
    return pl.pallas_call(
        functools.partial(_ret_prompt_body, chunk=chunk),
        grid=(n_seq, tps),
        in_specs=[row_spec(HK), row_spec(HK), row_spec(HV), row_spec(HV), row_spec(d),
                  _layer_resident(gn_g.shape, ret_layer), _layer_resident(w_out.shape, ret_layer)]
                 + [_resident(t.shape) for t in tabs],
        out_specs=[row_spec(d), pl.BlockSpec((1, RET_HEADS, RET_DK, RET_DV), lambda b, t: (b, 0, 0, 0))],
        out_shape=[jax.ShapeDtypeStruct((rows, d), F32),
                   jax.ShapeDtypeStruct((n_seq, RET_HEADS, RET_DK, RET_DV), F32)],
        scratch_shapes=[pltpu.VMEM((tile, HV), F32)],
        compiler_params=_params(("arbitrary", "arbitrary")),
        name="ret_prompt",
    )(q, k, v, gate, x2d, gn_g, w_out, *tabs)


def _ret_sample_body(q_ref, k_ref, v_ref, s0_ref, inner_ref, qdec_ref, kdec_ref, cdec_ref, o_ref, s_ref, *, seq_rows):
    _ret_units(q_ref, k_ref, v_ref, s0_ref, s_ref, o_ref, (inner_ref, qdec_ref, kdec_ref, cdec_ref),
               [(slice(i * seq_rows, (i + 1) * seq_rows), i) for i in range(s0_ref.shape[0])], lookahead=True)


def _ret_sample(q, k, v, states, ret_layer, n_seq, seq_rows):
    bb = 4 if n_seq % 4 == 0 else 1
    steps = n_seq // bb
    tabs = _decay_tables(seq_rows)
    row_spec = lambda w: pl.BlockSpec((bb * seq_rows, w), lambda i: (i, 0))
    st_block = (bb, RET_HEADS, RET_DK, RET_DV)
    return pl.pallas_call(
        functools.partial(_ret_sample_body, seq_rows=seq_rows),
        grid=(steps,),
        in_specs=[row_spec(HK), row_spec(HK), row_spec(HV),
                  pl.BlockSpec(st_block, lambda i: (ret_layer * steps + i, 0, 0, 0))] + [_resident(t.shape) for t in tabs],
        out_specs=[row_spec(HV), pl.BlockSpec(st_block, lambda i: (i, 0, 0, 0))],
        out_shape=[jax.ShapeDtypeStruct((n_seq * seq_rows, HV), F32),
                   jax.ShapeDtypeStruct((n_seq, RET_HEADS, RET_DK, RET_DV), F32)],
        compiler_params=_params(("arbitrary",)),
        name="ret_sample",
    )(q, k, v, states, *tabs)


def _retout_body(o_ref, gate_ref, x_ref, gn_ref, w_ref, out_ref):
    out_ref[...] = x_ref[...] + _dot(_gn_gate(o_ref[...], gate_ref[...], gn_ref[0]), w_ref[0])


def _ret_out(o, gate, x2d, gn_g, w_out, ret_layer):
    rows, d = x2d.shape
    tile = min(rows, ROW_TILE)
    row_spec = lambda w: pl.BlockSpec((tile, w), lambda r: (r, 0))
    return pl.pallas_call(
        _retout_body,
        grid=(rows // tile,),
        in_specs=[row_spec(HV), row_spec(HV), row_spec(d), _layer_resident(gn_g.shape, ret_layer),
                  _layer_resident(w_out.shape, ret_layer)],
        out_specs=row_spec(d),
        out_shape=jax.ShapeDtypeStruct((rows, d), F32),
        compiler_params=_params(("arbitrary",)),
        name="ret_out",
    )(o, gate, x2d, gn_g, w_out)


def _softmax(s):
    e = jnp.exp(s - jnp.max(s, axis=-1, keepdims=True))
    return e / jnp.sum(e, axis=-1, keepdims=True)


def _xattn_block(x_ref, g_ref, wq_ref, wo_ref, keys, values, out_ref, att_ref, *s_refs, n_seq, seq_rows):
    head_cols = [slice(hd * X_HEAD_DIM, (hd + 1) * X_HEAD_DIM) for hd in range(X_HEADS)]
    h = _rms(x_ref[...], g_ref[0]).astype(BF16)
    q = _dot(h, wq_ref[...]) * (X_HEAD_DIM ** -0.5)
    if s_refs:
        s_ref, = s_refs
        pairs = [(slice(i * seq_rows, (i + 1) * seq_rows), cols, i, hd)
                 for i in range(n_seq) for hd, cols in enumerate(head_cols)]
        for n, (rows, cols, i, hd) in enumerate(pairs):
            s_ref[n * seq_rows:(n + 1) * seq_rows] = _dot_nt(q[rows, cols].astype(BF16), keys(i, hd))
        s_ref[...] = _softmax(s_ref[...])
        for n, (rows, cols, i, hd) in enumerate(pairs):
            att_ref[rows, cols] = _dot(s_ref[n * seq_rows:(n + 1) * seq_rows].astype(BF16), values(i, hd))
    else:
        scores = lambda hd: _dot_nt(q[:, head_cols[hd]].astype(BF16), keys(0, hd))
        nxt = scores(0)
        for hd, cols in enumerate(head_cols):
            s = nxt
            if hd + 1 < X_HEADS:
                nxt = scores(hd + 1)
            att_ref[:, cols] = _dot(_softmax(s).astype(BF16), values(0, hd))
    out_ref[...] = x_ref[...] + _dot(att_ref[...].astype(BF16), wo_ref[...])


def _xattn_body(xl_ref, xs_ref, g_ref, wq_ref, wo_ref, kl_ref, vl_ref, ks_ref, vs_ref, outl_ref, outs_ref,
                attl_ref, atts_ref, score_ref, wqb_ref, wob_ref, kvl_ref, *,
                long_steps, short_steps, tiles_per_seq, short_seqs, short_rows):
    step = pl.program_id(0)
    idx = step // 2
    n_mem = kl_ref.shape[1] // KV_ROW_STRIDE

    @pl.when(step == 0)
    def _():
        wqb_ref[...] = wq_ref[0].astype(BF16)
        wob_ref[...] = wo_ref[0].astype(BF16)

    @pl.when((step % 2 == 0) & (idx < long_steps))
    def _():
        @pl.when(idx % tiles_per_seq == 0)
        def _():
            for hd in range(X_HEADS):
                kvl_ref[hd] = _head_kv(kl_ref, 0, hd, n_mem)
                kvl_ref[X_HEADS + hd] = _head_kv(vl_ref, 0, hd, n_mem)

        _xattn_block(xl_ref, g_ref, wqb_ref, wob_ref, lambda i, hd: kvl_ref[hd], lambda i, hd: kvl_ref[X_HEADS + hd],
                     outl_ref, attl_ref, n_seq=1, seq_rows=xl_ref.shape[0])

    @pl.when((step % 2 == 1) & (idx < short_steps))
    def _():
        _xattn_block(xs_ref, g_ref, wqb_ref, wob_ref, lambda i, hd: _head_kv(ks_ref, i, hd, n_mem),
                     lambda i, hd: _head_kv(vs_ref, i, hd, n_mem), outs_ref, atts_ref, score_ref,
                     n_seq=short_seqs, seq_rows=short_rows)


def _xattn(x_long, x_short, gains, wq, wo, layer, kv_long, kv_short, n_long, n_short):
    d = x_long.shape[1]
    long_rows, short_rows = x_long.shape[0] // n_long, x_short.shape[0] // n_short
    assert long_rows % ROW_TILE == 0
    n_mem = kv_long[0].shape[1]
    tps = long_rows // ROW_TILE
    long_steps = n_long * tps
    bb = 4 if n_short % 4 == 0 else 1
    short_steps = n_short // bb
    li = lambda s: jnp.minimum(s // 2, long_steps - 1)
    si = lambda s: jnp.minimum(jnp.maximum(s - 1, 0) // 2, short_steps - 1)
    long_spec = pl.BlockSpec((ROW_TILE, d), lambda s: (li(s), 0))
    short_spec = pl.BlockSpec((bb * short_rows, d), lambda s: (si(s), 0))
    kvl_spec = pl.BlockSpec((1, n_mem, LANES), lambda s: (layer * n_long + li(s) // tps, 0, 0))
    kvs_spec = pl.BlockSpec((bb, n_mem, LANES), lambda s: (layer * short_steps + si(s), 0, 0))
    mem_tokens = n_mem // KV_ROW_STRIDE
    return pl.pallas_call(
        functools.partial(_xattn_body, long_steps=long_steps, short_steps=short_steps, tiles_per_seq=tps,
                          short_seqs=bb, short_rows=short_rows),
        grid=(2 * max(long_steps, short_steps),),
        in_specs=[long_spec, short_spec, _layer_resident(gains.shape, layer), _layer_resident(wq.shape, layer),
                  _layer_resident(wo.shape, layer), kvl_spec, kvl_spec, kvs_spec, kvs_spec],
        out_specs=[long_spec, short_spec],
        out_shape=[jax.ShapeDtypeStruct(x_long.shape, F32), jax.ShapeDtypeStruct(x_short.shape, F32)],
        scratch_shapes=[pltpu.VMEM((ROW_TILE, d), F32), pltpu.VMEM((bb * short_rows, d), F32),
                        pltpu.VMEM((bb * X_HEADS * short_rows, mem_tokens), F32),
                        pltpu.VMEM((d, d), BF16), pltpu.VMEM((d, d), BF16),
                        pltpu.VMEM((2 * X_HEADS, mem_tokens, X_HEAD_DIM), BF16)],
        compiler_params=_params(("arbitrary",)),
        name="xattn",
    )(x_long, x_short, gains, wq, wo, *kv_long, *kv_short)


def _conv_chunk(u, prev, cwb, shift):
    c = cwb[3:4] + cwb[0:1] * shift(u, prev, 2)
    c = c + cwb[1:2] * shift(u, prev, 1)
    return c + cwb[2:3] * u


def _ffn_gate(ua, ug, prev_a, prev_g, cwb_a, cwb_g, shift):
    return (_conv_chunk(ua, prev_a, cwb_a, shift) * _silu(_conv_chunk(ug, prev_g, cwb_g, shift))).astype(BF16)


def _ffn_chunk(h, wa, wg, wd, prev_a, prev_g, cwb_a, cwb_g, shift):
    ua = _dot(h, wa)
    ug = _dot(h, wg)
    return _dot(_ffn_gate(ua, ug, prev_a, prev_g, cwb_a, cwb_g, shift), wd), ua, ug


def _ffn_prompt_body(x_ref, g_ref, gf_ref, wu_ref, wd_ref, cwb_ref, out_ref, c_ref,
                     h_ref, acc_ref, ua0_ref, ug0_ref, ua1_ref, ug1_ref, z0_ref, z1_ref, *, final_norm):
    halo = (CONV_W - 1) * SUBLANES
    tile = acc_ref.shape[0]
    groups = tile // SUBLANES

    @pl.when(pl.program_id(1) == 0)
    def _():
        c_ref[...] = jnp.zeros_like(c_ref)

    x = jnp.swapaxes(x_ref[...].reshape(SUBLANES, groups, D_MODEL), 0, 1).reshape(tile, D_MODEL)
    h_ref[...] = _rms(x, g_ref[0]).astype(BF16)
    acc_ref[...] = x
    slots = ((ua0_ref, ug0_ref), (ua1_ref, ug1_ref))
    starts = range(0, D_FF, FFN_PROMPT_CHUNK)
    n_chunks = len(starts)

    def cols(f):
        lo, hi = starts[f], min(starts[f] + FFN_PROMPT_CHUNK, D_FF)
        return slice(lo, hi), slice(D_FF + lo, D_FF + hi), hi - lo

    def up(f):
        ca, cg, width = cols(f)
        first_sublane = lax.broadcasted_iota(jnp.int32, (SUBLANES, width), 0) == 0
        for u_ref, cc in zip(slots[f % 2], (ca, cg)):
            u = _dot(h_ref[...], wu_ref[0, :, cc])
            u_ref[halo:, :width] = u
            for m in range(CONV_W - 1):
                rows = slice(m * SUBLANES, (m + 1) * SUBLANES)
                src = u[tile - halo + m * SUBLANES: tile - halo + (m + 1) * SUBLANES]
                u_ref[rows, :width] = jnp.where(first_sublane, pltpu.roll(c_ref[0, rows, cc], 1, axis=0),
                                                pltpu.roll(src, 1, axis=0))
                c_ref[0, rows, cc] = src

    def conv(u_ref, cw, r0, n, width):
        c = cw[3:4] + cw[0:1] * u_ref[r0: r0 + n, :width]
        c = c + cw[1:2] * u_ref[r0 + SUBLANES: r0 + SUBLANES + n, :width]
        return c + cw[2:3] * u_ref[r0 + halo: r0 + halo + n, :width]

    def gate(f):
        ca, cg, width = cols(f)
        ua_ref, ug_ref = slots[f % 2]
        cwa, cwg = cwb_ref[0, :, ca], cwb_ref[0, :, cg]
        z_ref = (z0_ref, z1_ref)[f % 2]
        for r0 in range(0, tile, FFN_GATE_ROWS):
            z_ref[r0: r0 + FFN_GATE_ROWS, :width] = (
                conv(ua_ref, cwa, r0, FFN_GATE_ROWS, width)
                * _silu(conv(ug_ref, cwg, r0, FFN_GATE_ROWS, width))).astype(BF16)
        return z_ref[:, :width]

    up(0)
    for f in range(n_chunks):
        if f + 1 < n_chunks:
            up(f + 1)
        acc_ref[...] += _dot(gate(f), wd_ref[0, cols(f)[0], :])
    y = acc_ref[...]
    if final_norm:
        y = _rms(y, gf_ref[...])
    out_ref[...] = jnp.swapaxes(y.reshape(groups, SUBLANES, D_MODEL), 0, 1).reshape(tile, D_MODEL)


def _ffn_prompt(x2d, gains, g_final, w_up, w_down, cwb, layer, n_seq, final_norm):
    rows, d = x2d.shape
    seq_rows = rows // n_seq
    tile = min(seq_rows, FFN_ROW_TILE)
    tps = seq_rows // tile
    halo = (CONV_W - 1) * SUBLANES
    row_spec = pl.BlockSpec((tile, d), lambda b, t: (b * tps + t, 0))
    u_scratch = pltpu.VMEM((halo + tile, FFN_PROMPT_CHUNK), F32)
    z_scratch = pltpu.VMEM((tile, FFN_PROMPT_CHUNK), BF16)
    y, carry = pl.pallas_call(
        functools.partial(_ffn_prompt_body, final_norm=final_norm),
        grid=(n_seq, tps),
        in_specs=[row_spec, _layer_resident(gains.shape, layer), _resident((1, d)),
                  _layer_resident(w_up.shape, layer), _layer_resident(w_down.shape, layer),
                  _layer_resident(cwb.shape, layer)],
        out_specs=[row_spec, pl.BlockSpec((1, halo, 2 * D_FF), lambda b, t: (b, 0, 0))],
        out_shape=[jax.ShapeDtypeStruct((rows, d), F32), jax.ShapeDtypeStruct((n_seq, halo, 2 * D_FF), F32)],
        scratch_shapes=[pltpu.VMEM((tile, d), BF16), pltpu.VMEM((tile, d), F32),
                        u_scratch, u_scratch, u_scratch, u_scratch, z_scratch, z_scratch],
        compiler_params=_params(("arbitrary", "arbitrary")),
        name="ffn_prompt",
    )(x2d, gains, g_final.reshape(1, d), w_up, w_down, cwb)
    return y, carry[:, SUBLANES - 1::SUBLANES]


def _ffn_sample_body(x_ref, g_ref, gf_ref, wa_ref, wg_ref, wd_ref, cwa_ref, cwg_ref, ba_ref, bg_ref,
                     out_ref, ca_ref, cg_ref, h_ref, acc_ref, *, n_seq, seq_rows, final_norm):
    f = pl.program_id(0)

    @pl.when(f == 0)
    def _():
        xt = _to_time_major(x_ref[...], n_seq, seq_rows)
        h_ref[...] = _rms(xt, g_ref[0]).astype(BF16)
        acc_ref[...] = xt

    rows = n_seq * seq_rows
    hist = (CONV_W - 1) * n_seq

    def shift(u, prev, j):
        return jnp.concatenate([prev, u], axis=0)[hist - j * n_seq: hist - j * n_seq + rows]

    y, ua, ug = _ffn_chunk(h_ref[...], wa_ref[0], wg_ref[0], wd_ref[0], ba_ref[...], bg_ref[...],
                           cwa_ref[0], cwg_ref[0], shift)
    acc_ref[...] += y
    ca_ref[...] = ua[rows - hist:]
    cg_ref[...] = ug[rows - hist:]

    @pl.when(f == pl.num_programs(0) - 1)
    def _():
        yt = acc_ref[...]
        if final_norm:
            yt = _rms(yt, gf_ref[...])
        out_ref[...] = _from_time_major(yt, n_seq, seq_rows)


def _ffn_sample(x2d, gains, g_final, w_up, w_down, cwb, layer, buf, n_seq, final_norm):
    rows, d = x2d.shape
    seq_rows = rows // n_seq
    hist = (CONV_W - 1) * n_seq
    a_col = lambda f: f
    g_col = lambda f: NF + f
    c_shape = jax.ShapeDtypeStruct((hist, D_FF), F32)
    c_spec = pl.BlockSpec((hist, FF_CHUNK), lambda f: (0, f))
    return pl.pallas_call(
        functools.partial(_ffn_sample_body, n_seq=n_seq, seq_rows=seq_rows, final_norm=final_norm),
        grid=(NF,),
        in_specs=[_resident((rows, d)), _layer_resident(gains.shape, layer), _resident((1, d)),
                  pl.BlockSpec((1, d, FF_CHUNK), lambda f: (layer, 0, a_col(f))),
                  pl.BlockSpec((1, d, FF_CHUNK), lambda f: (layer, 0, g_col(f))),
                  pl.BlockSpec((1, FF_CHUNK, d), lambda f: (layer, f, 0)),
                  pl.BlockSpec((1, 4, FF_CHUNK), lambda f: (layer, 0, a_col(f))),
                  pl.BlockSpec((1, 4, FF_CHUNK), lambda f: (layer, 0, g_col(f))),
                  pl.BlockSpec((hist, FF_CHUNK), lambda f: (0, a_col(f))),
                  pl.BlockSpec((hist, FF_CHUNK), lambda f: (0, g_col(f)))],
        out_specs=[pl.BlockSpec((rows, d), lambda f: (0, 0)), c_spec, c_spec],
        out_shape=[jax.ShapeDtypeStruct((rows, d), F32), c_shape, c_shape],
        scratch_shapes=[pltpu.VMEM((rows, d), BF16), pltpu.VMEM((rows, d), F32)],
        compiler_params=_params(("arbitrary",)),
        name="ffn_sample",
    )(x2d, gains, g_final.reshape(1, d), w_up, w_up, w_down, cwb, cwb, buf, buf)


def _pool_mix(pooled, w_ref, scale_ref, x):
    mixed = [_dot(pooled[g].astype(BF16), w_ref[0, g]) for g in range(len(POOL_WINDOWS))]
    return x + jnp.concatenate(mixed, axis=-1) * scale_ref[0]


def _pool_prompt_body(x_ref, g_ref, w_ref, scale_ref, out_ref, hist_ref):
    t = pl.program_id(1)

    @pl.when(t == 0)
    def _():
        hist_ref[...] = jnp.zeros_like(hist_ref)

    x = x_ref[...]
    tile = x.shape[0]
    h = _rms(x, g_ref[0])
    ext = jnp.concatenate([hist_ref[0], h], axis=0)
    hist_ref[0] = h[tile - POOL_CARRY:]
    pos1 = (t * tile + 1 + lax.broadcasted_iota(jnp.int32, (tile, POOL_GC), 0)).astype(F32)
    pooled = []
    s = ext
    for g, w in enumerate(POOL_WINDOWS):
        s = s[:, POOL_GC * (1 if g else 0):]
        s = s + pltpu.roll(s, w // 2, axis=0)
        cnt = jnp.minimum(pos1, float(w))
        pooled.append(s[POOL_CARRY:, :POOL_GC] / cnt - h[:, g * POOL_GC:(g + 1) * POOL_GC])
    out_ref[...] = _pool_mix(pooled, w_ref, scale_ref, x)


def _pool_prompt(x2d, gains, layer, pool_w, pool_scale, pool_layer, n_seq):
    rows, d = x2d.shape
    seq_rows = rows // n_seq
    tile = min(seq_rows, ROW_TILE)
    tps = seq_rows // tile
    row_spec = pl.BlockSpec((tile, d), lambda b, t: (b * tps + t, 0))
    return pl.pallas_call(
        _pool_prompt_body,
        grid=(n_seq, tps),
        in_specs=[row_spec, _layer_resident(gains.shape, layer), _layer_resident(pool_w.shape, pool_layer),
                  _layer_resident(pool_scale.shape, pool_layer)],
        out_specs=[row_spec, pl.BlockSpec((1, POOL_CARRY, d), lambda b, t: (b, 0, 0))],
        out_shape=[jax.ShapeDtypeStruct((rows, d), F32), jax.ShapeDtypeStruct((n_seq, POOL_CARRY, d), F32)],
        compiler_params=_params(("arbitrary", "arbitrary")),
        name="pool_prompt",
    )(x2d, gains, pool_w, pool_scale)


def _pool_sample_body(x_ref, g_ref, w_ref, scale_ref, buf_ref, out_ref, h_ref, *, n_seq, seq_rows, pos0):
    xt = _to_time_major(x_ref[...], n_seq, seq_rows)
    h = _rms(xt, g_ref[0])
    s = jnp.concatenate([buf_ref[...], h], axis=0)
    rows = n_seq * seq_rows
    first = POOL_BUF
    pooled = []
    for g, w in enumerate(POOL_WINDOWS):
        s = s[:, POOL_GC * (1 if g else 0):]
        step = (w // 2) * n_seq
        s = s[step:] + s[:-step]
        first -= w // 2
        win = s[first * n_seq: first * n_seq + rows, :POOL_GC]
        inv = [1.0 / min(pos0 + t + 1, w) for t in range(seq_rows)]
        if len(set(inv)) == 1:
            win = win * inv[0]
        else:
            win = jnp.concatenate([win[t * n_seq:(t + 1) * n_seq] * inv[t] for t in range(seq_rows)], axis=0)
        pooled.append(win - h[:, g * POOL_GC:(g + 1) * POOL_GC])
    out_ref[...] = _from_time_major(_pool_mix(pooled, w_ref, scale_ref, xt), n_seq, seq_rows)
    h_ref[...] = _from_time_major(h, n_seq, seq_rows)


def _pool_sample(x2d, gains, layer, pool_w, pool_scale, pool_layer, buf_t, n_seq, pos0):
    rows, d = x2d.shape
    seq_rows = rows // n_seq
    full = lambda s: pl.BlockSpec(s, lambda i: (0,) * len(s))
    return pl.pallas_call(
        functools.partial(_pool_sample_body, n_seq=n_seq, seq_rows=seq_rows, pos0=pos0),
        grid=(1,),
        in_specs=[full((rows, d)), _layer_resident(gains.shape, layer), _layer_resident(pool_w.shape, pool_layer),
                  _layer_resident(pool_scale.shape, pool_layer), full(buf_t.shape)],
        out_specs=[full((rows, d)), full((rows, d))],
        out_shape=[jax.ShapeDtypeStruct((rows, d), F32)] * 2,
        compiler_params=_params(("arbitrary",)),
        name="pool_sample",
    )(x2d, gains, pool_w, pool_scale, buf_t)


def _rope_tables(pos):
    inv = 1.0 / (ROPE_BASE ** (jnp.arange(0, RET_DK, 2, dtype=F32) / RET_DK))
    ang = pos[:, None] * inv[None, :]
    return jnp.cos(ang), jnp.sin(ang)


def kernel(x_prompt, x_sample, mem_prompt, cache_mem_k, cache_mem_v, state_ret, cache_pool, cache_ffn_conv, w_ret_in, ret_gn, w_ret_out, pool_w, pool_scale, norm_mem, w_xq, w_xk, w_xv, w_xo, w_up, conv_w, conv_b, w_down, norm_mix, norm_xattn, norm_ffn, norm_final):
    bp, lp, d = x_prompt.shape
    bs, ls, _ = x_sample.shape
    depth = w_up.shape[0]
    n_mem = mem_prompt.shape[1]

    w_in_b, w_out_b, pool_w_b, wk_b, wv_b, w_up_b, w_down_b = (
        w.astype(BF16) for w in (w_ret_in, w_ret_out, pool_w, w_xk, w_xv, w_up, w_down))
    cwb = jnp.concatenate([conv_w, conv_b[:, None, :]], axis=1)
    g_mix, g_x, g_ffn = _gain(norm_mix), _gain(norm_xattn), _gain(norm_ffn)
    gn_g, p_scale = _gain(ret_gn), _gain(pool_scale)

    mem_k2, mem_v2 = _mem_kv(mem_prompt.reshape(bp * n_mem, d), norm_mem, wk_b, wv_b)
    kv_p = tuple(a.reshape(depth * bp, n_mem * KV_ROW_STRIDE, LANES) for a in (mem_k2, mem_v2))
    kv_s = (_kv_rows(cache_mem_k), _kv_rows(cache_mem_v))
    states = state_ret.reshape((state_ret.shape[0] * bs,) + state_ret.shape[2:])

    cos_p, sin_p = _rope_tables(jnp.arange(lp, dtype=F32))
    cos_s, sin_s = _rope_tables(PAST_LEN + jnp.arange(ls, dtype=F32))
    cos_s, sin_s = jnp.tile(cos_s, (bs, 1)), jnp.tile(sin_s, (bs, 1))

    xp = x_prompt.reshape(bp * lp, d)
    xs = x_sample.reshape(bs * ls, d)
    ret_p, ret_s, pool_p, pool_s, conv_p, conv_s = [], [], [], [], [], []
    for i in range(depth):
        j = i // 2
        last = i == depth - 1
        if i % 2 == 0:
            q, k, v, gate = _ret_proj(xp, g_mix, i, cos_p, sin_p, w_in_b, j, bp, BF16)
            xp, s = _ret_prompt(q, k, v, gate, xp, gn_g, w_out_b, j, bp)
            ret_p.append(s)
            q, k, v, gate = _ret_proj(xs, g_mix, i, cos_s, sin_s, w_in_b, j, 1, F32)
            o, s = _ret_sample(q, k, v, states, j, bs, ls)
            xs = _ret_out(o, gate, xs, gn_g, w_out_b, j)
            ret_s.append(s)
        else:
            xp, hist = _pool_prompt(xp, g_mix, i, pool_w_b, p_scale, j, bp)
            pool_p.append(hist[:, POOL_CARRY - POOL_BUF:])
            buf_t = jnp.swapaxes(cache_pool[j], 0, 1).reshape(POOL_BUF * bs, d)
            xs, hs = _pool_sample(xs, g_mix, i, pool_w_b, p_scale, j, buf_t, bs, PAST_LEN)
            pool_s.append(jnp.concatenate([cache_pool[j], hs.reshape(bs, ls, d)], axis=1)[:, -POOL_BUF:])

        xp, xs = _xattn(xp, xs, g_x, w_xq, w_xo, i, kv_p, kv_s, bp, bs)

        xp, tail = _ffn_prompt(xp, g_ffn, norm_final, w_up_b, w_down_b, cwb, i, bp, last)
        conv_p.append(tail)
        buf = jnp.swapaxes(cache_ffn_conv[i], 0, 1).reshape((CONV_W - 1) * bs, 2 * D_FF)
        xs, ca, cg = _ffn_sample(xs, g_ffn, norm_final, w_up_b, w_down_b, cwb, i, buf, bs, last)
        tail = jnp.concatenate([ca, cg], axis=-1)
        conv_s.append(jnp.swapaxes(tail.reshape(CONV_W - 1, bs, 2 * D_FF), 0, 1))

    mk = _kv_unrows(mem_k2, (depth, bp), n_mem)
    mv = _kv_unrows(mem_v2, (depth, bp), n_mem)
    return (xp.reshape(bp, lp, d), xs.reshape(bs, ls, d), jnp.stack(ret_p), jnp.stack(ret_s).astype(state_ret.dtype),
            jnp.stack(pool_p), jnp.stack(pool_s), jnp.stack(conv_p), jnp.stack(conv_s), mk, mv)
```

```python
import functools

import jax
import jax.numpy as jnp
from jax import lax
from jax.experimental import pallas as pl
from jax.experimental.pallas import tpu as pltpu

F32 = jnp.float32
BF16 = jnp.bfloat16

D_MODEL = 1024
PAST_LEN = 16384
RET_HEADS = 4
RET_DK = D_MODEL // RET_HEADS
RET_DV = 2 * D_MODEL // RET_HEADS
HK = RET_HEADS * RET_DK
HV = RET_HEADS * RET_DV
ROPE_BASE = 10000.0
POOL_WINDOWS = (2, 4, 8, 16)
POOL_GC = D_MODEL // len(POOL_WINDOWS)
POOL_BUF = max(POOL_WINDOWS) - 1
X_HEADS = 4
X_HEAD_DIM = D_MODEL // X_HEADS
D_FF = 2816
CONV_W = 3
NORM_EPS = 1e-6
GN_EPS = 1e-5

SUBLANES = 8
LANES = 128
KV_LANE_TILES = X_HEAD_DIM // LANES
KV_ROW_STRIDE = X_HEADS * KV_LANE_TILES
assert KV_ROW_STRIDE == SUBLANES
ROW_TILE = 512
FFN_ROW_TILE = 512
RET_CHUNK_PROMPT = 256
FF_CHUNK = 1408
NF = D_FF // FF_CHUNK
FFN_PROMPT_CHUNK = 1536
FFN_GATE_ROWS = 64
POOL_CARRY = 16
VMEM_LIMIT = 56 * 1024 * 1024


def _params(sem):
    return pltpu.CompilerParams(dimension_semantics=sem, vmem_limit_bytes=VMEM_LIMIT)


def _resident(shape):
    zeros = (0,) * len(shape)
    return pl.BlockSpec(shape, lambda *_: zeros, pipeline_mode=pl.Buffered(1))


def _layer_resident(stacked_shape, layer):
    idx = (layer,) + (0,) * (len(stacked_shape) - 1)
    return pl.BlockSpec((1,) + tuple(stacked_shape[1:]), lambda *_: idx, pipeline_mode=pl.Buffered(1))


def _gain(g):
    return g.reshape(g.shape[0], 1, g.shape[1])


def _dot(a, b):
    return jnp.dot(a, b, preferred_element_type=F32)


def _dot_nt(a, b):
    return lax.dot_general(a, b, (((1,), (1,)), ((), ())), preferred_element_type=F32)


def _dot_tn(a, b):
    return lax.dot_general(a, b, (((0,), (0,)), ((), ())), preferred_element_type=F32)


def _rms(x, g):
    return x * lax.rsqrt(jnp.mean(x * x, axis=-1, keepdims=True) + NORM_EPS) * g


def _silu(x):
    return x * (1.0 / (1.0 + jnp.exp(-x)))


def _to_time_major(x, n_seq, seq_rows):
    return jnp.swapaxes(x.reshape(n_seq, seq_rows, x.shape[-1]), 0, 1).reshape(x.shape)


def _from_time_major(x, n_seq, seq_rows):
    return jnp.swapaxes(x.reshape(seq_rows, n_seq, x.shape[-1]), 0, 1).reshape(x.shape)


def _memkv_body(mem_ref, g_ref, wk_ref, wv_ref, k_ref, v_ref):
    h = _rms(mem_ref[...], g_ref[0]).astype(BF16)
    for w_ref, o_ref in ((wk_ref, k_ref), (wv_ref, v_ref)):
        y = _dot(h, w_ref[0])
        for hd in range(X_HEADS):
            for c in range(KV_LANE_TILES):
                col = hd * X_HEAD_DIM + c * LANES
                o_ref[0, pl.ds(c * X_HEADS + hd, y.shape[0], stride=KV_ROW_STRIDE), :] = y[:, col:col + LANES]


def _mem_kv(mem2d, norm_mem, wk, wv):
    rows, d = mem2d.shape
    depth = wk.shape[0]
    tile = min(rows, 1024)
    w_spec = pl.BlockSpec((1, d, d), lambda i, r: (i, 0, 0))
    o_spec = pl.BlockSpec((1, tile * KV_ROW_STRIDE, LANES), lambda i, r: (i, r, 0))
    return pl.pallas_call(
        _memkv_body,
        grid=(depth, rows // tile),
        in_specs=[pl.BlockSpec((tile, d), lambda i, r: (r, 0)),
                  pl.BlockSpec((1, 1, d), lambda i, r: (i, 0, 0)), w_spec, w_spec],
        out_specs=[o_spec, o_spec],
        out_shape=[jax.ShapeDtypeStruct((depth, rows * KV_ROW_STRIDE, LANES), F32)] * 2,
        compiler_params=_params(("arbitrary", "arbitrary")),
        name="mem_kv",
    )(mem2d, _gain(norm_mem), wk, wv)


def _kv_rows(a):
    lead, n_mem = a.shape[:-3], a.shape[-3]
    a = a.reshape(lead + (n_mem, X_HEADS, KV_LANE_TILES, LANES))
    a = jnp.swapaxes(a, -3, -2)
    return a.reshape((-1, n_mem * KV_ROW_STRIDE, LANES))


def _kv_unrows(a, lead, n_mem):
    a = a.reshape(lead + (n_mem, KV_LANE_TILES, X_HEADS, LANES))
    a = jnp.swapaxes(a, -3, -2)
    return a.reshape(lead + (n_mem, X_HEADS, X_HEAD_DIM))


def _head_kv(ref, i, hd, n_mem):
    parts = [ref[i, pl.ds(c * X_HEADS + hd, n_mem, stride=KV_ROW_STRIDE), :] for c in range(KV_LANE_TILES)]
    return jnp.concatenate(parts, axis=1).astype(BF16)


def _retproj_body(x_ref, g_ref, cos_ref, sin_ref, w_ref, q_ref, k_ref, v_ref, gate_ref):
    h = _rms(x_ref[...], g_ref[0]).astype(BF16)
    cos = cos_ref[...]
    sin = sin_ref[...]
    half = RET_DK // 2

    def rope_store(col0, out_ref, scale):
        p = _dot(h, w_ref[0, :, col0:col0 + HK])
        for hd in range(RET_HEADS):
            a = p[:, hd * RET_DK: hd * RET_DK + half]
            b = p[:, hd * RET_DK + half: (hd + 1) * RET_DK]
            out_ref[:, hd * RET_DK: hd * RET_DK + half] = ((a * cos - b * sin) * scale).astype(out_ref.dtype)
            out_ref[:, hd * RET_DK + half: (hd + 1) * RET_DK] = ((b * cos + a * sin) * scale).astype(out_ref.dtype)

    rope_store(0, q_ref, 1.0)
    rope_store(HK, k_ref, RET_DK ** -0.5)
    v_ref[...] = _dot(h, w_ref[0, :, 2 * HK: 2 * HK + HV]).astype(v_ref.dtype)
    gate_ref[...] = _dot(h, w_ref[0, :, 2 * HK + HV:]).astype(gate_ref.dtype)


def _ret_proj(x2d, gains, layer, cos, sin, w_in, ret_layer, n_seq, out_dtype):
    rows, d = x2d.shape
    seq_rows = rows // n_seq
    tile = min(seq_rows, ROW_TILE)
    tps = seq_rows // tile
    row_spec = lambda w: pl.BlockSpec((tile, w), lambda b, t: (b * tps + t, 0))
    tab_spec = pl.BlockSpec((tile, RET_DK // 2), lambda b, t: (t, 0))
    return pl.pallas_call(
        _retproj_body,
        grid=(n_seq, tps),
        in_specs=[row_spec(d), _layer_resident(gains.shape, layer), tab_spec, tab_spec,
                  _layer_resident(w_in.shape, ret_layer)],
        out_specs=[row_spec(HK), row_spec(HK), row_spec(HV), row_spec(HV)],
        out_shape=[jax.ShapeDtypeStruct((rows, HK), out_dtype), jax.ShapeDtypeStruct((rows, HK), out_dtype),
                   jax.ShapeDtypeStruct((rows, HV), out_dtype), jax.ShapeDtypeStruct((rows, HV), out_dtype)],
        compiler_params=_params(("arbitrary", "arbitrary")),
        name="ret_proj",
    )(x2d, gains, cos, sin, w_in)


def _ret_units(q_ref, k_ref, v_ref, s_in, s_out, o_ref, tables, row_blocks, lookahead):
    inner_ref, qdec_ref, kdec_ref, cdec_ref = tables
    units = [(rows, i, h) for rows, i in row_blocks for h in range(RET_HEADS)]

    def issue(rows, i, h):
        qb = q_ref[rows, h * RET_DK:(h + 1) * RET_DK].astype(BF16)
        kb = k_ref[rows, h * RET_DK:(h + 1) * RET_DK].astype(BF16)
        return _dot_nt(qb, kb), _dot(qb, s_in[i, h].astype(BF16))

    def finish(rows, i, h, qk, qs):
        kh = k_ref[rows, h * RET_DK:(h + 1) * RET_DK]
        vb = v_ref[rows, h * RET_DV:(h + 1) * RET_DV].astype(BF16)
        p = (qk * inner_ref[h]).astype(BF16)
        o_ref[rows, h * RET_DV:(h + 1) * RET_DV] = _dot(p, vb) + qs * qdec_ref[h]
        kd = (kh.astype(F32) * kdec_ref[h]).astype(BF16)
        s_out[i, h] = s_in[i, h] * cdec_ref[h] + _dot_tn(kd, vb)

    if not lookahead:
        for unit in units:
            finish(*unit, *issue(*unit))
        return
    nxt = issue(*units[0])
    for n, unit in enumerate(units):
        cur = nxt
        if n + 1 < len(units):
            nxt = issue(*units[n + 1])
        finish(*unit, *cur)


def _decay_tables(chunk):
    lg = jnp.log(1.0 - 2.0 ** (-5.0 - jnp.arange(RET_HEADS, dtype=F32)))
    idx = jnp.arange(chunk, dtype=F32)
    rel = idx[:, None] - idx[None, :]
    inner = jnp.where(rel[None] >= 0, jnp.exp(jnp.maximum(rel, 0.0)[None] * lg[:, None, None]), 0.0)
    q_dec = jnp.exp((idx + 1.0)[None, :] * lg[:, None])[..., None]
    k_dec = jnp.exp((chunk - 1.0 - idx)[None, :] * lg[:, None])[..., None]
    c_dec = jnp.exp(chunk * lg)[:, None, None]
    return (inner, jnp.broadcast_to(q_dec, (RET_HEADS, chunk, RET_DV)),
            jnp.broadcast_to(k_dec, (RET_HEADS, chunk, RET_DK)),
            jnp.broadcast_to(c_dec, (RET_HEADS, 1, RET_DV)))


def _gn_gate(o, gate, gn):
    parts = []
    for h in range(RET_HEADS):
        oh = o[:, h * RET_DV:(h + 1) * RET_DV]
        mu = jnp.mean(oh, axis=-1, keepdims=True)
        dlt = oh - mu
        var = jnp.mean(dlt * dlt, axis=-1, keepdims=True)
        parts.append(dlt * lax.rsqrt(var + GN_EPS))
    on = jnp.concatenate(parts, axis=-1) * gn
    return (_silu(gate.astype(F32)) * on).astype(BF16)


def _ret_prompt_body(q_ref, k_ref, v_ref, gate_ref, x_ref, gn_ref, w_ref, inner_ref, qdec_ref, kdec_ref, cdec_ref,
                     out_ref, s_ref, o_ref, *, chunk):
    @pl.when(pl.program_id(1) == 0)
    def _():
        s_ref[...] = jnp.zeros_like(s_ref)

    chunks = [slice(c * chunk, (c + 1) * chunk) for c in range(q_ref.shape[0] // chunk)]
    _ret_units(q_ref, k_ref, v_ref, s_ref, s_ref, o_ref, (inner_ref, qdec_ref, kdec_ref, cdec_ref),
               [(rows, 0) for rows in chunks], lookahead=False)
    for rows in chunks:
        z = _gn_gate(o_ref[rows], gate_ref[rows], gn_ref[0])
        out_ref[rows] = x_ref[rows] + _dot(z, w_ref[0])


def _ret_prompt(q, k, v, gate, x2d, gn_g, w_out, ret_layer, n_seq):
    rows, d = x2d.shape
    seq_rows = rows // n_seq
    tile = min(seq_rows, ROW_TILE)
    chunk = min(tile, RET_CHUNK_PROMPT)
    tps = seq_rows // tile
    tabs = _decay_tables(chunk)
    row_spec = lambda w: pl.BlockSpec((tile, w), lambda b, t: (b * tps + t, 0))
    return pl.pallas_call(
        functools.partial(_ret_prompt_body, chunk=chunk),
        grid=(n_seq, tps),
        in_specs=[row_spec(HK), row_spec(HK), row_spec(HV), row_spec(HV), row_spec(d),
                  _layer_resident(gn_g.shape, ret_layer), _layer_resident(w_out.shape, ret_layer)]
                 + [_resident(t.shape) for t in tabs],
        out_specs=[row_spec(d), pl.BlockSpec((1, RET_HEADS, RET_DK, RET_DV), lambda b, t: (b, 0, 0, 0))],
        out_shape=[jax.ShapeDtypeStruct((rows, d), F32),
                   jax.ShapeDtypeStruct((n_seq, RET_HEADS, RET_DK, RET_DV), F32)],
        scratch_shapes=[pltpu.VMEM((tile, HV), F32)],
        compiler_params=_params(("arbitrary", "arbitrary")),
        name="ret_prompt",
    )(q, k, v, gate, x2d, gn_g, w_out, *tabs)


def _ret_sample_body(q_ref, k_ref, v_ref, s0_ref, inner_ref, qdec_ref, kdec_ref, cdec_ref, o_ref, s_ref, *, seq_rows):
    _ret_units(q_ref, k_ref, v_ref, s0_ref, s_ref, o_ref, (inner_ref, qdec_ref, kdec_ref, cdec_ref),
               [(slice(i * seq_rows, (i + 1) * seq_rows), i) for i in range(s0_ref.shape[0])], lookahead=True)


def _ret_sample(q, k, v, states, ret_layer, n_seq, seq_rows):
    bb = 4 if n_seq % 4 == 0 else 1
    steps = n_seq // bb
    tabs = _decay_tables(seq_rows)
    row_spec = lambda w: pl.BlockSpec((bb * seq_rows, w), lambda i: (i, 0))
    st_block = (bb, RET_HEADS, RET_DK, RET_DV)
    return pl.pallas_call(
        functools.partial(_ret_sample_body, seq_rows=seq_rows),
        grid=(steps,),
        in_specs=[row_spec(HK), row_spec(HK), row_spec(HV),
                  pl.BlockSpec(st_block, lambda i: (ret_layer * steps + i, 0, 0, 0))] + [_resident(t.shape) for t in tabs],
        out_specs=[row_spec(HV), pl.BlockSpec(st_block, lambda i: (i, 0, 0, 0))],
        out_shape=[jax.ShapeDtypeStruct((n_seq * seq_rows, HV), F32),
                   jax.ShapeDtypeStruct((n_seq, RET_HEADS, RET_DK, RET_DV), F32)],
        compiler_params=_params(("arbitrary",)),
        name="ret_sample",
    )(q, k, v, states, *tabs)


def _retout_body(o_ref, gate_ref, x_ref, gn_ref, w_ref, out_ref):
    out_ref[...] = x_ref[...] + _dot(_gn_gate(o_ref[...], gate_ref[...], gn_ref[0]), w_ref[0])


def _ret_out(o, gate, x2d, gn_g, w_out, ret_layer):
    rows, d = x2d.shape
    tile = min(rows, ROW_TILE)
    row_spec = lambda w: pl.BlockSpec((tile, w), lambda r: (r, 0))
    return pl.pallas_call(
        _retout_body,
        grid=(rows // tile,),
        in_specs=[row_spec(HV), row_spec(HV), row_spec(d), _layer_resident(gn_g.shape, ret_layer),
                  _layer_resident(w_out.shape, ret_layer)],
        out_specs=row_spec(d),
        out_shape=jax.ShapeDtypeStruct((rows, d), F32),
        compiler_params=_params(("arbitrary",)),
        name="ret_out",
    )(o, gate, x2d, gn_g, w_out)


def _softmax(s):
    e = jnp.exp(s - jnp.max(s, axis=-1, keepdims=True))
    return e / jnp.sum(e, axis=-1, keepdims=True)


def _xattn_block(x_ref, g_ref, wq_ref, wo_ref, keys, values, out_ref, att_ref, *s_refs, n_seq, seq_rows):
    head_cols = [slice(hd * X_HEAD_DIM, (hd + 1) * X_HEAD_DIM) for hd in range(X_HEADS)]
    h = _rms(x_ref[...], g_ref[0]).astype(BF16)
    q = _dot(h, wq_ref[...]) * (X_HEAD_DIM ** -0.5)
    if s_refs:
        s_ref, = s_refs
        pairs = [(slice(i * seq_rows, (i + 1) * seq_rows), cols, i, hd)
                 for i in range(n_seq) for hd, cols in enumerate(head_cols)]
        for n, (rows, cols, i, hd) in enumerate(pairs):
            s_ref[n * seq_rows:(n + 1) * seq_rows] = _dot_nt(q[rows, cols].astype(BF16), keys(i, hd))
        s_ref[...] = _softmax(s_ref[...])
        for n, (rows, cols, i, hd) in enumerate(pairs):
            att_ref[rows, cols] = _dot(s_ref[n * seq_rows:(n + 1) * seq_rows].astype(BF16), values(i, hd))
    else:
        scores = lambda hd: _dot_nt(q[:, head_cols[hd]].astype(BF16), keys(0, hd))
        nxt = scores(0)
        for hd, cols in enumerate(head_cols):
            s = nxt
            if hd + 1 < X_HEADS:
                nxt = scores(hd + 1)
            att_ref[:, cols] = _dot(_softmax(s).astype(BF16), values(0, hd))
    out_ref[...] = x_ref[...] + _dot(att_ref[...].astype(BF16), wo_ref[...])


def _xattn_body(xl_ref, xs_ref, g_ref, wq_ref, wo_ref, kl_ref, vl_ref, ks_ref, vs_ref, outl_ref, outs_ref,
                attl_ref, atts_ref, score_ref, wqb_ref, wob_ref, kvl_ref, *,
                long_steps, short_steps, tiles_per_seq, short_seqs, short_rows):
    step = pl.program_id(0)
    idx = step // 2
    n_mem = kl_ref.shape[1] // KV_ROW_STRIDE

    @pl.when(step == 0)
    def _():
        wqb_ref[...] = wq_ref[0].astype(BF16)
        wob_ref[...] = wo_ref[0].astype(BF16)

    @pl.when((step % 2 == 0) & (idx < long_steps))
    def _():
        @pl.when(idx % tiles_per_seq == 0)
        def _():
            for hd in range(X_HEADS):
                kvl_ref[hd] = _head_kv(kl_ref, 0, hd, n_mem)
                kvl_ref[X_HEADS + hd] = _head_kv(vl_ref, 0, hd, n_mem)

        _xattn_block(xl_ref, g_ref, wqb_ref, wob_ref, lambda i, hd: kvl_ref[hd], lambda i, hd: kvl_ref[X_HEADS + hd],
                     outl_ref, attl_ref, n_seq=1, seq_rows=xl_ref.shape[0])

    @pl.when((step % 2 == 1) & (idx < short_steps))
    def _():
        _xattn_block(xs_ref, g_ref, wqb_ref, wob_ref, lambda i, hd: _head_kv(ks_ref, i, hd, n_mem),
                     lambda i, hd: _head_kv(vs_ref, i, hd, n_mem), outs_ref, atts_ref, score_ref,
                     n_seq=short_seqs, seq_rows=short_rows)


def _xattn(x_long, x_short, gains, wq, wo, layer, kv_long, kv_short, n_long, n_short):
    d = x_long.shape[1]
    long_rows, short_rows = x_long.shape[0] // n_long, x_short.shape[0] // n_short
    assert long_rows % ROW_TILE == 0
    n_mem = kv_long[0].shape[1]
    tps = long_rows // ROW_TILE
    long_steps = n_long * tps
    bb = 4 if n_short % 4 == 0 else 1
    short_steps = n_short // bb
    li = lambda s: jnp.minimum(s // 2, long_steps - 1)
    si = lambda s: jnp.minimum(jnp.maximum(s - 1, 0) // 2, short_steps - 1)
    long_spec = pl.BlockSpec((ROW_TILE, d), lambda s: (li(s), 0))
    short_spec = pl.BlockSpec((bb * short_rows, d), lambda s: (si(s), 0))
    kvl_spec = pl.BlockSpec((1, n_mem, LANES), lambda s: (layer * n_long + li(s) // tps, 0, 0))
    kvs_spec = pl.BlockSpec((bb, n_mem, LANES), lambda s: (layer * short_steps + si(s), 0, 0))
    mem_tokens = n_mem // KV_ROW_STRIDE
    return pl.pallas_call(
        functools.partial(_xattn_body, long_steps=long_steps, short_steps=short_steps, tiles_per_seq=tps,
                          short_seqs=bb, short_rows=short_rows),
        grid=(2 * max(long_steps, short_steps),),
        in_specs=[long_spec, short_spec, _layer_resident(gains.shape, layer), _layer_resident(wq.shape, layer),
                  _layer_resident(wo.shape, layer), kvl_spec, kvl_spec, kvs_spec, kvs_spec],
        out_specs=[long_spec, short_spec],
        out_shape=[jax.ShapeDtypeStruct(x_long.shape, F32), jax.ShapeDtypeStruct(x_short.shape, F32)],
        scratch_shapes=[pltpu.VMEM((ROW_TILE, d), F32), pltpu.VMEM((bb * short_rows, d), F32),
                        pltpu.VMEM((bb * X_HEADS * short_rows, mem_tokens), F32),
                        pltpu.VMEM((d, d), BF16), pltpu.VMEM((d, d), BF16),
                        pltpu.VMEM((2 * X_HEADS, mem_tokens, X_HEAD_DIM), BF16)],
        compiler_params=_params(("arbitrary",)),
        name="xattn",
    )(x_long, x_short, gains, wq, wo, *kv_long, *kv_short)


def _conv_chunk(u, prev, cwb, shift):
    c = cwb[3:4] + cwb[0:1] * shift(u, prev, 2)
    c = c + cwb[1:2] * shift(u, prev, 1)
    return c + cwb[2:3] * u


def _ffn_gate(ua, ug, prev_a, prev_g, cwb_a, cwb_g, shift):
    return (_conv_chunk(ua, prev_a, cwb_a, shift) * _silu(_conv_chunk(ug, prev_g, cwb_g, shift))).astype(BF16)


def _ffn_chunk(h, wa, wg, wd, prev_a, prev_g, cwb_a, cwb_g, shift):
    ua = _dot(h, wa)
    ug = _dot(h, wg)
    return _dot(_ffn_gate(ua, ug, prev_a, prev_g, cwb_a, cwb_g, shift), wd), ua, ug


def _ffn_prompt_body(x_ref, g_ref, gf_ref, wu_ref, wd_ref, cwb_ref, out_ref, c_ref,
                     h_ref, acc_ref, ua0_ref, ug0_ref, ua1_ref, ug1_ref, z0_ref, z1_ref, *, final_norm):
    halo = (CONV_W - 1) * SUBLANES
    tile = acc_ref.shape[0]
    groups = tile // SUBLANES

    @pl.when(pl.program_id(1) == 0)
    def _():
        c_ref[...] = jnp.zeros_like(c_ref)

    x = jnp.swapaxes(x_ref[...].reshape(SUBLANES, groups, D_MODEL), 0, 1).reshape(tile, D_MODEL)
    h_ref[...] = _rms(x, g_ref[0]).astype(BF16)
    acc_ref[...] = x
    slots = ((ua0_ref, ug0_ref), (ua1_ref, ug1_ref))
    starts = range(0, D_FF, FFN_PROMPT_CHUNK)
    n_chunks = len(starts)

    def cols(f):
        lo, hi = starts[f], min(starts[f] + FFN_PROMPT_CHUNK, D_FF)
        return slice(lo, hi), slice(D_FF + lo, D_FF + hi), hi - lo

    def up(f):
        ca, cg, width = cols(f)
        first_sublane = lax.broadcasted_iota(jnp.int32, (SUBLANES, width), 0) == 0
        for u_ref, cc in zip(slots[f % 2], (ca, cg)):
            u = _dot(h_ref[...], wu_ref[0, :, cc])
            u_ref[halo:, :width] = u
            for m in range(CONV_W - 1):
                rows = slice(m * SUBLANES, (m + 1) * SUBLANES)
                src = u[tile - halo + m * SUBLANES: tile - halo + (m + 1) * SUBLANES]
                u_ref[rows, :width] = jnp.where(first_sublane, pltpu.roll(c_ref[0, rows, cc], 1, axis=0),
                                                pltpu.roll(src, 1, axis=0))
                c_ref[0, rows, cc] = src

    def conv(u_ref, cw, r0, n, width):
        c = cw[3:4] + cw[0:1] * u_ref[r0: r0 + n, :width]
        c = c + cw[1:2] * u_ref[r0 + SUBLANES: r0 + SUBLANES + n, :width]
        return c + cw[2:3] * u_ref[r0 + halo: r0 + halo + n, :width]

    def gate(f):
        ca, cg, width = cols(f)
        ua_ref, ug_ref = slots[f % 2]
        cwa, cwg = cwb_ref[0, :, ca], cwb_ref[0, :, cg]
        z_ref = (z0_ref, z1_ref)[f % 2]
        for r0 in range(0, tile, FFN_GATE_ROWS):
            z_ref[r0: r0 + FFN_GATE_ROWS, :width] = (
                conv(ua_ref, cwa, r0, FFN_GATE_ROWS, width)
                * _silu(conv(ug_ref, cwg, r0, FFN_GATE_ROWS, width))).astype(BF16)
        return z_ref[:, :width]

    up(0)
    for f in range(n_chunks):
        if f + 1 < n_chunks:
            up(f + 1)
        acc_ref[...] += _dot(gate(f), wd_ref[0, cols(f)[0], :])
    y = acc_ref[...]
    if final_norm:
        y = _rms(y, gf_ref[...])
    out_ref[...] = jnp.swapaxes(y.reshape(groups, SUBLANES, D_MODEL), 0, 1).reshape(tile, D_MODEL)


def _ffn_prompt(x2d, gains, g_final, w_up, w_down, cwb, layer, n_seq, final_norm):
    rows, d = x2d.shape
    seq_rows = rows // n_seq
    tile = min(seq_rows, FFN_ROW_TILE)
    tps = seq_rows // tile
    halo = (CONV_W - 1) * SUBLANES
    row_spec = pl.BlockSpec((tile, d), lambda b, t: (b * tps + t, 0))
    u_scratch = pltpu.VMEM((halo + tile, FFN_PROMPT_CHUNK), F32)
    z_scratch = pltpu.VMEM((tile, FFN_PROMPT_CHUNK), BF16)
    y, carry = pl.pallas_call(
        functools.partial(_ffn_prompt_body, final_norm=final_norm),
        grid=(n_seq, tps),
        in_specs=[row_spec, _layer_resident(gains.shape, layer), _resident((1, d)),
                  _layer_resident(w_up.shape, layer), _layer_resident(w_down.shape, layer),
                  _layer_resident(cwb.shape, layer)],
        out_specs=[row_spec, pl.BlockSpec((1, halo, 2 * D_FF), lambda b, t: (b, 0, 0))],
        out_shape=[jax.ShapeDtypeStruct((rows, d), F32), jax.ShapeDtypeStruct((n_seq, halo, 2 * D_FF), F32)],
        scratch_shapes=[pltpu.VMEM((tile, d), BF16), pltpu.VMEM((tile, d), F32),
                        u_scratch, u_scratch, u_scratch, u_scratch, z_scratch, z_scratch],
        compiler_params=_params(("arbitrary", "arbitrary")),
        name="ffn_prompt",
    )(x2d, gains, g_final.reshape(1, d), w_up, w_down, cwb)
    return y, carry[:, SUBLANES - 1::SUBLANES]


def _ffn_sample_body(x_ref, g_ref, gf_ref, wa_ref, wg_ref, wd_ref, cwa_ref, cwg_ref, ba_ref, bg_ref,
                     out_ref, ca_ref, cg_ref, h_ref, acc_ref, *, n_seq, seq_rows, final_norm):
    f = pl.program_id(0)

    @pl.when(f == 0)
    def _():
        xt = _to_time_major(x_ref[...], n_seq, seq_rows)
        h_ref[...] = _rms(xt, g_ref[0]).astype(BF16)
        acc_ref[...] = xt

    rows = n_seq * seq_rows
    hist = (CONV_W - 1) * n_seq

    def shift(u, prev, j):
        return jnp.concatenate([prev, u], axis=0)[hist - j * n_seq: hist - j * n_seq + rows]

    y, ua, ug = _ffn_chunk(h_ref[...], wa_ref[0], wg_ref[0], wd_ref[0], ba_ref[...], bg_ref[...],
                           cwa_ref[0], cwg_ref[0], shift)
    acc_ref[...] += y
    ca_ref[...] = ua[rows - hist:]
    cg_ref[...] = ug[rows - hist:]

    @pl.when(f == pl.num_programs(0) - 1)
    def _():
        yt = acc_ref[...]
        if final_norm:
            yt = _rms(yt, gf_ref[...])
        out_ref[...] = _from_time_major(yt, n_seq, seq_rows)


def _ffn_sample(x2d, gains, g_final, w_up, w_down, cwb, layer, buf, n_seq, final_norm):
    rows, d = x2d.shape
    seq_rows = rows // n_seq
    hist = (CONV_W - 1) * n_seq
    a_col = lambda f: f
    g_col = lambda f: NF + f
    c_shape = jax.ShapeDtypeStruct((hist, D_FF), F32)
    c_spec = pl.BlockSpec((hist, FF_CHUNK), lambda f: (0, f))
    return pl.pallas_call(
        functools.partial(_ffn_sample_body, n_seq=n_seq, seq_rows=seq_rows, final_norm=final_norm),
        grid=(NF,),
        in_specs=[_resident((rows, d)), _layer_resident(gains.shape, layer), _resident((1, d)),
                  pl.BlockSpec((1, d, FF_CHUNK), lambda f: (layer, 0, a_col(f))),
                  pl.BlockSpec((1, d, FF_CHUNK), lambda f: (layer, 0, g_col(f))),
                  pl.BlockSpec((1, FF_CHUNK, d), lambda f: (layer, f, 0)),
                  pl.BlockSpec((1, 4, FF_CHUNK), lambda f: (layer, 0, a_col(f))),
                  pl.BlockSpec((1, 4, FF_CHUNK), lambda f: (layer, 0, g_col(f))),
                  pl.BlockSpec((hist, FF_CHUNK), lambda f: (0, a_col(f))),
                  pl.BlockSpec((hist, FF_CHUNK), lambda f: (0, g_col(f)))],
        out_specs=[pl.BlockSpec((rows, d), lambda f: (0, 0)), c_spec, c_spec],
        out_shape=[jax.ShapeDtypeStruct((rows, d), F32), c_shape, c_shape],
        scratch_shapes=[pltpu.VMEM((rows, d), BF16), pltpu.VMEM((rows, d), F32)],
        compiler_params=_params(("arbitrary",)),
        name="ffn_sample",
    )(x2d, gains, g_final.reshape(1, d), w_up, w_up, w_down, cwb, cwb, buf, buf)


def _pool_mix(pooled, w_ref, scale_ref, x):
    mixed = [_dot(pooled[g].astype(BF16), w_ref[0, g]) for g in range(len(POOL_WINDOWS))]
    return x + jnp.concatenate(mixed, axis=-1) * scale_ref[0]


def _pool_prompt_body(x_ref, g_ref, w_ref, scale_ref, out_ref, hist_ref):
    t = pl.program_id(1)

    @pl.when(t == 0)
    def _():
        hist_ref[...] = jnp.zeros_like(hist_ref)

    x = x_ref[...]
    tile = x.shape[0]
    h = _rms(x, g_ref[0])
    ext = jnp.concatenate([hist_ref[0], h], axis=0)
    hist_ref[0] = h[tile - POOL_CARRY:]
    pos1 = (t * tile + 1 + lax.broadcasted_iota(jnp.int32, (tile, POOL_GC), 0)).astype(F32)
    pooled = []
    s = ext
    for g, w in enumerate(POOL_WINDOWS):
        s = s[:, POOL_GC * (1 if g else 0):]
        s = s + pltpu.roll(s, w // 2, axis=0)
        cnt = jnp.minimum(pos1, float(w))
        pooled.append(s[POOL_CARRY:, :POOL_GC] / cnt - h[:, g * POOL_GC:(g + 1) * POOL_GC])
    out_ref[...] = _pool_mix(pooled, w_ref, scale_ref, x)


def _pool_prompt(x2d, gains, layer, pool_w, pool_scale, pool_layer, n_seq):
    rows, d = x2d.shape
    seq_rows = rows // n_seq
    tile = min(seq_rows, ROW_TILE)
    tps = seq_rows // tile
    row_spec = pl.BlockSpec((tile, d), lambda b, t: (b * tps + t, 0))
    return pl.pallas_call(
        _pool_prompt_body,
        grid=(n_seq, tps),
        in_specs=[row_spec, _layer_resident(gains.shape, layer), _layer_resident(pool_w.shape, pool_layer),
                  _layer_resident(pool_scale.shape, pool_layer)],
        out_specs=[row_spec, pl.BlockSpec((1, POOL_CARRY, d), lambda b, t: (b, 0, 0))],
        out_shape=[jax.ShapeDtypeStruct((rows, d), F32), jax.ShapeDtypeStruct((n_seq, POOL_CARRY, d), F32)],
        compiler_params=_params(("arbitrary", "arbitrary")),
        name="pool_prompt",
    )(x2d, gains, pool_w, pool_scale)


def _pool_sample_body(x_ref, g_ref, w_ref, scale_ref, buf_ref, out_ref, h_ref, *, n_seq, seq_rows, pos0):
    xt = _to_time_major(x_ref[...], n_seq, seq_rows)
    h = _rms(xt, g_ref[0])
    s = jnp.concatenate([buf_ref[...], h], axis=0)
    rows = n_seq * seq_rows
    first = POOL_BUF
    pooled = []
    for g, w in enumerate(POOL_WINDOWS):
        s = s[:, POOL_GC * (1 if g else 0):]
        step = (w // 2) * n_seq
        s = s[step:] + s[:-step]
        first -= w // 2
        win = s[first * n_seq: first * n_seq + rows, :POOL_GC]
        inv = [1.0 / min(pos0 + t + 1, w) for t in range(seq_rows)]
        if len(set(inv)) == 1:
            win = win * inv[0]
        else:
            win = jnp.concatenate([win[t * n_seq:(t + 1) * n_seq] * inv[t] for t in range(seq_rows)], axis=0)
        pooled.append(win - h[:, g * POOL_GC:(g + 1) * POOL_GC])
    out_ref[...] = _from_time_major(_pool_mix(pooled, w_ref, scale_ref, xt), n_seq, seq_rows)
    h_ref[...] = _from_time_major(h, n_seq, seq_rows)


def _pool_sample(x2d, gains, layer, pool_w, pool_scale, pool_layer, buf_t, n_seq, pos0):
    rows, d = x2d.shape
    seq_rows = rows // n_seq
    full = lambda s: pl.BlockSpec(s, lambda i: (0,) * len(s))
    return pl.pallas_call(
        functools.partial(_pool_sample_body, n_seq=n_seq, seq_rows=seq_rows, pos0=pos0),
        grid=(1,),
        in_specs=[full((rows, d)), _layer_resident(gains.shape, layer), _layer_resident(pool_w.shape, pool_layer),
                  _layer_resident(pool_scale.shape, pool_layer), full(buf_t.shape)],
        out_specs=[full((rows, d)), full((rows, d))],
        out_shape=[jax.ShapeDtypeStruct((rows, d), F32)] * 2,
        compiler_params=_params(("arbitrary",)),
        name="pool_sample",
    )(x2d, gains, pool_w, pool_scale, buf_t)


def _rope_tables(pos):
    inv = 1.0 / (ROPE_BASE ** (jnp.arange(0, RET_DK, 2, dtype=F32) / RET_DK))
    ang = pos[:, None] * inv[None, :]
    return jnp.cos(ang), jnp.sin(ang)


def kernel(x_prompt, x_sample, mem_prompt, cache_mem_k, cache_mem_v, state_ret, cache_pool, cache_ffn_conv, w_ret_in, ret_gn, w_ret_out, pool_w, pool_scale, norm_mem, w_xq, w_xk, w_xv, w_xo, w_up, conv_w, conv_b, w_down, norm_mix, norm_xattn, norm_ffn, norm_final):
    bp, lp, d = x_prompt.shape
    bs, ls, _ = x_sample.shape
    depth = w_up.shape[0]
    n_mem = mem_prompt.shape[1]

    w_in_b, w_out_b, pool_w_b, wk_b, wv_b, w_up_b, w_down_b = (
        w.astype(BF16) for w in (w_ret_in, w_ret_out, pool_w, w_xk, w_xv, w_up, w_down))
    cwb = jnp.concatenate([conv_w, conv_b[:, None, :]], axis=1)
    g_mix, g_x, g_ffn = _gain(norm_mix), _gain(norm_xattn), _gain(norm_ffn)
    gn_g, p_scale = _gain(ret_gn), _gain(pool_scale)

    mem_k2, mem_v2 = _mem_kv(mem_prompt.reshape(bp * n_mem, d), norm_mem, wk_b, wv_b)
    kv_p = tuple(a.reshape(depth * bp, n_mem * KV_ROW_STRIDE, LANES) for a in (mem_k2, mem_v2))
    kv_s = (_kv_rows(cache_mem_k), _kv_rows(cache_mem_v))
    states = state_ret.reshape((state_ret.shape[0] * bs,) + state_ret.shape[2:])

    cos_p, sin_p = _rope_tables(jnp.arange(lp, dtype=F32))
    cos_s, sin_s = _rope_tables(PAST_LEN + jnp.arange(ls, dtype=F32))
    cos_s, sin_s = jnp.tile(cos_s, (bs, 1)), jnp.tile(sin_s, (bs, 1))

    xp = x_prompt.reshape(bp * lp, d)
    xs = x_sample.reshape(bs * ls, d)
    ret_p, ret_s, pool_p, pool_s, conv_p, conv_s = [], [], [], [], [], []
    for i in range(depth):
        j = i // 2
        last = i == depth - 1
        if i % 2 == 0:
            q, k, v, gate = _ret_proj(xp, g_mix, i, cos_p, sin_p, w_in_b, j, bp, BF16)
            xp, s = _ret_prompt(q, k, v, gate, xp, gn_g, w_out_b, j, bp)
            ret_p.append(s)
            q, k, v, gate = _ret_proj(xs, g_mix, i, cos_s, sin_s, w_in_b, j, 1, F32)
            o, s = _ret_sample(q, k, v, states, j, bs, ls)
            xs = _ret_out(o, gate, xs, gn_g, w_out_b, j)
            ret_s.append(s)
        else:
            xp, hist = _pool_prompt(xp, g_mix, i, pool_w_b, p_scale, j, bp)
            pool_p.append(hist[:, POOL_CARRY - POOL_BUF:])
            buf_t = jnp.swapaxes(cache_pool[j], 0, 1).reshape(POOL_BUF * bs, d)
            xs, hs = _pool_sample(xs, g_mix, i, pool_w_b, p_scale, j, buf_t, bs, PAST_LEN)
            pool_s.append(jnp.concatenate([cache_pool[j], hs.reshape(bs, ls, d)], axis=1)[:, -POOL_BUF:])

        xp, xs = _xattn(xp, xs, g_x, w_xq, w_xo, i, kv_p, kv_s, bp, bs)

        xp, tail = _ffn_prompt(xp, g_ffn, norm_final, w_up_b, w_down_b, cwb, i, bp, last)
        conv_p.append(tail)
        buf = jnp.swapaxes(cache_ffn_conv[i], 0, 1).reshape((CONV_W - 1) * bs, 2 * D_FF)
        xs, ca, cg = _ffn_sample(xs, g_ffn, norm_final, w_up_b, w_down_b, cwb, i, buf, bs, last)
        tail = jnp.concatenate([ca, cg], axis=-1)
        conv_s.append(jnp.swapaxes(tail.reshape(CONV_W - 1, bs, 2 * D_FF), 0, 1))

    mk = _kv_unrows(mem_k2, (depth, bp), n_mem)
    mv = _kv_unrows(mem_v2, (depth, bp), n_mem)
    return (xp.reshape(bp, lp, d), xs.reshape(bs, ls, d), jnp.stack(ret_p), jnp.stack(ret_s).astype(state_ret.dtype),
            jnp.stack(pool_p), jnp.stack(pool_s), jnp.stack(conv_p), jnp.stack(conv_s), mk, mv)
```

```python
import functools

import jax
import jax.numpy as jnp
from jax import lax
from jax.experimental import pallas as pl
from jax.experimental.pallas import tpu as pltpu

F32 = jnp.float32
BF16 = jnp.bfloat16

D_MODEL = 1024
PAST_LEN = 16384
RET_HEADS = 4
RET_DK = D_MODEL // RET_HEADS
RET_DV = 2 * D_MODEL // RET_HEADS
HK = RET_HEADS * RET_DK
HV = RET_HEADS * RET_DV
ROPE_BASE = 10000.0
POOL_WINDOWS = (2, 4, 8, 16)
POOL_GC = D_MODEL // len(POOL_WINDOWS)
POOL_BUF = max(POOL_WINDOWS) - 1
X_HEADS = 4
X_HEAD_DIM = D_MODEL // X_HEADS
D_FF = 2816
CONV_W = 3
NORM_EPS = 1e-6
GN_EPS = 1e-5

SUBLANES = 8
LANES = 128
KV_LANE_TILES = X_HEAD_DIM // LANES
KV_ROW_STRIDE = X_HEADS * KV_LANE_TILES
assert KV_ROW_STRIDE == SUBLANES
ROW_TILE = 512
FFN_ROW_TILE = 512
PROJ_ROW_TILE = 1024
RET_CHUNK_PROMPT = 256
FF_CHUNK = 1408
NF = D_FF // FF_CHUNK
FFN_PROMPT_CHUNK = 1536
FFN_GATE_ROWS = 64
POOL_CARRY = 16
VMEM_LIMIT = 56 * 1024 * 1024


def _params(sem):
    return pltpu.CompilerParams(dimension_semantics=sem, vmem_limit_bytes=VMEM_LIMIT)


def _resident(shape):
    zeros = (0,) * len(shape)
    return pl.BlockSpec(shape, lambda *_: zeros, pipeline_mode=pl.Buffered(1))


def _layer_resident(stacked_shape, layer):
    idx = (layer,) + (0,) * (len(stacked_shape) - 1)
    return pl.BlockSpec((1,) + tuple(stacked_shape[1:]), lambda *_: idx, pipeline_mode=pl.Buffered(1))


def _gain(g):
    return g.reshape(g.shape[0], 1, g.shape[1])


def _dot(a, b):
    return jnp.dot(a, b, preferred_element_type=F32)


def _dot_nt(a, b):
    return lax.dot_general(a, b, (((1,), (1,)), ((), ())), preferred_element_type=F32)


def _dot_tn(a, b):
    return lax.dot_general(a, b, (((0,), (0,)), ((), ())), preferred_element_type=F32)


def _rms(x, g):
    return x * lax.rsqrt(jnp.mean(x * x, axis=-1, keepdims=True) + NORM_EPS) * g


def _silu(x):
    return x * (1.0 / (1.0 + jnp.exp(-x)))


def _to_time_major(x, n_seq, seq_rows):
    return jnp.swapaxes(x.reshape(n_seq, seq_rows, x.shape[-1]), 0, 1).reshape(x.shape)


def _from_time_major(x, n_seq, seq_rows):
    return jnp.swapaxes(x.reshape(seq_rows, n_seq, x.shape[-1]), 0, 1).reshape(x.shape)


def _memkv_body(mem_ref, g_ref, wk_ref, wv_ref, k_ref, v_ref):
    h = _rms(mem_ref[...], g_ref[0]).astype(BF16)
    for w_ref, o_ref in ((wk_ref, k_ref), (wv_ref, v_ref)):
        y = _dot(h, w_ref[0].astype(BF16))
        for hd in range(X_HEADS):
            for c in range(KV_LANE_TILES):
                col = hd * X_HEAD_DIM + c * LANES
                o_ref[0, pl.ds(c * X_HEADS + hd, y.shape[0], stride=KV_ROW_STRIDE), :] = y[:, col:col + LANES]


def _mem_kv(mem2d, norm_mem, wk, wv):
    rows, d = mem2d.shape
    depth = wk.shape[0]
    tile = min(rows, 1024)
    w_spec = pl.BlockSpec((1, d, d), lambda i, r: (i, 0, 0))
    o_spec = pl.BlockSpec((1, tile * KV_ROW_STRIDE, LANES), lambda i, r: (i, r, 0))
    return pl.pallas_call(
        _memkv_body,
        grid=(depth, rows // tile),
        in_specs=[pl.BlockSpec((tile, d), lambda i, r: (r, 0)),
                  pl.BlockSpec((1, 1, d), lambda i, r: (i, 0, 0)), w_spec, w_spec],
        out_specs=[o_spec, o_spec],
        out_shape=[jax.ShapeDtypeStruct((depth, rows * KV_ROW_STRIDE, LANES), F32)] * 2,
        compiler_params=_params(("arbitrary", "arbitrary")),
        name="mem_kv",
    )(mem2d, _gain(norm_mem), wk, wv)


def _kv_rows(a):
    lead, n_mem = a.shape[:-3], a.shape[-3]
    a = a.reshape(lead + (n_mem, X_HEADS, KV_LANE_TILES, LANES))
    a = jnp.swapaxes(a, -3, -2)
    return a.reshape((-1, n_mem * KV_ROW_STRIDE, LANES))


def _kv_unrows(a, lead, n_mem):
    a = a.reshape(lead + (n_mem, KV_LANE_TILES, X_HEADS, LANES))
    a = jnp.swapaxes(a, -3, -2)
    return a.reshape(lead + (n_mem, X_HEADS, X_HEAD_DIM))


def _head_kv(ref, i, hd, n_mem):
    parts = [ref[i, pl.ds(c * X_HEADS + hd, n_mem, stride=KV_ROW_STRIDE), :] for c in range(KV_LANE_TILES)]
    return jnp.concatenate(parts, axis=1).astype(BF16)


def _retproj_body(x_ref, g_ref, cos_ref, sin_ref, w_ref, q_ref, k_ref, v_ref, gate_ref):
    h = _rms(x_ref[...], g_ref[0]).astype(BF16)
    cos = cos_ref[...]
    sin = sin_ref[...]
    half = RET_DK // 2

    def rope_store(col0, out_ref, scale):
        p = _dot(h, w_ref[0, :, col0:col0 + HK])
        for hd in range(RET_HEADS):
            a = p[:, hd * RET_DK: hd * RET_DK + half]
            b = p[:, hd * RET_DK + half: (hd + 1) * RET_DK]
            out_ref[:, hd * RET_DK: hd * RET_DK + half] = ((a * cos - b * sin) * scale).astype(out_ref.dtype)
            out_ref[:, hd * RET_DK + half: (hd + 1) * RET_DK] = ((b * cos + a * sin) * scale).astype(out_ref.dtype)

    rope_store(0, q_ref, 1.0)
    rope_store(HK, k_ref, RET_DK ** -0.5)
    v_ref[...] = _dot(h, w_ref[0, :, 2 * HK: 2 * HK + HV]).astype(v_ref.dtype)
    gate_ref[...] = _dot(h, w_ref[0, :, 2 * HK + HV:]).astype(gate_ref.dtype)


def _ret_proj(x2d, gains, layer, cos, sin, w_in, ret_layer, n_seq, out_dtype):
    rows, d = x2d.shape
    seq_rows = rows // n_seq
    tile = min(seq_rows, PROJ_ROW_TILE)
    tps = seq_rows // tile
    row_spec = lambda w: pl.BlockSpec((tile, w), lambda b, t: (b * tps + t, 0))
    tab_spec = pl.BlockSpec((tile, RET_DK // 2), lambda b, t: (t, 0))
    return pl.pallas_call(
        _retproj_body,
        grid=(n_seq, tps),
        in_specs=[row_spec(d), _layer_resident(gains.shape, layer), tab_spec, tab_spec,
                  _layer_resident(w_in.shape, ret_layer)],
        out_specs=[row_spec(HK), row_spec(HK), row_spec(HV), row_spec(HV)],
        out_shape=[jax.ShapeDtypeStruct((rows, HK), out_dtype), jax.ShapeDtypeStruct((rows, HK), out_dtype),
                   jax.ShapeDtypeStruct((rows, HV), out_dtype), jax.ShapeDtypeStruct((rows, HV), out_dtype)],
        compiler_params=_params(("arbitrary", "arbitrary")),
        name="ret_proj",
    )(x2d, gains, cos, sin, w_in)


def _ret_units(q_ref, k_ref, v_ref, s_in, s_out, o_ref, tables, row_blocks, lookahead):
    inner_ref, qdec_ref, kdec_ref, cdec_ref = tables
    units = [(rows, i, h) for rows, i in row_blocks for h in range(RET_HEADS)]

    def issue(rows, i, h):
        qb = q_ref[rows, h * RET_DK:(h + 1) * RET_DK].astype(BF16)
        kb = k_ref[rows, h * RET_DK:(h + 1) * RET_DK].astype(BF16)
        return _dot_nt(qb, kb), _dot(qb, s_in[i, h].astype(BF16))

    def finish(rows, i, h, qk, qs):
        kh = k_ref[rows, h * RET_DK:(h + 1) * RET_DK]
        vb = v_ref[rows, h * RET_DV:(h + 1) * RET_DV].astype(BF16)
        p = (qk * inner_ref[h]).astype(BF16)
        o_ref[rows, h * RET_DV:(h + 1) * RET_DV] = _dot(p, vb) + qs * qdec_ref[h]
        kd = (kh.astype(F32) * kdec_ref[h]).astype(BF16)
        s_out[i, h] = s_in[i, h] * cdec_ref[h] + _dot_tn(kd, vb)

    if not lookahead:
        for unit in units:
            finish(*unit, *issue(*unit))
        return
    nxt = issue(*units[0])
    for n, unit in enumerate(units):
        cur = nxt
        if n + 1 < len(units):
            nxt = issue(*units[n + 1])
        finish(*unit, *cur)


def _decay_tables(chunk):
    lg = jnp.log(1.0 - 2.0 ** (-5.0 - jnp.arange(RET_HEADS, dtype=F32)))
    idx = jnp.arange(chunk, dtype=F32)
    rel = idx[:, None] - idx[None, :]
    inner = jnp.where(rel[None] >= 0, jnp.exp(jnp.maximum(rel, 0.0)[None] * lg[:, None, None]), 0.0)
    q_dec = jnp.exp((idx + 1.0)[None, :] * lg[:, None])[..., None]
    k_dec = jnp.exp((chunk - 1.0 - idx)[None, :] * lg[:, None])[..., None]
    c_dec = jnp.exp(chunk * lg)[:, None, None]
    return (inner, jnp.broadcast_to(q_dec, (RET_HEADS, chunk, RET_DV)),
            jnp.broadcast_to(k_dec, (RET_HEADS, chunk, RET_DK)),
            jnp.broadcast_to(c_dec, (RET_HEADS, 1, RET_DV)))


def _gn_gate(o, gate, gn):
    parts = []
    for h in range(RET_HEADS):
        oh = o[:, h * RET_DV:(h + 1) * RET_DV]
        mu = jnp.mean(oh, axis=-1, keepdims=True)
        dlt = oh - mu
        var = jnp.mean(dlt * dlt, axis=-1, keepdims=True)
        parts.append(dlt * lax.rsqrt(var + GN_EPS))
    on = jnp.concatenate(parts, axis=-1) * gn
    return (_silu(gate.astype(F32)) * on).astype(BF16)


def _ret_prompt_body(q_ref, k_ref, v_ref, gate_ref, x_ref, gn_ref, w_ref, inner_ref, qdec_ref, kdec_ref, cdec_ref,
                     out_ref, s_ref, o_ref, *, chunk):
    @pl.when(pl.program_id(1) == 0)
    def _():
        s_ref[...] = jnp.zeros_like(s_ref)

    chunks = [slice(c * chunk, (c + 1) * chunk) for c in range(q_ref.shape[0] // chunk)]
    _ret_units(q_ref, k_ref, v_ref, s_ref, s_ref, o_ref, (inner_ref, qdec_ref, kdec_ref, cdec_ref),
               [(rows, 0) for rows in chunks], lookahead=False)
    for rows in chunks:
        z = _gn_gate(o_ref[rows], gate_ref[rows], gn_ref[0])
        out_ref[rows] = x_ref[rows] + _dot(z, w_ref[0])


def _ret_prompt(q, k, v, gate, x2d, gn_g, w_out, ret_layer, n_seq):
    rows, d = x2d.shape
    seq_rows = rows // n_seq
    tile = min(seq_rows, ROW_TILE)
    chunk = min(tile, RET_CHUNK_PROMPT)
    tps = seq_rows // tile
    tabs = _decay_tables(chunk)
    row_spec = lambda w: pl.BlockSpec((tile, w), lambda b, t: (b * tps + t, 0))
    return pl.pallas_call(
        functools.partial(_ret_prompt_body, chunk=chunk),
        grid=(n_seq, tps),
        in_specs=[row_spec(HK), row_spec(HK), row_spec(HV), row_spec(HV), row_spec(d),
                  _layer_resident(gn_g.shape, ret_layer), _layer_resident(w_out.shape, ret_layer)]
                 + [_resident(t.shape) for t in tabs],
        out_specs=[row_spec(d), pl.BlockSpec((1, RET_HEADS, RET_DK, RET_DV), lambda b, t: (b, 0, 0, 0))],
        out_shape=[jax.ShapeDtypeStruct((rows, d), F32),
                   jax.ShapeDtypeStruct((n_seq, RET_HEADS, RET_DK, RET_DV), F32)],
        scratch_shapes=[pltpu.VMEM((tile, HV), F32)],
        compiler_params=_params(("arbitrary", "arbitrary")),
        name="ret_prompt",
    )(q, k, v, gate, x2d, gn_g, w_out, *tabs)


def _ret_sample_body(q_ref, k_ref, v_ref, s0_ref, inner_ref, qdec_ref, kdec_ref, cdec_ref, o_ref, s_ref, *, seq_rows):
    _ret_units(q_ref, k_ref, v_ref, s0_ref, s_ref, o_ref, (inner_ref, qdec_ref, kdec_ref, cdec_ref),
               [(slice(i * seq_rows, (i + 1) * seq_rows), i) for i in range(s0_ref.shape[0])], lookahead=True)


def _ret_sample(q, k, v, states, ret_layer, n_seq, seq_rows):
    bb = 4 if n_seq % 4 == 0 else 1
    steps = n_seq // bb
    tabs = _decay_tables(seq_rows)
    row_spec = lambda w: pl.BlockSpec((bb * seq_rows, w), lambda i: (i, 0))
    st_block = (bb, RET_HEADS, RET_DK, RET_DV)
    return pl.pallas_call(
        functools.partial(_ret_sample_body, seq_rows=seq_rows),
        grid=(steps,),
        in_specs=[row_spec(HK), row_spec(HK), row_spec(HV),
                  pl.BlockSpec(st_block, lambda i: (ret_layer * steps + i, 0, 0, 0))] + [_resident(t.shape) for t in tabs],
        out_specs=[row_spec(HV), pl.BlockSpec(st_block, lambda i: (i, 0, 0, 0))],
        out_shape=[jax.ShapeDtypeStruct((n_seq * seq_rows, HV), F32),
                   jax.ShapeDtypeStruct((n_seq, RET_HEADS, RET_DK, RET_DV), F32)],
        compiler_params=_params(("arbitrary",)),
        name="ret_sample",
    )(q, k, v, states, *tabs)


def _retout_body(o_ref, gate_ref, x_ref, gn_ref, w_ref, out_ref):
    out_ref[...] = x_ref[...] + _dot(_gn_gate(o_ref[...], gate_ref[...], gn_ref[0]), w_ref[0])


def _ret_out(o, gate, x2d, gn_g, w_out, ret_layer):
    rows, d = x2d.shape
    tile = min(rows, ROW_TILE)
    row_spec = lambda w: pl.BlockSpec((tile, w), lambda r: (r, 0))
    return pl.pallas_call(
        _retout_body,
        grid=(rows // tile,),
        in_specs=[row_spec(HV), row_spec(HV), row_spec(d), _layer_resident(gn_g.shape, ret_layer),
                  _layer_resident(w_out.shape, ret_layer)],
        out_specs=row_spec(d),
        out_shape=jax.ShapeDtypeStruct((rows, d), F32),
        compiler_params=_params(("arbitrary",)),
        name="ret_out",
    )(o, gate, x2d, gn_g, w_out)


def _softmax(s):
    e = jnp.exp(s - jnp.max(s, axis=-1, keepdims=True))
    return e / jnp.sum(e, axis=-1, keepdims=True)


def _xattn_block(x_ref, g_ref, wq_ref, wo_ref, keys, values, out_ref, att_ref, *s_refs, n_seq, seq_rows):
    head_cols = [slice(hd * X_HEAD_DIM, (hd + 1) * X_HEAD_DIM) for hd in range(X_HEADS)]
    h = _rms(x_ref[...], g_ref[0]).astype(BF16)
    q = _dot(h, wq_ref[...]) * (X_HEAD_DIM ** -0.5)
    if s_refs:
        s_ref, = s_refs
        pairs = [(slice(i * seq_rows, (i + 1) * seq_rows), cols, i, hd)
                 for i in range(n_seq) for hd, cols in enumerate(head_cols)]
        for n, (rows, cols, i, hd) in enumerate(pairs):
            s_ref[n * seq_rows:(n + 1) * seq_rows] = _dot_nt(q[rows, cols].astype(BF16), keys(i, hd))
        s_ref[...] = _softmax(s_ref[...])
        for n, (rows, cols, i, hd) in enumerate(pairs):
            att_ref[rows, cols] = _dot(s_ref[n * seq_rows:(n + 1) * seq_rows].astype(BF16), values(i, hd))
    else:
        scores = lambda hd: _dot_nt(q[:, head_cols[hd]].astype(BF16), keys(0, hd))
        nxt = scores(0)
        for hd, cols in enumerate(head_cols):
            s = nxt
            if hd + 1 < X_HEADS:
                nxt = scores(hd + 1)
            att_ref[:, cols] = _dot(_softmax(s).astype(BF16), values(0, hd))
    out_ref[...] = x_ref[...] + _dot(att_ref[...].astype(BF16), wo_ref[...])


def _xattn_body(xl_ref, xs_ref, g_ref, wq_ref, wo_ref, kl_ref, vl_ref, ks_ref, vs_ref, outl_ref, outs_ref,
                attl_ref, atts_ref, score_ref, wqb_ref, wob_ref, kvl_ref, *,
                long_steps, short_steps, tiles_per_seq, short_seqs, short_rows):
    step = pl.program_id(0)
    idx = step // 2
    n_mem = kl_ref.shape[1] // KV_ROW_STRIDE

    @pl.when(step == 0)
    def _():
        wqb_ref[...] = wq_ref[0].astype(BF16)
        wob_ref[...] = wo_ref[0].astype(BF16)

    @pl.when((step % 2 == 0) & (idx < long_steps))
    def _():
        @pl.when(idx % tiles_per_seq == 0)
        def _():
            for hd in range(X_HEADS):
                kvl_ref[hd] = _head_kv(kl_ref, 0, hd, n_mem)
                kvl_ref[X_HEADS + hd] = _head_kv(vl_ref, 0, hd, n_mem)

        _xattn_block(xl_ref, g_ref, wqb_ref, wob_ref, lambda i, hd: kvl_ref[hd], lambda i, hd: kvl_ref[X_HEADS + hd],
                     outl_ref, attl_ref, n_seq=1, seq_rows=xl_ref.shape[0])

    @pl.when((step % 2 == 1) & (idx < short_steps))
    def _():
        _xattn_block(xs_ref, g_ref, wqb_ref, wob_ref, lambda i, hd: _head_kv(ks_ref, i, hd, n_mem),
                     lambda i, hd: _head_kv(vs_ref, i, hd, n_mem), outs_ref, atts_ref, score_ref,
                     n_seq=short_seqs, seq_rows=short_rows)


def _xattn(x_long, x_short, gains, wq, wo, layer, kv_long, kv_short, n_long, n_short):
    d = x_long.shape[1]
    long_rows, short_rows = x_long.shape[0] // n_long, x_short.shape[0] // n_short
    assert long_rows % ROW_TILE == 0
    n_mem = kv_long[0].shape[1]
    tps = long_rows // ROW_TILE
    long_steps = n_long * tps
    bb = 4 if n_short % 4 == 0 else 1
    short_steps = n_short // bb
    li = lambda s: jnp.minimum(s // 2, long_steps - 1)
    si = lambda s: jnp.minimum(jnp.maximum(s - 1, 0) // 2, short_steps - 1)
    long_spec = pl.BlockSpec((ROW_TILE, d), lambda s: (li(s), 0))
    short_spec = pl.BlockSpec((bb * short_rows, d), lambda s: (si(s), 0))
    kvl_spec = pl.BlockSpec((1, n_mem, LANES), lambda s: (layer * n_long + li(s) // tps, 0, 0))
    kvs_spec = pl.BlockSpec((bb, n_mem, LANES), lambda s: (layer * short_steps + si(s), 0, 0))
    mem_tokens = n_mem // KV_ROW_STRIDE
    return pl.pallas_call(
        functools.partial(_xattn_body, long_steps=long_steps, short_steps=short_steps, tiles_per_seq=tps,
                          short_seqs=bb, short_rows=short_rows),
        grid=(2 * max(long_steps, short_steps),),
        in_specs=[long_spec, short_spec, _layer_resident(gains.shape, layer), _layer_resident(wq.shape, layer),
                  _layer_resident(wo.shape, layer), kvl_spec, kvl_spec, kvs_spec, kvs_spec],
        out_specs=[long_spec, short_spec],
        out_shape=[jax.ShapeDtypeStruct(x_long.shape, F32), jax.ShapeDtypeStruct(x_short.shape, F32)],
        scratch_shapes=[pltpu.VMEM((ROW_TILE, d), F32), pltpu.VMEM((bb * short_rows, d), F32),
                        pltpu.VMEM((bb * X_HEADS * short_rows, mem_tokens), F32),
                        pltpu.VMEM((d, d), BF16), pltpu.VMEM((d, d), BF16),
                        pltpu.VMEM((2 * X_HEADS, mem_tokens, X_HEAD_DIM), BF16)],
        compiler_params=_params(("arbitrary",)),
        name="xattn",
    )(x_long, x_short, gains, wq, wo, *kv_long, *kv_short)


def _conv_chunk(u, prev, cwb, shift):
    c = cwb[3:4] + cwb[0:1] * shift(u, prev, 2)
    c = c + cwb[1:2] * shift(u, prev, 1)
    return c + cwb[2:3] * u


def _ffn_gate(ua, ug, prev_a, prev_g, cwb_a, cwb_g, shift):
    return (_conv_chunk(ua, prev_a, cwb_a, shift) * _silu(_conv_chunk(ug, prev_g, cwb_g, shift))).astype(BF16)


def _ffn_chunk(h, wa, wg, wd, prev_a, prev_g, cwb_a, cwb_g, shift):
    ua = _dot(h, wa)
    ug = _dot(h, wg)
    return _dot(_ffn_gate(ua, ug, prev_a, prev_g, cwb_a, cwb_g, shift), wd), ua, ug


def _ffn_prompt_body(x_ref, g_ref, gf_ref, wu_ref, wd_ref, cwb_ref, out_ref, c_ref,
                     h_ref, acc_ref, ua0_ref, ug0_ref, ua1_ref, ug1_ref, z0_ref, z1_ref, *, final_norm):
    halo = (CONV_W - 1) * SUBLANES
    tile = acc_ref.shape[0]
    groups = tile // SUBLANES

    @pl.when(pl.program_id(1) == 0)
    def _():
        c_ref[...] = jnp.zeros_like(c_ref)

    x = jnp.swapaxes(x_ref[...].reshape(SUBLANES, groups, D_MODEL), 0, 1).reshape(tile, D_MODEL)
    h_ref[...] = _rms(x, g_ref[0]).astype(BF16)
    acc_ref[...] = x
    slots = ((ua0_ref, ug0_ref), (ua1_ref, ug1_ref))
    starts = range(0, D_FF, FFN_PROMPT_CHUNK)
    n_chunks = len(starts)

    def cols(f):
        lo, hi = starts[f], min(starts[f] + FFN_PROMPT_CHUNK, D_FF)
        return slice(lo, hi), slice(D_FF + lo, D_FF + hi), hi - lo

    def up(f):
        ca, cg, width = cols(f)
        first_sublane = lax.broadcasted_iota(jnp.int32, (SUBLANES, width), 0) == 0
        for u_ref, cc in zip(slots[f % 2], (ca, cg)):
            u = _dot(h_ref[...], wu_ref[0, :, cc])
            u_ref[halo:, :width] = u
            for m in range(CONV_W - 1):
                rows = slice(m * SUBLANES, (m + 1) * SUBLANES)
                src = u[tile - halo + m * SUBLANES: tile - halo + (m + 1) * SUBLANES]
                u_ref[rows, :width] = jnp.where(first_sublane, pltpu.roll(c_ref[0, rows, cc], 1, axis=0),
                                                pltpu.roll(src, 1, axis=0))
                c_ref[0, rows, cc] = src

    def conv(u_ref, cw, r0, n, width):
        c = cw[3:4] + cw[0:1] * u_ref[r0: r0 + n, :width]
        c = c + cw[1:2] * u_ref[r0 + SUBLANES: r0 + SUBLANES + n, :width]
        return c + cw[2:3] * u_ref[r0 + halo: r0 + halo + n, :width]

    def gate(f):
        ca, cg, width = cols(f)
        ua_ref, ug_ref = slots[f % 2]
        cwa, cwg = cwb_ref[0, :, ca], cwb_ref[0, :, cg]
        z_ref = (z0_ref, z1_ref)[f % 2]
        for r0 in range(0, tile, FFN_GATE_ROWS):
            z_ref[r0: r0 + FFN_GATE_ROWS, :width] = (
                conv(ua_ref, cwa, r0, FFN_GATE_ROWS, width)
                * _silu(conv(ug_ref, cwg, r0, FFN_GATE_ROWS, width))).astype(BF16)
        return z_ref[:, :width]

    up(0)
    for f in range(n_chunks):
        if f + 1 < n_chunks:
            up(f + 1)
        acc_ref[...] += _dot(gate(f), wd_ref[0, cols(f)[0], :])
    y = acc_ref[...]
    if final_norm:
        y = _rms(y, gf_ref[...])
    out_ref[...] = jnp.swapaxes(y.reshape(groups, SUBLANES, D_MODEL), 0, 1).reshape(tile, D_MODEL)


def _ffn_prompt(x2d, gains, g_final, w_up, w_down, cwb, layer, n_seq, final_norm):
    rows, d = x2d.shape
    seq_rows = rows // n_seq
    tile = min(seq_rows, FFN_ROW_TILE)
    tps = seq_rows // tile
    halo = (CONV_W - 1) * SUBLANES
    row_spec = pl.BlockSpec((tile, d), lambda b, t: (b * tps + t, 0))
    u_scratch = pltpu.VMEM((halo + tile, FFN_PROMPT_CHUNK), F32)
    z_scratch = pltpu.VMEM((tile, FFN_PROMPT_CHUNK), BF16)
    y, carry = pl.pallas_call(
        functools.partial(_ffn_prompt_body, final_norm=final_norm),
        grid=(n_seq, tps),
        in_specs=[row_spec, _layer_resident(gains.shape, layer), _resident((1, d)),
                  _layer_resident(w_up.shape, layer), _layer_resident(w_down.shape, layer),
                  _layer_resident(cwb.shape, layer)],
        out_specs=[row_spec, pl.BlockSpec((1, halo, 2 * D_FF), lambda b, t: (b, 0, 0))],
        out_shape=[jax.ShapeDtypeStruct((rows, d), F32), jax.ShapeDtypeStruct((n_seq, halo, 2 * D_FF), F32)],
        scratch_shapes=[pltpu.VMEM((tile, d), BF16), pltpu.VMEM((tile, d), F32),
                        u_scratch, u_scratch, u_scratch, u_scratch, z_scratch, z_scratch],
        compiler_params=_params(("arbitrary", "arbitrary")),
        name="ffn_prompt",
    )(x2d, gains, g_final.reshape(1, d), w_up, w_down, cwb)
    return y, carry[:, SUBLANES - 1::SUBLANES]


def _ffn_sample_body(x_ref, g_ref, gf_ref, wa_ref, wg_ref, wd_ref, cwa_ref, cwg_ref, ba_ref, bg_ref,
                     out_ref, ca_ref, cg_ref, h_ref, acc_ref, *, n_seq, seq_rows, final_norm):
    f = pl.program_id(0)

    @pl.when(f == 0)
    def _():
        xt = _to_time_major(x_ref[...], n_seq, seq_rows)
        h_ref[...] = _rms(xt, g_ref[0]).astype(BF16)
        acc_ref[...] = xt

    rows = n_seq * seq_rows
    hist = (CONV_W - 1) * n_seq

    def shift(u, prev, j):
        return jnp.concatenate([prev, u], axis=0)[hist - j * n_seq: hist - j * n_seq + rows]

    y, ua, ug = _ffn_chunk(h_ref[...], wa_ref[0], wg_ref[0], wd_ref[0], ba_ref[...], bg_ref[...],
                           cwa_ref[0], cwg_ref[0], shift)
    acc_ref[...] += y
    ca_ref[...] = ua[rows - hist:]
    cg_ref[...] = ug[rows - hist:]

    @pl.when(f == pl.num_programs(0) - 1)
    def _():
        yt = acc_ref[...]
        if final_norm:
            yt = _rms(yt, gf_ref[...])
        out_ref[...] = _from_time_major(yt, n_seq, seq_rows)


def _ffn_sample(x2d, gains, g_final, w_up, w_down, cwb, layer, buf, n_seq, final_norm):
    rows, d = x2d.shape
    seq_rows = rows // n_seq
    hist = (CONV_W - 1) * n_seq
    a_col = lambda f: f
    g_col = lambda f: NF + f
    c_shape = jax.ShapeDtypeStruct((hist, D_FF), F32)
    c_spec = pl.BlockSpec((hist, FF_CHUNK), lambda f: (0, f))
    return pl.pallas_call(
        functools.partial(_ffn_sample_body, n_seq=n_seq, seq_rows=seq_rows, final_norm=final_norm),
        grid=(NF,),
        in_specs=[_resident((rows, d)), _layer_resident(gains.shape, layer), _resident((1, d)),
                  pl.BlockSpec((1, d, FF_CHUNK), lambda f: (layer, 0, a_col(f))),
                  pl.BlockSpec((1, d, FF_CHUNK), lambda f: (layer, 0, g_col(f))),
                  pl.BlockSpec((1, FF_CHUNK, d), lambda f: (layer, f, 0)),
                  pl.BlockSpec((1, 4, FF_CHUNK), lambda f: (layer, 0, a_col(f))),
                  pl.BlockSpec((1, 4, FF_CHUNK), lambda f: (layer, 0, g_col(f))),
                  pl.BlockSpec((hist, FF_CHUNK), lambda f: (0, a_col(f))),
                  pl.BlockSpec((hist, FF_CHUNK), lambda f: (0, g_col(f)))],
        out_specs=[pl.BlockSpec((rows, d), lambda f: (0, 0)), c_spec, c_spec],
        out_shape=[jax.ShapeDtypeStruct((rows, d), F32), c_shape, c_shape],
        scratch_shapes=[pltpu.VMEM((rows, d), BF16), pltpu.VMEM((rows, d), F32)],
        compiler_params=_params(("arbitrary",)),
        name="ffn_sample",
    )(x2d, gains, g_final.reshape(1, d), w_up, w_up, w_down, cwb, cwb, buf, buf)


def _pool_mix(pooled, w_ref, scale_ref, x):
    mixed = [_dot(pooled[g].astype(BF16), w_ref[0, g]) for g in range(len(POOL_WINDOWS))]
    return x + jnp.concatenate(mixed, axis=-1) * scale_ref[0]


def _pool_prompt_body(x_ref, g_ref, w_ref, scale_ref, out_ref, hist_ref):
    t = pl.program_id(1)

    @pl.when(t == 0)
    def _():
        hist_ref[...] = jnp.zeros_like(hist_ref)

    x = x_ref[...]
    tile = x.shape[0]
    h = _rms(x, g_ref[0])
    ext = jnp.concatenate([hist_ref[0], h], axis=0)
    hist_ref[0] = h[tile - POOL_CARRY:]
    pos1 = (t * tile + 1 + lax.broadcasted_iota(jnp.int32, (tile, POOL_GC), 0)).astype(F32)
    pooled = []
    s = ext
    for g, w in enumerate(POOL_WINDOWS):
        s = s[:, POOL_GC * (1 if g else 0):]
        s = s + pltpu.roll(s, w // 2, axis=0)
        cnt = jnp.minimum(pos1, float(w))
        pooled.append(s[POOL_CARRY:, :POOL_GC] / cnt - h[:, g * POOL_GC:(g + 1) * POOL_GC])
    out_ref[...] = _pool_mix(pooled, w_ref, scale_ref, x)


def _pool_prompt(x2d, gains, layer, pool_w, pool_scale, pool_layer, n_seq):
    rows, d = x2d.shape
    seq_rows = rows // n_seq
    tile = min(seq_rows, ROW_TILE)
    tps = seq_rows // tile
    row_spec = pl.BlockSpec((tile, d), lambda b, t: (b * tps + t, 0))
    return pl.pallas_call(
        _pool_prompt_body,
        grid=(n_seq, tps),
        in_specs=[row_spec, _layer_resident(gains.shape, layer), _layer_resident(pool_w.shape, pool_layer),
                  _layer_resident(pool_scale.shape, pool_layer)],
        out_specs=[row_spec, pl.BlockSpec((1, POOL_CARRY, d), lambda b, t: (b, 0, 0))],
        out_shape=[jax.ShapeDtypeStruct((rows, d), F32), jax.ShapeDtypeStruct((n_seq, POOL_CARRY, d), F32)],
        compiler_params=_params(("arbitrary", "arbitrary")),
        name="pool_prompt",
    )(x2d, gains, pool_w, pool_scale)


def _pool_sample_body(x_ref, g_ref, w_ref, scale_ref, buf_ref, out_ref, h_ref, *, n_seq, seq_rows, pos0):
    xt = _to_time_major(x_ref[...], n_seq, seq_rows)
    h = _rms(xt, g_ref[0])
    s = jnp.concatenate([buf_ref[...], h], axis=0)
    rows = n_seq * seq_rows
    first = POOL_BUF
    pooled = []
    for g, w in enumerate(POOL_WINDOWS):
        s = s[:, POOL_GC * (1 if g else 0):]
        step = (w // 2) * n_seq
        s = s[step:] + s[:-step]
        first -= w // 2
        win = s[first * n_seq: first * n_seq + rows, :POOL_GC]
        inv = [1.0 / min(pos0 + t + 1, w) for t in range(seq_rows)]
        if len(set(inv)) == 1:
            win = win * inv[0]
        else:
            win = jnp.concatenate([win[t * n_seq:(t + 1) * n_seq] * inv[t] for t in range(seq_rows)], axis=0)
        pooled.append(win - h[:, g * POOL_GC:(g + 1) * POOL_GC])
    out_ref[...] = _from_time_major(_pool_mix(pooled, w_ref, scale_ref, xt), n_seq, seq_rows)
    h_ref[...] = _from_time_major(h, n_seq, seq_rows)


def _pool_sample(x2d, gains, layer, pool_w, pool_scale, pool_layer, buf_t, n_seq, pos0):
    rows, d = x2d.shape
    seq_rows = rows // n_seq
    full = lambda s: pl.BlockSpec(s, lambda i: (0,) * len(s))
    return pl.pallas_call(
        functools.partial(_pool_sample_body, n_seq=n_seq, seq_rows=seq_rows, pos0=pos0),
        grid=(1,),
        in_specs=[full((rows, d)), _layer_resident(gains.shape, layer), _layer_resident(pool_w.shape, pool_layer),
                  _layer_resident(pool_scale.shape, pool_layer), full(buf_t.shape)],
        out_specs=[full((rows, d)), full((rows, d))],
        out_shape=[jax.ShapeDtypeStruct((rows, d), F32)] * 2,
        compiler_params=_params(("arbitrary",)),
        name="pool_sample",
    )(x2d, gains, pool_w, pool_scale, buf_t)


def _rope_tables(pos):
    inv = 1.0 / (ROPE_BASE ** (jnp.arange(0, RET_DK, 2, dtype=F32) / RET_DK))
    ang = pos[:, None] * inv[None, :]
    return jnp.cos(ang), jnp.sin(ang)


def kernel(x_prompt, x_sample, mem_prompt, cache_mem_k, cache_mem_v, state_ret, cache_pool, cache_ffn_conv, w_ret_in, ret_gn, w_ret_out, pool_w, pool_scale, norm_mem, w_xq, w_xk, w_xv, w_xo, w_up, conv_w, conv_b, w_down, norm_mix, norm_xattn, norm_ffn, norm_final):
    bp, lp, d = x_prompt.shape
    bs, ls, _ = x_sample.shape
    depth = w_up.shape[0]
    n_mem = mem_prompt.shape[1]

    w_in_b, w_out_b, pool_w_b, w_up_b, w_down_b = (
        w.astype(BF16) for w in (w_ret_in, w_ret_out, pool_w, w_up, w_down))
    cwb = jnp.concatenate([conv_w, conv_b[:, None, :]], axis=1)
    g_mix, g_x, g_ffn = _gain(norm_mix), _gain(norm_xattn), _gain(norm_ffn)
    gn_g, p_scale = _gain(ret_gn), _gain(pool_scale)

    mem_k2, mem_v2 = _mem_kv(mem_prompt.reshape(bp * n_mem, d), norm_mem, w_xk, w_xv)
    kv_p = tuple(a.reshape(depth * bp, n_mem * KV_ROW_STRIDE, LANES) for a in (mem_k2, mem_v2))
    kv_s = (_kv_rows(cache_mem_k), _kv_rows(cache_mem_v))
    states = state_ret.reshape((state_ret.shape[0] * bs,) + state_ret.shape[2:])

    cos_p, sin_p = _rope_tables(jnp.arange(lp, dtype=F32))
    cos_s, sin_s = _rope_tables(PAST_LEN + jnp.arange(ls, dtype=F32))
    cos_s, sin_s = jnp.tile(cos_s, (bs, 1)), jnp.tile(sin_s, (bs, 1))

    xp = x_prompt.reshape(bp * lp, d)
    xs = x_sample.reshape(bs * ls, d)
    ret_p, ret_s, pool_p, pool_s, conv_p, conv_s = [], [], [], [], [], []
    for i in range(depth):
        j = i // 2
        last = i == depth - 1
        if i % 2 == 0:
            q, k, v, gate = _ret_proj(xp, g_mix, i, cos_p, sin_p, w_in_b, j, bp, BF16)
            xp, s = _ret_prompt(q, k, v, gate, xp, gn_g, w_out_b, j, bp)
            ret_p.append(s)
            q, k, v, gate = _ret_proj(xs, g_mix, i, cos_s, sin_s, w_in_b, j, 1, F32)
            o, s = _ret_sample(q, k, v, states, j, bs, ls)
            xs = _ret_out(o, gate, xs, gn_g, w_out_b, j)
            ret_s.append(s)
        else:
            xp, hist = _pool_prompt(xp, g_mix, i, pool_w_b, p_scale, j, bp)
            pool_p.append(hist[:, POOL_CARRY - POOL_BUF:])
            buf_t = jnp.swapaxes(cache_pool[j], 0, 1).reshape(POOL_BUF * bs, d)
            xs, hs = _pool_sample(xs, g_mix, i, pool_w_b, p_scale, j, buf_t, bs, PAST_LEN)
            pool_s.append(jnp.concatenate([cache_pool[j], hs.reshape(bs, ls, d)], axis=1)[:, -POOL_BUF:])

        xp, xs = _xattn(xp, xs, g_x, w_xq, w_xo, i, kv_p, kv_s, bp, bs)

        xp, tail = _ffn_prompt(xp, g_ffn, norm_final, w_up_b, w_down_b, cwb, i, bp, last)
        conv_p.append(tail)
        buf = jnp.swapaxes(cache_ffn_conv[i], 0, 1).reshape((CONV_W - 1) * bs, 2 * D_FF)
        xs, ca, cg = _ffn_sample(xs, g_ffn, norm_final, w_up_b, w_down_b, cwb, i, buf, bs, last)
        tail = jnp.concatenate([ca, cg], axis=-1)
        conv_s.append(jnp.swapaxes(tail.reshape(CONV_W - 1, bs, 2 * D_FF), 0, 1))

    mk = _kv_unrows(mem_k2, (depth, bp), n_mem)
    mv = _kv_unrows(mem_v2, (depth, bp), n_mem)
    return (xp.reshape(bp, lp, d), xs.reshape(bs, ls, d), jnp.stack(ret_p), jnp.stack(ret_s).astype(state_ret.dtype),
            jnp.stack(pool_p), jnp.stack(pool_s), jnp.stack(conv_p), jnp.stack(conv_s), mk, mv)
```

```python
import functools

import jax
import jax.numpy as jnp
from jax import lax
from jax.experimental import pallas as pl
from jax.experimental.pallas import tpu as pltpu

F32 = jnp.float32
BF16 = jnp.bfloat16

D_MODEL = 1024
PAST_LEN = 16384
RET_HEADS = 4
RET_DK = D_MODEL // RET_HEADS
RET_DV = 2 * D_MODEL // RET_HEADS
HK = RET_HEADS * RET_DK
HV = RET_HEADS * RET_DV
ROPE_BASE = 10000.0
POOL_WINDOWS = (2, 4, 8, 16)
POOL_GC = D_MODEL // len(POOL_WINDOWS)
POOL_BUF = max(POOL_WINDOWS) - 1
X_HEADS = 4
X_HEAD_DIM = D_MODEL // X_HEADS
D_FF = 2816
CONV_W = 3
NORM_EPS = 1e-6
GN_EPS = 1e-5

SUBLANES = 8
LANES = 128
KV_LANE_TILES = X_HEAD_DIM // LANES
KV_ROW_STRIDE = X_HEADS * KV_LANE_TILES
assert KV_ROW_STRIDE == SUBLANES
ROW_TILE = 512
FFN_ROW_TILE = 512
WIDE_ROW_TILE = 1024
RET_CHUNK_PROMPT = 256
FF_CHUNK = 1408
NF = D_FF // FF_CHUNK
FFN_PROMPT_CHUNK = 1536
FFN_GATE_ROWS = 64
POOL_CARRY = 16
VMEM_LIMIT = 56 * 1024 * 1024


def _params(sem):
    return pltpu.CompilerParams(dimension_semantics=sem, vmem_limit_bytes=VMEM_LIMIT)


def _resident(shape):
    zeros = (0,) * len(shape)
    return pl.BlockSpec(shape, lambda *_: zeros, pipeline_mode=pl.Buffered(1))


def _layer_resident(stacked_shape, layer):
    idx = (layer,) + (0,) * (len(stacked_shape) - 1)
    return pl.BlockSpec((1,) + tuple(stacked_shape[1:]), lambda *_: idx, pipeline_mode=pl.Buffered(1))


def _gain(g):
    return g.reshape(g.shape[0], 1, g.shape[1])


def _dot(a, b):
    return jnp.dot(a, b, preferred_element_type=F32)


def _dot_nt(a, b):
    return lax.dot_general(a, b, (((1,), (1,)), ((), ())), preferred_element_type=F32)


def _dot_tn(a, b):
    return lax.dot_general(a, b, (((0,), (0,)), ((), ())), preferred_element_type=F32)


def _rms(x, g):
    return x * lax.rsqrt(jnp.mean(x * x, axis=-1, keepdims=True) + NORM_EPS) * g


def _silu(x):
    return x * (1.0 / (1.0 + jnp.exp(-x)))


def _to_time_major(x, n_seq, seq_rows):
    return jnp.swapaxes(x.reshape(n_seq, seq_rows, x.shape[-1]), 0, 1).reshape(x.shape)


def _from_time_major(x, n_seq, seq_rows):
    return jnp.swapaxes(x.reshape(seq_rows, n_seq, x.shape[-1]), 0, 1).reshape(x.shape)


def _memkv_body(mem_ref, g_ref, wk_ref, wv_ref, k_ref, v_ref):
    h = _rms(mem_ref[...], g_ref[0]).astype(BF16)
    for w_ref, o_ref in ((wk_ref, k_ref), (wv_ref, v_ref)):
        y = _dot(h, w_ref[0].astype(BF16))
        for hd in range(X_HEADS):
            for c in range(KV_LANE_TILES):
                col = hd * X_HEAD_DIM + c * LANES
                o_ref[0, pl.ds(c * X_HEADS + hd, y.shape[0], stride=KV_ROW_STRIDE), :] = y[:, col:col + LANES]


def _mem_kv(mem2d, norm_mem, wk, wv):
    rows, d = mem2d.shape
    depth = wk.shape[0]
    tile = min(rows, 1024)
    w_spec = pl.BlockSpec((1, d, d), lambda i, r: (i, 0, 0))
    o_spec = pl.BlockSpec((1, tile * KV_ROW_STRIDE, LANES), lambda i, r: (i, r, 0))
    return pl.pallas_call(
        _memkv_body,
        grid=(depth, rows // tile),
        in_specs=[pl.BlockSpec((tile, d), lambda i, r: (r, 0)),
                  pl.BlockSpec((1, 1, d), lambda i, r: (i, 0, 0)), w_spec, w_spec],
        out_specs=[o_spec, o_spec],
        out_shape=[jax.ShapeDtypeStruct((depth, rows * KV_ROW_STRIDE, LANES), F32)] * 2,
        compiler_params=_params(("arbitrary", "arbitrary")),
        name="mem_kv",
    )(mem2d, _gain(norm_mem), wk, wv)


def _kv_rows(a):
    lead, n_mem = a.shape[:-3], a.shape[-3]
    a = a.reshape(lead + (n_mem, X_HEADS, KV_LANE_TILES, LANES))
    a = jnp.swapaxes(a, -3, -2)
    return a.reshape((-1, n_mem * KV_ROW_STRIDE, LANES))


def _kv_unrows(a, lead, n_mem):
    a = a.reshape(lead + (n_mem, KV_LANE_TILES, X_HEADS, LANES))
    a = jnp.swapaxes(a, -3, -2)
    return a.reshape(lead + (n_mem, X_HEADS, X_HEAD_DIM))


def _head_kv(ref, i, hd, n_mem):
    parts = [ref[i, pl.ds(c * X_HEADS + hd, n_mem, stride=KV_ROW_STRIDE), :] for c in range(KV_LANE_TILES)]
    return jnp.concatenate(parts, axis=1).astype(BF16)


def _retproj_body(x_ref, g_ref, cos_ref, sin_ref, w_ref, q_ref, k_ref, v_ref, gate_ref):
    h = _rms(x_ref[...], g_ref[0]).astype(BF16)
    cos = cos_ref[...]
    sin = sin_ref[...]
    half = RET_DK // 2

    def rope_store(col0, out_ref, scale):
        p = _dot(h, w_ref[0, :, col0:col0 + HK])
        for hd in range(RET_HEADS):
            a = p[:, hd * RET_DK: hd * RET_DK + half]
            b = p[:, hd * RET_DK + half: (hd + 1) * RET_DK]
            out_ref[:, hd * RET_DK: hd * RET_DK + half] = ((a * cos - b * sin) * scale).astype(out_ref.dtype)
            out_ref[:, hd * RET_DK + half: (hd + 1) * RET_DK] = ((b * cos + a * sin) * scale).astype(out_ref.dtype)

    rope_store(0, q_ref, 1.0)
    rope_store(HK, k_ref, RET_DK ** -0.5)
    v_ref[...] = _dot(h, w_ref[0, :, 2 * HK: 2 * HK + HV]).astype(v_ref.dtype)
    gate_ref[...] = _dot(h, w_ref[0, :, 2 * HK + HV:]).astype(gate_ref.dtype)


def _ret_proj(x2d, gains, layer, cos, sin, w_in, ret_layer, n_seq, out_dtype):
    rows, d = x2d.shape
    seq_rows = rows // n_seq
    tile = min(seq_rows, WIDE_ROW_TILE)
    tps = seq_rows // tile
    row_spec = lambda w: pl.BlockSpec((tile, w), lambda b, t: (b * tps + t, 0))
    tab_spec = pl.BlockSpec((tile, RET_DK // 2), lambda b, t: (t, 0))
    return pl.pallas_call(
        _retproj_body,
        grid=(n_seq, tps),
        in_specs=[row_spec(d), _layer_resident(gains.shape, layer), tab_spec, tab_spec,
                  _layer_resident(w_in.shape, ret_layer)],
        out_specs=[row_spec(HK), row_spec(HK), row_spec(HV), row_spec(HV)],
        out_shape=[jax.ShapeDtypeStruct((rows, HK), out_dtype), jax.ShapeDtypeStruct((rows, HK), out_dtype),
                   jax.ShapeDtypeStruct((rows, HV), out_dtype), jax.ShapeDtypeStruct((rows, HV), out_dtype)],
        compiler_params=_params(("arbitrary", "arbitrary")),
        name="ret_proj",
    )(x2d, gains, cos, sin, w_in)


def _ret_units(q_ref, k_ref, v_ref, s_in, s_out, o_ref, tables, row_blocks, lookahead):
    inner_ref, qdec_ref, kdec_ref, cdec_ref = tables
    units = [(rows, i, h) for rows, i in row_blocks for h in range(RET_HEADS)]

    def issue(rows, i, h):
        qb = q_ref[rows, h * RET_DK:(h + 1) * RET_DK].astype(BF16)
        kb = k_ref[rows, h * RET_DK:(h + 1) * RET_DK].astype(BF16)
        return _dot_nt(qb, kb), _dot(qb, s_in[i, h].astype(BF16))

    def finish(rows, i, h, qk, qs):
        kh = k_ref[rows, h * RET_DK:(h + 1) * RET_DK]
        vb = v_ref[rows, h * RET_DV:(h + 1) * RET_DV].astype(BF16)
        p = (qk * inner_ref[h]).astype(BF16)
        o_ref[rows, h * RET_DV:(h + 1) * RET_DV] = _dot(p, vb) + qs * qdec_ref[h]
        kd = (kh.astype(F32) * kdec_ref[h]).astype(BF16)
        s_out[i, h] = s_in[i, h] * cdec_ref[h] + _dot_tn(kd, vb)

    if not lookahead:
        for unit in units:
            finish(*unit, *issue(*unit))
        return
    nxt = issue(*units[0])
    for n, unit in enumerate(units):
        cur = nxt
        if n + 1 < len(units):
            nxt = issue(*units[n + 1])
        finish(*unit, *cur)


def _decay_tables(chunk):
    lg = jnp.log(1.0 - 2.0 ** (-5.0 - jnp.arange(RET_HEADS, dtype=F32)))
    idx = jnp.arange(chunk, dtype=F32)
    rel = idx[:, None] - idx[None, :]
    inner = jnp.where(rel[None] >= 0, jnp.exp(jnp.maximum(rel, 0.0)[None] * lg[:, None, None]), 0.0)
    q_dec = jnp.exp((idx + 1.0)[None, :] * lg[:, None])[..., None]
    k_dec = jnp.exp((chunk - 1.0 - idx)[None, :] * lg[:, None])[..., None]
    c_dec = jnp.exp(chunk * lg)[:, None, None]
    return (inner, jnp.broadcast_to(q_dec, (RET_HEADS, chunk, RET_DV)),
            jnp.broadcast_to(k_dec, (RET_HEADS, chunk, RET_DK)),
            jnp.broadcast_to(c_dec, (RET_HEADS, 1, RET_DV)))


def _gn_gate(o, gate, gn):
    parts = []
    for h in range(RET_HEADS):
        oh = o[:, h * RET_DV:(h + 1) * RET_DV]
        mu = jnp.mean(oh, axis=-1, keepdims=True)
        dlt = oh - mu
        var = jnp.mean(dlt * dlt, axis=-1, keepdims=True)
        parts.append(dlt * lax.rsqrt(var + GN_EPS))
    on = jnp.concatenate(parts, axis=-1) * gn
    return (_silu(gate.astype(F32)) * on).astype(BF16)


def _ret_prompt_body(q_ref, k_ref, v_ref, gate_ref, x_ref, gn_ref, w_ref, inner_ref, qdec_ref, kdec_ref, cdec_ref,
                     out_ref, s_ref, o_ref, *, chunk):
    @pl.when(pl.program_id(1) == 0)
    def _():
        s_ref[...] = jnp.zeros_like(s_ref)

    chunks = [slice(c * chunk, (c + 1) * chunk) for c in range(q_ref.shape[0] // chunk)]
    _ret_units(q_ref, k_ref, v_ref, s_ref, s_ref, o_ref, (inner_ref, qdec_ref, kdec_ref, cdec_ref),
               [(rows, 0) for rows in chunks], lookahead=False)
    for rows in chunks:
        z = _gn_gate(o_ref[rows], gate_ref[rows], gn_ref[0])
        out_ref[rows] = x_ref[rows] + _dot(z, w_ref[0])


def _ret_prompt(q, k, v, gate, x2d, gn_g, w_out, ret_layer, n_seq):
    rows, d = x2d.shape
    seq_rows = rows // n_seq
    tile = min(seq_rows, ROW_TILE)
    chunk = min(tile, RET_CHUNK_PROMPT)
    tps = seq_rows // tile
    tabs = _decay_tables(chunk)
    row_spec = lambda w: pl.BlockSpec((tile, w), lambda b, t: (b * tps + t, 0))
    return pl.pallas_call(
        functools.partial(_ret_prompt_body, chunk=chunk),
        grid=(n_seq, tps),
        in_specs=[row_spec(HK), row_spec(HK), row_spec(HV), row_spec(HV), row_spec(d),
                  _layer_resident(gn_g.shape, ret_layer), _layer_resident(w_out.shape, ret_layer)]
                 + [_resident(t.shape) for t in tabs],
        out_specs=[row_spec(d), pl.BlockSpec((1, RET_HEADS, RET_DK, RET_DV), lambda b, t: (b, 0, 0, 0))],
        out_shape=[jax.ShapeDtypeStruct((rows, d), F32),
                   jax.ShapeDtypeStruct((n_seq, RET_HEADS, RET_DK, RET_DV), F32)],
        scratch_shapes=[pltpu.VMEM((tile, HV), F32)],
        compiler_params=_params(("arbitrary", "arbitrary")),
        name="ret_prompt",
    )(q, k, v, gate, x2d, gn_g, w_out, *tabs)


def _ret_sample_body(q_ref, k_ref, v_ref, s0_ref, inner_ref, qdec_ref, kdec_ref, cdec_ref, o_ref, s_ref, *, seq_rows):
    _ret_units(q_ref, k_ref, v_ref, s0_ref, s_ref, o_ref, (inner_ref, qdec_ref, kdec_ref, cdec_ref),
               [(slice(i * seq_rows, (i + 1) * seq_rows), i) for i in range(s0_ref.shape[0])], lookahead=True)


def _ret_sample(q, k, v, states, ret_layer, n_seq, seq_rows):
    bb = 4 if n_seq % 4 == 0 else 1
    steps = n_seq // bb
    tabs = _decay_tables(seq_rows)
    row_spec = lambda w: pl.BlockSpec((bb * seq_rows, w), lambda i: (i, 0))
    st_block = (bb, RET_HEADS, RET_DK, RET_DV)
    return pl.pallas_call(
        functools.partial(_ret_sample_body, seq_rows=seq_rows),
        grid=(steps,),
        in_specs=[row_spec(HK), row_spec(HK), row_spec(HV),
                  pl.BlockSpec(st_block, lambda i: (ret_layer * steps + i, 0, 0, 0))] + [_resident(t.shape) for t in tabs],
        out_specs=[row_spec(HV), pl.BlockSpec(st_block, lambda i: (i, 0, 0, 0))],
        out_shape=[jax.ShapeDtypeStruct((n_seq * seq_rows, HV), F32),
                   jax.ShapeDtypeStruct((n_seq, RET_HEADS, RET_DK, RET_DV), F32)],
        compiler_params=_params(("arbitrary",)),
        name="ret_sample",
    )(q, k, v, states, *tabs)


def _retout_body(o_ref, gate_ref, x_ref, gn_ref, w_ref, out_ref):
    out_ref[...] = x_ref[...] + _dot(_gn_gate(o_ref[...], gate_ref[...], gn_ref[0]), w_ref[0])


def _ret_out(o, gate, x2d, gn_g, w_out, ret_layer):
    rows, d = x2d.shape
    tile = min(rows, ROW_TILE)
    row_spec = lambda w: pl.BlockSpec((tile, w), lambda r: (r, 0))
    return pl.pallas_call(
        _retout_body,
        grid=(rows // tile,),
        in_specs=[row_spec(HV), row_spec(HV), row_spec(d), _layer_resident(gn_g.shape, ret_layer),
                  _layer_resident(w_out.shape, ret_layer)],
        out_specs=row_spec(d),
        out_shape=jax.ShapeDtypeStruct((rows, d), F32),
        compiler_params=_params(("arbitrary",)),
        name="ret_out",
    )(o, gate, x2d, gn_g, w_out)


def _softmax(s):
    e = jnp.exp(s - jnp.max(s, axis=-1, keepdims=True))
    return e / jnp.sum(e, axis=-1, keepdims=True)


def _xattn_block(x_ref, g_ref, wq_ref, wo_ref, keys, values, out_ref, att_ref, *s_refs, n_seq, seq_rows):
    head_cols = [slice(hd * X_HEAD_DIM, (hd + 1) * X_HEAD_DIM) for hd in range(X_HEADS)]
    h = _rms(x_ref[...], g_ref[0]).astype(BF16)
    q = _dot(h, wq_ref[...]) * (X_HEAD_DIM ** -0.5)
    if s_refs:
        s_ref, = s_refs
        pairs = [(slice(i * seq_rows, (i + 1) * seq_rows), cols, i, hd)
                 for i in range(n_seq) for hd, cols in enumerate(head_cols)]
        for n, (rows, cols, i, hd) in enumerate(pairs):
            s_ref[n * seq_rows:(n + 1) * seq_rows] = _dot_nt(q[rows, cols].astype(BF16), keys(i, hd))
        s_ref[...] = _softmax(s_ref[...])
        for n, (rows, cols, i, hd) in enumerate(pairs):
            att_ref[rows, cols] = _dot(s_ref[n * seq_rows:(n + 1) * seq_rows].astype(BF16), values(i, hd))
    else:
        scores = lambda hd: _dot_nt(q[:, head_cols[hd]].astype(BF16), keys(0, hd))
        nxt = scores(0)
        for hd, cols in enumerate(head_cols):
            s = nxt
            if hd + 1 < X_HEADS:
                nxt = scores(hd + 1)
            att_ref[:, cols] = _dot(_softmax(s).astype(BF16), values(0, hd))
    out_ref[...] = x_ref[...] + _dot(att_ref[...].astype(BF16), wo_ref[...])


def _xattn_body(xl_ref, xs_ref, g_ref, wq_ref, wo_ref, kl_ref, vl_ref, ks_ref, vs_ref, outl_ref, outs_ref,
                attl_ref, atts_ref, score_ref, wqb_ref, wob_ref, kvl_ref, *,
                long_steps, short_steps, tiles_per_seq, short_seqs, short_rows):
    step = pl.program_id(0)
    idx = step // 2
    n_mem = kl_ref.shape[1] // KV_ROW_STRIDE

    @pl.when(step == 0)
    def _():
        wqb_ref[...] = wq_ref[0].astype(BF16)
        wob_ref[...] = wo_ref[0].astype(BF16)

    @pl.when((step % 2 == 0) & (idx < long_steps))
    def _():
        @pl.when(idx % tiles_per_seq == 0)
        def _():
            for hd in range(X_HEADS):
                kvl_ref[hd] = _head_kv(kl_ref, 0, hd, n_mem)
                kvl_ref[X_HEADS + hd] = _head_kv(vl_ref, 0, hd, n_mem)

        _xattn_block(xl_ref, g_ref, wqb_ref, wob_ref, lambda i, hd: kvl_ref[hd], lambda i, hd: kvl_ref[X_HEADS + hd],
                     outl_ref, attl_ref, n_seq=1, seq_rows=xl_ref.shape[0])

    @pl.when((step % 2 == 1) & (idx < short_steps))
    def _():
        _xattn_block(xs_ref, g_ref, wqb_ref, wob_ref, lambda i, hd: _head_kv(ks_ref, i, hd, n_mem),
                     lambda i, hd: _head_kv(vs_ref, i, hd, n_mem), outs_ref, atts_ref, score_ref,
                     n_seq=short_seqs, seq_rows=short_rows)


def _xattn(x_long, x_short, gains, wq, wo, layer, kv_long, kv_short, n_long, n_short):
    d = x_long.shape[1]
    long_rows, short_rows = x_long.shape[0] // n_long, x_short.shape[0] // n_short
    assert long_rows % ROW_TILE == 0
    n_mem = kv_long[0].shape[1]
    tps = long_rows // ROW_TILE
    long_steps = n_long * tps
    bb = 4 if n_short % 4 == 0 else 1
    short_steps = n_short // bb
    li = lambda s: jnp.minimum(s // 2, long_steps - 1)
    si = lambda s: jnp.minimum(jnp.maximum(s - 1, 0) // 2, short_steps - 1)
    long_spec = pl.BlockSpec((ROW_TILE, d), lambda s: (li(s), 0))
    short_spec = pl.BlockSpec((bb * short_rows, d), lambda s: (si(s), 0))
    kvl_spec = pl.BlockSpec((1, n_mem, LANES), lambda s: (layer * n_long + li(s) // tps, 0, 0))
    kvs_spec = pl.BlockSpec((bb, n_mem, LANES), lambda s: (layer * short_steps + si(s), 0, 0))
    mem_tokens = n_mem // KV_ROW_STRIDE
    return pl.pallas_call(
        functools.partial(_xattn_body, long_steps=long_steps, short_steps=short_steps, tiles_per_seq=tps,
                          short_seqs=bb, short_rows=short_rows),
        grid=(2 * max(long_steps, short_steps),),
        in_specs=[long_spec, short_spec, _layer_resident(gains.shape, layer), _layer_resident(wq.shape, layer),
                  _layer_resident(wo.shape, layer), kvl_spec, kvl_spec, kvs_spec, kvs_spec],
        out_specs=[long_spec, short_spec],
        out_shape=[jax.ShapeDtypeStruct(x_long.shape, F32), jax.ShapeDtypeStruct(x_short.shape, F32)],
        scratch_shapes=[pltpu.VMEM((ROW_TILE, d), F32), pltpu.VMEM((bb * short_rows, d), F32),
                        pltpu.VMEM((bb * X_HEADS * short_rows, mem_tokens), F32),
                        pltpu.VMEM((d, d), BF16), pltpu.VMEM((d, d), BF16),
                        pltpu.VMEM((2 * X_HEADS, mem_tokens, X_HEAD_DIM), BF16)],
        compiler_params=_params(("arbitrary",)),
        name="xattn",
    )(x_long, x_short, gains, wq, wo, *kv_long, *kv_short)


def _conv_chunk(u, prev, cwb, shift):
    c = cwb[3:4] + cwb[0:1] * shift(u, prev, 2)
    c = c + cwb[1:2] * shift(u, prev, 1)
    return c + cwb[2:3] * u


def _ffn_gate(ua, ug, prev_a, prev_g, cwb_a, cwb_g, shift):
    return (_conv_chunk(ua, prev_a, cwb_a, shift) * _silu(_conv_chunk(ug, prev_g, cwb_g, shift))).astype(BF16)


def _ffn_chunk(h, wa, wg, wd, prev_a, prev_g, cwb_a, cwb_g, shift):
    ua = _dot(h, wa)
    ug = _dot(h, wg)
    return _dot(_ffn_gate(ua, ug, prev_a, prev_g, cwb_a, cwb_g, shift), wd), ua, ug


def _ffn_prompt_body(x_ref, g_ref, gf_ref, wu_ref, wd_ref, cwb_ref, out_ref, c_ref,
                     h_ref, acc_ref, ua0_ref, ug0_ref, ua1_ref, ug1_ref, z0_ref, z1_ref, *, final_norm):
    halo = (CONV_W - 1) * SUBLANES
    tile = acc_ref.shape[0]
    groups = tile // SUBLANES

    @pl.when(pl.program_id(1) == 0)
    def _():
        c_ref[...] = jnp.zeros_like(c_ref)

    x = jnp.swapaxes(x_ref[...].reshape(SUBLANES, groups, D_MODEL), 0, 1).reshape(tile, D_MODEL)
    h_ref[...] = _rms(x, g_ref[0]).astype(BF16)
    acc_ref[...] = x
    slots = ((ua0_ref, ug0_ref), (ua1_ref, ug1_ref))
    starts = range(0, D_FF, FFN_PROMPT_CHUNK)
    n_chunks = len(starts)

    def cols(f):
        lo, hi = starts[f], min(starts[f] + FFN_PROMPT_CHUNK, D_FF)
        return slice(lo, hi), slice(D_FF + lo, D_FF + hi), hi - lo

    def up(f):
        ca, cg, width = cols(f)
        first_sublane = lax.broadcasted_iota(jnp.int32, (SUBLANES, width), 0) == 0
        for u_ref, cc in zip(slots[f % 2], (ca, cg)):
            u = _dot(h_ref[...], wu_ref[0, :, cc])
            u_ref[halo:, :width] = u
            for m in range(CONV_W - 1):
                rows = slice(m * SUBLANES, (m + 1) * SUBLANES)
                src = u[tile - halo + m * SUBLANES: tile - halo + (m + 1) * SUBLANES]
                u_ref[rows, :width] = jnp.where(first_sublane, pltpu.roll(c_ref[0, rows, cc], 1, axis=0),
                                                pltpu.roll(src, 1, axis=0))
                c_ref[0, rows, cc] = src

    def conv(u_ref, cw, r0, n, width):
        c = cw[3:4] + cw[0:1] * u_ref[r0: r0 + n, :width]
        c = c + cw[1:2] * u_ref[r0 + SUBLANES: r0 + SUBLANES + n, :width]
        return c + cw[2:3] * u_ref[r0 + halo: r0 + halo + n, :width]

    def gate(f):
        ca, cg, width = cols(f)
        ua_ref, ug_ref = slots[f % 2]
        cwa, cwg = cwb_ref[0, :, ca], cwb_ref[0, :, cg]
        z_ref = (z0_ref, z1_ref)[f % 2]
        for r0 in range(0, tile, FFN_GATE_ROWS):
            z_ref[r0: r0 + FFN_GATE_ROWS, :width] = (
                conv(ua_ref, cwa, r0, FFN_GATE_ROWS, width)
                * _silu(conv(ug_ref, cwg, r0, FFN_GATE_ROWS, width))).astype(BF16)
        return z_ref[:, :width]

    up(0)
    for f in range(n_chunks):
        if f + 1 < n_chunks:
            up(f + 1)
        acc_ref[...] += _dot(gate(f), wd_ref[0, cols(f)[0], :])
    y = acc_ref[...]
    if final_norm:
        y = _rms(y, gf_ref[...])
    out_ref[...] = jnp.swapaxes(y.reshape(groups, SUBLANES, D_MODEL), 0, 1).reshape(tile, D_MODEL)


def _ffn_prompt(x2d, gains, g_final, w_up, w_down, cwb, layer, n_seq, final_norm):
    rows, d = x2d.shape
    seq_rows = rows // n_seq
    tile = min(seq_rows, FFN_ROW_TILE)
    tps = seq_rows // tile
    halo = (CONV_W - 1) * SUBLANES
    row_spec = pl.BlockSpec((tile, d), lambda b, t: (b * tps + t, 0))
    u_scratch = pltpu.VMEM((halo + tile, FFN_PROMPT_CHUNK), F32)
    z_scratch = pltpu.VMEM((tile, FFN_PROMPT_CHUNK), BF16)
    y, carry = pl.pallas_call(
        functools.partial(_ffn_prompt_body, final_norm=final_norm),
        grid=(n_seq, tps),
        in_specs=[row_spec, _layer_resident(gains.shape, layer), _resident((1, d)),
                  _layer_resident(w_up.shape, layer), _layer_resident(w_down.shape, layer),
                  _layer_resident(cwb.shape, layer)],
        out_specs=[row_spec, pl.BlockSpec((1, halo, 2 * D_FF), lambda b, t: (b, 0, 0))],
        out_shape=[jax.ShapeDtypeStruct((rows, d), F32), jax.ShapeDtypeStruct((n_seq, halo, 2 * D_FF), F32)],
        scratch_shapes=[pltpu.VMEM((tile, d), BF16), pltpu.VMEM((tile, d), F32),
                        u_scratch, u_scratch, u_scratch, u_scratch, z_scratch, z_scratch],
        compiler_params=_params(("arbitrary", "arbitrary")),
        name="ffn_prompt",
    )(x2d, gains, g_final.reshape(1, d), w_up, w_down, cwb)
    return y, carry[:, SUBLANES - 1::SUBLANES]


def _ffn_sample_body(x_ref, g_ref, gf_ref, wa_ref, wg_ref, wd_ref, cwa_ref, cwg_ref, ba_ref, bg_ref,
                     out_ref, ca_ref, cg_ref, h_ref, acc_ref, *, n_seq, seq_rows, final_norm):
    f = pl.program_id(0)

    @pl.when(f == 0)
    def _():
        xt = _to_time_major(x_ref[...], n_seq, seq_rows)
        h_ref[...] = _rms(xt, g_ref[0]).astype(BF16)
        acc_ref[...] = xt

    rows = n_seq * seq_rows
    hist = (CONV_W - 1) * n_seq

    def shift(u, prev, j):
        return jnp.concatenate([prev, u], axis=0)[hist - j * n_seq: hist - j * n_seq + rows]

    y, ua, ug = _ffn_chunk(h_ref[...], wa_ref[0], wg_ref[0], wd_ref[0], ba_ref[...], bg_ref[...],
                           cwa_ref[0], cwg_ref[0], shift)
    acc_ref[...] += y
    ca_ref[...] = ua[rows - hist:]
    cg_ref[...] = ug[rows - hist:]

    @pl.when(f == pl.num_programs(0) - 1)
    def _():
        yt = acc_ref[...]
        if final_norm:
            yt = _rms(yt, gf_ref[...])
        out_ref[...] = _from_time_major(yt, n_seq, seq_rows)


def _ffn_sample(x2d, gains, g_final, w_up, w_down, cwb, layer, buf, n_seq, final_norm):
    rows, d = x2d.shape
    seq_rows = rows // n_seq
    hist = (CONV_W - 1) * n_seq
    a_col = lambda f: f
    g_col = lambda f: NF + f
    c_shape = jax.ShapeDtypeStruct((hist, D_FF), F32)
    c_spec = pl.BlockSpec((hist, FF_CHUNK), lambda f: (0, f))
    return pl.pallas_call(
        functools.partial(_ffn_sample_body, n_seq=n_seq, seq_rows=seq_rows, final_norm=final_norm),
        grid=(NF,),
        in_specs=[_resident((rows, d)), _layer_resident(gains.shape, layer), _resident((1, d)),
                  pl.BlockSpec((1, d, FF_CHUNK), lambda f: (layer, 0, a_col(f))),
                  pl.BlockSpec((1, d, FF_CHUNK), lambda f: (layer, 0, g_col(f))),
                  pl.BlockSpec((1, FF_CHUNK, d), lambda f: (layer, f, 0)),
                  pl.BlockSpec((1, 4, FF_CHUNK), lambda f: (layer, 0, a_col(f))),
                  pl.BlockSpec((1, 4, FF_CHUNK), lambda f: (layer, 0, g_col(f))),
                  pl.BlockSpec((hist, FF_CHUNK), lambda f: (0, a_col(f))),
                  pl.BlockSpec((hist, FF_CHUNK), lambda f: (0, g_col(f)))],
        out_specs=[pl.BlockSpec((rows, d), lambda f: (0, 0)), c_spec, c_spec],
        out_shape=[jax.ShapeDtypeStruct((rows, d), F32), c_shape, c_shape],
        scratch_shapes=[pltpu.VMEM((rows, d), BF16), pltpu.VMEM((rows, d), F32)],
        compiler_params=_params(("arbitrary",)),
        name="ffn_sample",
    )(x2d, gains, g_final.reshape(1, d), w_up, w_up, w_down, cwb, cwb, buf, buf)


def _pool_mix(pooled, w_ref, scale_ref, x):
    mixed = [_dot(pooled[g].astype(BF16), w_ref[0, g]) for g in range(len(POOL_WINDOWS))]
    return x + jnp.concatenate(mixed, axis=-1) * scale_ref[0]


def _pool_prompt_body(x_ref, g_ref, w_ref, scale_ref, out_ref, hist_ref):
    t = pl.program_id(1)

    @pl.when(t == 0)
    def _():
        hist_ref[...] = jnp.zeros_like(hist_ref)

    x = x_ref[...]
    tile = x.shape[0]
    h = _rms(x, g_ref[0])
    ext = jnp.concatenate([hist_ref[0], h], axis=0)
    hist_ref[0] = h[tile - POOL_CARRY:]
    pos1 = (t * tile + 1 + lax.broadcasted_iota(jnp.int32, (tile, POOL_GC), 0)).astype(F32)
    pooled = []
    s = ext
    for g, w in enumerate(POOL_WINDOWS):
        s = s[:, POOL_GC * (1 if g else 0):]
        s = s + pltpu.roll(s, w // 2, axis=0)
        cnt = jnp.minimum(pos1, float(w))
        pooled.append(s[POOL_CARRY:, :POOL_GC] / cnt - h[:, g * POOL_GC:(g + 1) * POOL_GC])
    out_ref[...] = _pool_mix(pooled, w_ref, scale_ref, x)


def _pool_prompt(x2d, gains, layer, pool_w, pool_scale, pool_layer, n_seq):
    rows, d = x2d.shape
    seq_rows = rows // n_seq
    tile = min(seq_rows, WIDE_ROW_TILE)
    tps = seq_rows // tile
    row_spec = pl.BlockSpec((tile, d), lambda b, t: (b * tps + t, 0))
    return pl.pallas_call(
        _pool_prompt_body,
        grid=(n_seq, tps),
        in_specs=[row_spec, _layer_resident(gains.shape, layer), _layer_resident(pool_w.shape, pool_layer),
                  _layer_resident(pool_scale.shape, pool_layer)],
        out_specs=[row_spec, pl.BlockSpec((1, POOL_CARRY, d), lambda b, t: (b, 0, 0))],
        out_shape=[jax.ShapeDtypeStruct((rows, d), F32), jax.ShapeDtypeStruct((n_seq, POOL_CARRY, d), F32)],
        compiler_params=_params(("arbitrary", "arbitrary")),
        name="pool_prompt",
    )(x2d, gains, pool_w, pool_scale)


def _pool_sample_body(x_ref, g_ref, w_ref, scale_ref, buf_ref, out_ref, h_ref, *, n_seq, seq_rows, pos0):
    xt = _to_time_major(x_ref[...], n_seq, seq_rows)
    h = _rms(xt, g_ref[0])
    s = jnp.concatenate([buf_ref[...], h], axis=0)
    rows = n_seq * seq_rows
    first = POOL_BUF
    pooled = []
    for g, w in enumerate(POOL_WINDOWS):
        s = s[:, POOL_GC * (1 if g else 0):]
        step = (w // 2) * n_seq
        s = s[step:] + s[:-step]
        first -= w // 2
        win = s[first * n_seq: first * n_seq + rows, :POOL_GC]
        inv = [1.0 / min(pos0 + t + 1, w) for t in range(seq_rows)]
        if len(set(inv)) == 1:
            win = win * inv[0]
        else:
            win = jnp.concatenate([win[t * n_seq:(t + 1) * n_seq] * inv[t] for t in range(seq_rows)], axis=0)
        pooled.append(win - h[:, g * POOL_GC:(g + 1) * POOL_GC])
    out_ref[...] = _from_time_major(_pool_mix(pooled, w_ref, scale_ref, xt), n_seq, seq_rows)
    h_ref[...] = _from_time_major(h, n_seq, seq_rows)


def _pool_sample(x2d, gains, layer, pool_w, pool_scale, pool_layer, buf_t, n_seq, pos0):
    rows, d = x2d.shape
    seq_rows = rows // n_seq
    full = lambda s: pl.BlockSpec(s, lambda i: (0,) * len(s))
    return pl.pallas_call(
        functools.partial(_pool_sample_body, n_seq=n_seq, seq_rows=seq_rows, pos0=pos0),
        grid=(1,),
        in_specs=[full((rows, d)), _layer_resident(gains.shape, layer), _layer_resident(pool_w.shape, pool_layer),
                  _layer_resident(pool_scale.shape, pool_layer), full(buf_t.shape)],
        out_specs=[full((rows, d)), full((rows, d))],
        out_shape=[jax.ShapeDtypeStruct((rows, d), F32)] * 2,
        compiler_params=_params(("arbitrary",)),
        name="pool_sample",
    )(x2d, gains, pool_w, pool_scale, buf_t)


def _rope_tables(pos):
    inv = 1.0 / (ROPE_BASE ** (jnp.arange(0, RET_DK, 2, dtype=F32) / RET_DK))
    ang = pos[:, None] * inv[None, :]
    return jnp.cos(ang), jnp.sin(ang)


def kernel(x_prompt, x_sample, mem_prompt, cache_mem_k, cache_mem_v, state_ret, cache_pool, cache_ffn_conv, w_ret_in, ret_gn, w_ret_out, pool_w, pool_scale, norm_mem, w_xq, w_xk, w_xv, w_xo, w_up, conv_w, conv_b, w_down, norm_mix, norm_xattn, norm_ffn, norm_final):
    bp, lp, d = x_prompt.shape
    bs, ls, _ = x_sample.shape
    depth = w_up.shape[0]
    n_mem = mem_prompt.shape[1]

    w_in_b, w_out_b, pool_w_b, w_up_b, w_down_b = (
        w.astype(BF16) for w in (w_ret_in, w_ret_out, pool_w, w_up, w_down))
    cwb = jnp.concatenate([conv_w, conv_b[:, None, :]], axis=1)
    g_mix, g_x, g_ffn = _gain(norm_mix), _gain(norm_xattn), _gain(norm_ffn)
    gn_g, p_scale = _gain(ret_gn), _gain(pool_scale)

    mem_k2, mem_v2 = _mem_kv(mem_prompt.reshape(bp * n_mem, d), norm_mem, w_xk, w_xv)
    kv_p = tuple(a.reshape(depth * bp, n_mem * KV_ROW_STRIDE, LANES) for a in (mem_k2, mem_v2))
    kv_s = (_kv_rows(cache_mem_k), _kv_rows(cache_mem_v))
    states = state_ret.reshape((state_ret.shape[0] * bs,) + state_ret.shape[2:])

    cos_p, sin_p = _rope_tables(jnp.arange(lp, dtype=F32))
    cos_s, sin_s = _rope_tables(PAST_LEN + jnp.arange(ls, dtype=F32))
    cos_s, sin_s = jnp.tile(cos_s, (bs, 1)), jnp.tile(sin_s, (bs, 1))

    xp = x_prompt.reshape(bp * lp, d)
    xs = x_sample.reshape(bs * ls, d)
    ret_p, ret_s, pool_p, pool_s, conv_p, conv_s = [], [], [], [], [], []
    for i in range(depth):
        j = i // 2
        last = i == depth - 1
        if i % 2 == 0:
            q, k, v, gate = _ret_proj(xp, g_mix, i, cos_p, sin_p, w_in_b, j, bp, BF16)
            xp, s = _ret_prompt(q, k, v, gate, xp, gn_g, w_out_b, j, bp)
            ret_p.append(s)
            q, k, v, gate = _ret_proj(xs, g_mix, i, cos_s, sin_s, w_in_b, j, 1, F32)
            o, s = _ret_sample(q, k, v, states, j, bs, ls)
            xs = _ret_out(o, gate, xs, gn_g, w_out_b, j)
            ret_s.append(s)
        else:
            xp, hist = _pool_prompt(xp, g_mix, i, pool_w_b, p_scale, j, bp)
            pool_p.append(hist[:, POOL_CARRY - POOL_BUF:])
            buf_t = jnp.swapaxes(cache_pool[j], 0, 1).reshape(POOL_BUF * bs, d)
            xs, hs = _pool_sample(xs, g_mix, i, pool_w_b, p_scale, j, buf_t, bs, PAST_LEN)
            pool_s.append(jnp.concatenate([cache_pool[j], hs.reshape(bs, ls, d)], axis=1)[:, -POOL_BUF:])

        xp, xs = _xattn(xp, xs, g_x, w_xq, w_xo, i, kv_p, kv_s, bp, bs)

        xp, tail = _ffn_prompt(xp, g_ffn, norm_final, w_up_b, w_down_b, cwb, i, bp, last)
        conv_p.append(tail)
        buf = jnp.swapaxes(cache_ffn_conv[i], 0, 1).reshape((CONV_W - 1) * bs, 2 * D_FF)
        xs, ca, cg = _ffn_sample(xs, g_ffn, norm_final, w_up_b, w_down_b, cwb, i, buf, bs, last)
        tail = jnp.concatenate([ca, cg], axis=-1)
        conv_s.append(jnp.swapaxes(tail.reshape(CONV_W - 1, bs, 2 * D_FF), 0, 1))

    mk = _kv_unrows(mem_k2, (depth, bp), n_mem)
    mv = _kv_unrows(mem_v2, (depth, bp), n_mem)
    return (xp.reshape(bp, lp, d), xs.reshape(bs, ls, d), jnp.stack(ret_p), jnp.stack(ret_s).astype(state_ret.dtype),
            jnp.stack(pool_p), jnp.stack(pool_s), jnp.stack(conv_p), jnp.stack(conv_s), mk, mv)
```

```python
import functools

import jax
import jax.numpy as jnp
from jax import lax
from jax.experimental import pallas as pl
from jax.experimental.pallas import tpu as pltpu

F32 = jnp.float32
BF16 = jnp.bfloat16

D_MODEL = 1024
PAST_LEN = 16384
RET_HEADS = 4
RET_DK = D_MODEL // RET_HEADS
RET_DV = 2 * D_MODEL // RET_HEADS
HK = RET_HEADS * RET_DK
HV = RET_HEADS * RET_DV
ROPE_BASE = 10000.0
POOL_WINDOWS = (2, 4, 8, 16)
POOL_GC = D_MODEL // len(POOL_WINDOWS)
POOL_BUF = max(POOL_WINDOWS) - 1
X_HEADS = 4
X_HEAD_DIM = D_MODEL // X_HEADS
D_FF = 2816
CONV_W = 3
NORM_EPS = 1e-6
GN_EPS = 1e-5

SUBLANES = 8
LANES = 128
KV_LANE_TILES = X_HEAD_DIM // LANES
KV_ROW_STRIDE = X_HEADS * KV_LANE_TILES
assert KV_ROW_STRIDE == SUBLANES
ROW_TILE = 512
FFN_ROW_TILE = 512
WIDE_ROW_TILE = 1024
RET_CHUNK_PROMPT = 256
FF_CHUNK = 1408
NF = D_FF // FF_CHUNK
FFN_PROMPT_CHUNK = 1536
FFN_GATE_ROWS = 64
POOL_CARRY = 16
VMEM_LIMIT = 56 * 1024 * 1024


def _params(sem):
    return pltpu.CompilerParams(dimension_semantics=sem, vmem_limit_bytes=VMEM_LIMIT)


def _resident(shape):
    zeros = (0,) * len(shape)
    return pl.BlockSpec(shape, lambda *_: zeros, pipeline_mode=pl.Buffered(1))


def _layer_resident(stacked_shape, layer):
    idx = (layer,) + (0,) * (len(stacked_shape) - 1)
    return pl.BlockSpec((1,) + tuple(stacked_shape[1:]), lambda *_: idx, pipeline_mode=pl.Buffered(1))


def _gain(g):
    return g.reshape(g.shape[0], 1, g.shape[1])


def _dot(a, b):
    return jnp.dot(a, b, preferred_element_type=F32)


def _dot_nt(a, b):
    return lax.dot_general(a, b, (((1,), (1,)), ((), ())), preferred_element_type=F32)


def _dot_tn(a, b):
    return lax.dot_general(a, b, (((0,), (0,)), ((), ())), preferred_element_type=F32)


def _rms(x, g):
    return x * lax.rsqrt(jnp.mean(x * x, axis=-1, keepdims=True) + NORM_EPS) * g


def _silu(x):
    return x * (1.0 / (1.0 + jnp.exp(-x)))


def _to_time_major(x, n_seq, seq_rows):
    return jnp.swapaxes(x.reshape(n_seq, seq_rows, x.shape[-1]), 0, 1).reshape(x.shape)


def _from_time_major(x, n_seq, seq_rows):
    return jnp.swapaxes(x.reshape(seq_rows, n_seq, x.shape[-1]), 0, 1).reshape(x.shape)


def _memkv_body(mem_ref, g_ref, wk_ref, wv_ref, k_ref, v_ref):
    h = _rms(mem_ref[...], g_ref[0]).astype(BF16)
    for w_ref, o_ref in ((wk_ref, k_ref), (wv_ref, v_ref)):
        y = _dot(h, w_ref[0].astype(BF16))
        for hd in range(X_HEADS):
            for c in range(KV_LANE_TILES):
                col = hd * X_HEAD_DIM + c * LANES
                o_ref[0, pl.ds(c * X_HEADS + hd, y.shape[0], stride=KV_ROW_STRIDE), :] = y[:, col:col + LANES]


def _mem_kv(mem2d, norm_mem, wk, wv):
    rows, d = mem2d.shape
    depth = wk.shape[0]
    tile = min(rows, 1024)
    w_spec = pl.BlockSpec((1, d, d), lambda i, r: (i, 0, 0))
    o_spec = pl.BlockSpec((1, tile * KV_ROW_STRIDE, LANES), lambda i, r: (i, r, 0))
    return pl.pallas_call(
        _memkv_body,
        grid=(depth, rows // tile),
        in_specs=[pl.BlockSpec((tile, d), lambda i, r: (r, 0)),
                  pl.BlockSpec((1, 1, d), lambda i, r: (i, 0, 0)), w_spec, w_spec],
        out_specs=[o_spec, o_spec],
        out_shape=[jax.ShapeDtypeStruct((depth, rows * KV_ROW_STRIDE, LANES), F32)] * 2,
        compiler_params=_params(("arbitrary", "arbitrary")),
        name="mem_kv",
    )(mem2d, _gain(norm_mem), wk, wv)


def _kv_rows(a):
    lead, n_mem = a.shape[:-3], a.shape[-3]
    a = a.reshape(lead + (n_mem, X_HEADS, KV_LANE_TILES, LANES))
    a = jnp.swapaxes(a, -3, -2)
    return a.reshape((-1, n_mem * KV_ROW_STRIDE, LANES))


def _kv_unrows(a, lead, n_mem):
    a = a.reshape(lead + (n_mem, KV_LANE_TILES, X_HEADS, LANES))
    a = jnp.swapaxes(a, -3, -2)
    return a.reshape(lead + (n_mem, X_HEADS, X_HEAD_DIM))


def _head_kv(ref, i, hd, n_mem):
    parts = [ref[i, pl.ds(c * X_HEADS + hd, n_mem, stride=KV_ROW_STRIDE), :] for c in range(KV_LANE_TILES)]
    return jnp.concatenate(parts, axis=1).astype(BF16)


def _retproj_body(x_ref, g_ref, cos_ref, sin_ref, w_ref, q_ref, k_ref, v_ref, gate_ref):
    h = _rms(x_ref[...], g_ref[0]).astype(BF16)
    cos = cos_ref[...]
    sin = sin_ref[...]
    half = RET_DK // 2

    def rope_store(col0, out_ref, scale):
        p = _dot(h, w_ref[0, :, col0:col0 + HK])
        for hd in range(RET_HEADS):
            a = p[:, hd * RET_DK: hd * RET_DK + half]
            b = p[:, hd * RET_DK + half: (hd + 1) * RET_DK]
            out_ref[:, hd * RET_DK: hd * RET_DK + half] = ((a * cos - b * sin) * scale).astype(out_ref.dtype)
            out_ref[:, hd * RET_DK + half: (hd + 1) * RET_DK] = ((b * cos + a * sin) * scale).astype(out_ref.dtype)

    rope_store(0, q_ref, 1.0)
    rope_store(HK, k_ref, RET_DK ** -0.5)
    v_ref[...] = _dot(h, w_ref[0, :, 2 * HK: 2 * HK + HV]).astype(v_ref.dtype)
    gate_ref[...] = _dot(h, w_ref[0, :, 2 * HK + HV:]).astype(gate_ref.dtype)


def _ret_proj(x2d, gains, layer, cos, sin, w_in, ret_layer, n_seq, out_dtype):
    rows, d = x2d.shape
    seq_rows = rows // n_seq
    tile = min(seq_rows, WIDE_ROW_TILE)
    tps = seq_rows // tile
    row_spec = lambda w: pl.BlockSpec((tile, w), lambda b, t: (b * tps + t, 0))
    tab_spec = pl.BlockSpec((tile, RET_DK // 2), lambda b, t: (t, 0))
    return pl.pallas_call(
        _retproj_body,
        grid=(n_seq, tps),
        in_specs=[row_spec(d), _layer_resident(gains.shape, layer), tab_spec, tab_spec,
                  _layer_resident(w_in.shape, ret_layer)],
        out_specs=[row_spec(HK), row_spec(HK), row_spec(HV), row_spec(HV)],
        out_shape=[jax.ShapeDtypeStruct((rows, HK), out_dtype), jax.ShapeDtypeStruct((rows, HK), out_dtype),
                   jax.ShapeDtypeStruct((rows, HV), out_dtype), jax.ShapeDtypeStruct((rows, HV), out_dtype)],
        compiler_params=_params(("arbitrary", "arbitrary")),
        name="ret_proj",
    )(x2d, gains, cos, sin, w_in)


def _ret_units(q_ref, k_ref, v_ref, s_in, s_out, o_ref, tables, row_blocks, lookahead):
    inner_ref, qdec_ref, kdec_ref, cdec_ref = tables
    units = [(rows, i, h) for rows, i in row_blocks for h in range(RET_HEADS)]

    def issue(rows, i, h):
        qb = q_ref[rows, h * RET_DK:(h + 1) * RET_DK].astype(BF16)
        kb = k_ref[rows, h * RET_DK:(h + 1) * RET_DK].astype(BF16)
        return _dot_nt(qb, kb), _dot(qb, s_in[i, h].astype(BF16))

    def finish(rows, i, h, qk, qs):
        kh = k_ref[rows, h * RET_DK:(h + 1) * RET_DK]
        vb = v_ref[rows, h * RET_DV:(h + 1) * RET_DV].astype(BF16)
        p = (qk * inner_ref[h]).astype(BF16)
        o_ref[rows, h * RET_DV:(h + 1) * RET_DV] = _dot(p, vb) + qs * qdec_ref[h]
        kd = (kh.astype(F32) * kdec_ref[h]).astype(BF16)
        s_out[i, h] = s_in[i, h] * cdec_ref[h] + _dot_tn(kd, vb)

    if not lookahead:
        for unit in units:
            finish(*unit, *issue(*unit))
        return
    nxt = issue(*units[0])
    for n, unit in enumerate(units):
        cur = nxt
        if n + 1 < len(units):
            nxt = issue(*units[n + 1])
        finish(*unit, *cur)


def _decay_tables(chunk):
    lg = jnp.log(1.0 - 2.0 ** (-5.0 - jnp.arange(RET_HEADS, dtype=F32)))
    idx = jnp.arange(chunk, dtype=F32)
    rel = idx[:, None] - idx[None, :]
    inner = jnp.where(rel[None] >= 0, jnp.exp(jnp.maximum(rel, 0.0)[None] * lg[:, None, None]), 0.0)
    q_dec = jnp.exp((idx + 1.0)[None, :] * lg[:, None])[..., None]
    k_dec = jnp.exp((chunk - 1.0 - idx)[None, :] * lg[:, None])[..., None]
    c_dec = jnp.exp(chunk * lg)[:, None, None]
    return (inner, jnp.broadcast_to(q_dec, (RET_HEADS, chunk, RET_DV)),
            jnp.broadcast_to(k_dec, (RET_HEADS, chunk, RET_DK)),
            jnp.broadcast_to(c_dec, (RET_HEADS, 1, RET_DV)))


def _gn_gate(o, gate, gn):
    parts = []
    for h in range(RET_HEADS):
        oh = o[:, h * RET_DV:(h + 1) * RET_DV]
        mu = jnp.mean(oh, axis=-1, keepdims=True)
        dlt = oh - mu
        var = jnp.mean(dlt * dlt, axis=-1, keepdims=True)
        parts.append(dlt * lax.rsqrt(var + GN_EPS))
    on = jnp.concatenate(parts, axis=-1) * gn
    return (_silu(gate.astype(F32)) * on).astype(BF16)


def _ret_prompt_body(q_ref, k_ref, v_ref, gate_ref, x_ref, gn_ref, w_ref, inner_ref, qdec_ref, kdec_ref, cdec_ref,
                     out_ref, s_ref, o_ref, *, chunk):
    @pl.when(pl.program_id(1) == 0)
    def _():
        s_ref[...] = jnp.zeros_like(s_ref)

    chunks = [slice(c * chunk, (c + 1) * chunk) for c in range(q_ref.shape[0] // chunk)]
    _ret_units(q_ref, k_ref, v_ref, s_ref, s_ref, o_ref, (inner_ref, qdec_ref, kdec_ref, cdec_ref),
               [(rows, 0) for rows in chunks], lookahead=False)
    for rows in chunks:
        z = _gn_gate(o_ref[rows], gate_ref[rows], gn_ref[0])
        out_ref[rows] = x_ref[rows] + _dot(z, w_ref[0])


def _ret_prompt(q, k, v, gate, x2d, gn_g, w_out, ret_layer, n_seq):
    rows, d = x2d.shape
    seq_rows = rows // n_seq
    tile = min(seq_rows, ROW_TILE)
    chunk = min(tile, RET_CHUNK_PROMPT)
    tps = seq_rows // tile
    tabs = _decay_tables(chunk)
    row_spec = lambda w: pl.BlockSpec((tile, w), lambda b, t: (b * tps + t, 0))
    return pl.pallas_call(
        functools.partial(_ret_prompt_body, chunk=chunk),
        grid=(n_seq, tps),
        in_specs=[row_spec(HK), row_spec(HK), row_spec(HV), row_spec(HV), row_spec(d),
                  _layer_resident(gn_g.shape, ret_layer), _layer_resident(w_out.shape, ret_layer)]
                 + [_resident(t.shape) for t in tabs],
        out_specs=[row_spec(d), pl.BlockSpec((1, RET_HEADS, RET_DK, RET_DV), lambda b, t: (b, 0, 0, 0))],
        out_shape=[jax.ShapeDtypeStruct((rows, d), F32),
                   jax.ShapeDtypeStruct((n_seq, RET_HEADS, RET_DK, RET_DV), F32)],
        scratch_shapes=[pltpu.VMEM((tile, HV), F32)],
        compiler_params=_params(("arbitrary", "arbitrary")),
        name="ret_prompt",
    )(q, k, v, gate, x2d, gn_g, w_out, *tabs)


def _ret_sample_body(q_ref, k_ref, v_ref, s0_ref, inner_ref, qdec_ref, kdec_ref, cdec_ref, o_ref, s_ref, *, seq_rows):
    _ret_units(q_ref, k_ref, v_ref, s0_ref, s_ref, o_ref, (inner_ref, qdec_ref, kdec_ref, cdec_ref),
               [(slice(i * seq_rows, (i + 1) * seq_rows), i) for i in range(s0_ref.shape[0])], lookahead=True)


def _ret_sample(q, k, v, states, ret_layer, n_seq, seq_rows):
    bb = 4 if n_seq % 4 == 0 else 1
    steps = n_seq // bb
    tabs = _decay_tables(seq_rows)
    row_spec = lambda w: pl.BlockSpec((bb * seq_rows, w), lambda i: (i, 0))
    st_block = (bb, RET_HEADS, RET_DK, RET_DV)
    return pl.pallas_call(
        functools.partial(_ret_sample_body, seq_rows=seq_rows),
        grid=(steps,),
        in_specs=[row_spec(HK), row_spec(HK), row_spec(HV),
                  pl.BlockSpec(st_block, lambda i: (ret_layer * steps + i, 0, 0, 0))] + [_resident(t.shape) for t in tabs],
        out_specs=[row_spec(HV), pl.BlockSpec(st_block, lambda i: (i, 0, 0, 0))],
        out_shape=[jax.ShapeDtypeStruct((n_seq * seq_rows, HV), F32),
                   jax.ShapeDtypeStruct((n_seq, RET_HEADS, RET_DK, RET_DV), F32)],
        compiler_params=_params(("arbitrary",)),
        name="ret_sample",
    )(q, k, v, states, *tabs)


def _retout_body(o_ref, gate_ref, x_ref, gn_ref, w_ref, out_ref):
    out_ref[...] = x_ref[...] + _dot(_gn_gate(o_ref[...], gate_ref[...], gn_ref[0]), w_ref[0])


def _ret_out(o, gate, x2d, gn_g, w_out, ret_layer):
    rows, d = x2d.shape
    tile = min(rows, ROW_TILE)
    row_spec = lambda w: pl.BlockSpec((tile, w), lambda r: (r, 0))
    return pl.pallas_call(
        _retout_body,
        grid=(rows // tile,),
        in_specs=[row_spec(HV), row_spec(HV), row_spec(d), _layer_resident(gn_g.shape, ret_layer),
                  _layer_resident(w_out.shape, ret_layer)],
        out_specs=row_spec(d),
        out_shape=jax.ShapeDtypeStruct((rows, d), F32),
        compiler_params=_params(("arbitrary",)),
        name="ret_out",
    )(o, gate, x2d, gn_g, w_out)


def _softmax(s):
    e = jnp.exp(s - jnp.max(s, axis=-1, keepdims=True))
    return e / jnp.sum(e, axis=-1, keepdims=True)


def _xattn_block(x_ref, g_ref, wq_ref, wo_ref, keys, values, out_ref, att_ref, *s_refs, n_seq, seq_rows):
    head_cols = [slice(hd * X_HEAD_DIM, (hd + 1) * X_HEAD_DIM) for hd in range(X_HEADS)]
    h = _rms(x_ref[...], g_ref[0]).astype(BF16)
    q = _dot(h, wq_ref[...]) * (X_HEAD_DIM ** -0.5)
    if s_refs:
        s_ref, = s_refs
        pairs = [(slice(i * seq_rows, (i + 1) * seq_rows), cols, i, hd)
                 for i in range(n_seq) for hd, cols in enumerate(head_cols)]
        for n, (rows, cols, i, hd) in enumerate(pairs):
            s_ref[n * seq_rows:(n + 1) * seq_rows] = _dot_nt(q[rows, cols].astype(BF16), keys(i, hd))
        s_ref[...] = _softmax(s_ref[...])
        for n, (rows, cols, i, hd) in enumerate(pairs):
            att_ref[rows, cols] = _dot(s_ref[n * seq_rows:(n + 1) * seq_rows].astype(BF16), values(i, hd))
    else:
        scores = lambda hd: _dot_nt(q[:, head_cols[hd]].astype(BF16), keys(0, hd))
        nxt = scores(0)
        for hd, cols in enumerate(head_cols):
            s = nxt
            if hd + 1 < X_HEADS:
                nxt = scores(hd + 1)
            att_ref[:, cols] = _dot(_softmax(s).astype(BF16), values(0, hd))
    out_ref[...] = x_ref[...] + _dot(att_ref[...].astype(BF16), wo_ref[...])


def _xattn_body(xl_ref, xs_ref, g_ref, wq_ref, wo_ref, kl_ref, vl_ref, ks_ref, vs_ref, outl_ref, outs_ref,
                attl_ref, atts_ref, score_ref, wqb_ref, wob_ref, kvl_ref, *,
                long_steps, short_steps, tiles_per_seq, short_seqs, short_rows):
    step = pl.program_id(0)
    idx = step // 2
    n_mem = kl_ref.shape[1] // KV_ROW_STRIDE

    @pl.when(step == 0)
    def _():
        wqb_ref[...] = wq_ref[0].astype(BF16)
        wob_ref[...] = wo_ref[0].astype(BF16)

    @pl.when((step % 2 == 0) & (idx < long_steps))
    def _():
        @pl.when(idx % tiles_per_seq == 0)
        def _():
            for hd in range(X_HEADS):
                kvl_ref[hd] = _head_kv(kl_ref, 0, hd, n_mem)
                kvl_ref[X_HEADS + hd] = _head_kv(vl_ref, 0, hd, n_mem)

        _xattn_block(xl_ref, g_ref, wqb_ref, wob_ref, lambda i, hd: kvl_ref[hd], lambda i, hd: kvl_ref[X_HEADS + hd],
                     outl_ref, attl_ref, n_seq=1, seq_rows=xl_ref.shape[0])

    @pl.when((step % 2 == 1) & (idx < short_steps))
    def _():
        _xattn_block(xs_ref, g_ref, wqb_ref, wob_ref, lambda i, hd: _head_kv(ks_ref, i, hd, n_mem),
                     lambda i, hd: _head_kv(vs_ref, i, hd, n_mem), outs_ref, atts_ref, score_ref,
                     n_seq=short_seqs, seq_rows=short_rows)


def _xattn(x_long, x_short, gains, wq, wo, layer, kv_long, kv_short, n_long, n_short):
    d = x_long.shape[1]
    long_rows, short_rows = x_long.shape[0] // n_long, x_short.shape[0] // n_short
    tile = min(long_rows, WIDE_ROW_TILE)
    assert long_rows % tile == 0
    n_mem = kv_long[0].shape[1]
    tps = long_rows // tile
    long_steps = n_long * tps
    bb = 4 if n_short % 4 == 0 else 1
    short_steps = n_short // bb
    li = lambda s: jnp.minimum(s // 2, long_steps - 1)
    si = lambda s: jnp.minimum(jnp.maximum(s - 1, 0) // 2, short_steps - 1)
    long_spec = pl.BlockSpec((tile, d), lambda s: (li(s), 0))
    short_spec = pl.BlockSpec((bb * short_rows, d), lambda s: (si(s), 0))
    kvl_spec = pl.BlockSpec((1, n_mem, LANES), lambda s: (layer * n_long + li(s) // tps, 0, 0))
    kvs_spec = pl.BlockSpec((bb, n_mem, LANES), lambda s: (layer * short_steps + si(s), 0, 0))
    mem_tokens = n_mem // KV_ROW_STRIDE
    return pl.pallas_call(
        functools.partial(_xattn_body, long_steps=long_steps, short_steps=short_steps, tiles_per_seq=tps,
                          short_seqs=bb, short_rows=short_rows),
        grid=(2 * max(long_steps, short_steps),),
        in_specs=[long_spec, short_spec, _layer_resident(gains.shape, layer), _layer_resident(wq.shape, layer),
                  _layer_resident(wo.shape, layer), kvl_spec, kvl_spec, kvs_spec, kvs_spec],
        out_specs=[long_spec, short_spec],
        out_shape=[jax.ShapeDtypeStruct(x_long.shape, F32), jax.ShapeDtypeStruct(x_short.shape, F32)],
        scratch_shapes=[pltpu.VMEM((tile, d), F32), pltpu.VMEM((bb * short_rows, d), F32),
                        pltpu.VMEM((bb * X_HEADS * short_rows, mem_tokens), F32),
                        pltpu.VMEM((d, d), BF16), pltpu.VMEM((d, d), BF16),
                        pltpu.VMEM((2 * X_HEADS, mem_tokens, X_HEAD_DIM), BF16)],
        compiler_params=_params(("arbitrary",)),
        name="xattn",
    )(x_long, x_short, gains, wq, wo, *kv_long, *kv_short)


def _conv_chunk(u, prev, cwb, shift):
    c = cwb[3:4] + cwb[0:1] * shift(u, prev, 2)
    c = c + cwb[1:2] * shift(u, prev, 1)
    return c + cwb[2:3] * u


def _ffn_gate(ua, ug, prev_a, prev_g, cwb_a, cwb_g, shift):
    return (_conv_chunk(ua, prev_a, cwb_a, shift) * _silu(_conv_chunk(ug, prev_g, cwb_g, shift))).astype(BF16)


def _ffn_chunk(h, wa, wg, wd, prev_a, prev_g, cwb_a, cwb_g, shift):
    ua = _dot(h, wa)
    ug = _dot(h, wg)
    return _dot(_ffn_gate(ua, ug, prev_a, prev_g, cwb_a, cwb_g, shift), wd), ua, ug


def _ffn_prompt_body(x_ref, g_ref, gf_ref, wu_ref, wd_ref, cwb_ref, out_ref, c_ref,
                     h_ref, acc_ref, ua0_ref, ug0_ref, ua1_ref, ug1_ref, z0_ref, z1_ref, *, final_norm):
    halo = (CONV_W - 1) * SUBLANES
    tile = acc_ref.shape[0]
    groups = tile // SUBLANES

    @pl.when(pl.program_id(1) == 0)
    def _():
        c_ref[...] = jnp.zeros_like(c_ref)

    x = jnp.swapaxes(x_ref[...].reshape(SUBLANES, groups, D_MODEL), 0, 1).reshape(tile, D_MODEL)
    h_ref[...] = _rms(x, g_ref[0]).astype(BF16)
    acc_ref[...] = x
    slots = ((ua0_ref, ug0_ref), (ua1_ref, ug1_ref))
    starts = range(0, D_FF, FFN_PROMPT_CHUNK)
    n_chunks = len(starts)

    def cols(f):
        lo, hi = starts[f], min(starts[f] + FFN_PROMPT_CHUNK, D_FF)
        return slice(lo, hi), slice(D_FF + lo, D_FF + hi), hi - lo

    def up(f):
        ca, cg, width = cols(f)
        first_sublane = lax.broadcasted_iota(jnp.int32, (SUBLANES, width), 0) == 0
        for u_ref, cc in zip(slots[f % 2], (ca, cg)):
            u = _dot(h_ref[...], wu_ref[0, :, cc])
            u_ref[halo:, :width] = u
            for m in range(CONV_W - 1):
                rows = slice(m * SUBLANES, (m + 1) * SUBLANES)
                src = u[tile - halo + m * SUBLANES: tile - halo + (m + 1) * SUBLANES]
                u_ref[rows, :width] = jnp.where(first_sublane, pltpu.roll(c_ref[0, rows, cc], 1, axis=0),
                                                pltpu.roll(src, 1, axis=0))
                c_ref[0, rows, cc] = src

    def conv(u_ref, cw, r0, n, width):
        c = cw[3:4] + cw[0:1] * u_ref[r0: r0 + n, :width]
        c = c + cw[1:2] * u_ref[r0 + SUBLANES: r0 + SUBLANES + n, :width]
        return c + cw[2:3] * u_ref[r0 + halo: r0 + halo + n, :width]

    def gate(f):
        ca, cg, width = cols(f)
        ua_ref, ug_ref = slots[f % 2]
        cwa, cwg = cwb_ref[0, :, ca], cwb_ref[0, :, cg]
        z_ref = (z0_ref, z1_ref)[f % 2]
        for r0 in range(0, tile, FFN_GATE_ROWS):
            z_ref[r0: r0 + FFN_GATE_ROWS, :width] = (
                conv(ua_ref, cwa, r0, FFN_GATE_ROWS, width)
                * _silu(conv(ug_ref, cwg, r0, FFN_GATE_ROWS, width))).astype(BF16)
        return z_ref[:, :width]

    up(0)
    for f in range(n_chunks):
        if f + 1 < n_chunks:
            up(f + 1)
        acc_ref[...] += _dot(gate(f), wd_ref[0, cols(f)[0], :])
    y = acc_ref[...]
    if final_norm:
        y = _rms(y, gf_ref[...])
    out_ref[...] = jnp.swapaxes(y.reshape(groups, SUBLANES, D_MODEL), 0, 1).reshape(tile, D_MODEL)


def _ffn_prompt(x2d, gains, g_final, w_up, w_down, cwb, layer, n_seq, final_norm):
    rows, d = x2d.shape
    seq_rows = rows // n_seq
    tile = min(seq_rows, FFN_ROW_TILE)
    tps = seq_rows // tile
    halo = (CONV_W - 1) * SUBLANES
    row_spec = pl.BlockSpec((tile, d), lambda b, t: (b * tps + t, 0))
    u_scratch = pltpu.VMEM((halo + tile, FFN_PROMPT_CHUNK), F32)
    z_scratch = pltpu.VMEM((tile, FFN_PROMPT_CHUNK), BF16)
    y, carry = pl.pallas_call(
        functools.partial(_ffn_prompt_body, final_norm=final_norm),
        grid=(n_seq, tps),
        in_specs=[row_spec, _layer_resident(gains.shape, layer), _resident((1, d)),
                  _layer_resident(w_up.shape, layer), _layer_resident(w_down.shape, layer),
                  _layer_resident(cwb.shape, layer)],
        out_specs=[row_spec, pl.BlockSpec((1, halo, 2 * D_FF), lambda b, t: (b, 0, 0))],
        out_shape=[jax.ShapeDtypeStruct((rows, d), F32), jax.ShapeDtypeStruct((n_seq, halo, 2 * D_FF), F32)],
        scratch_shapes=[pltpu.VMEM((tile, d), BF16), pltpu.VMEM((tile, d), F32),
                        u_scratch, u_scratch, u_scratch, u_scratch, z_scratch, z_scratch],
        compiler_params=_params(("arbitrary", "arbitrary")),
        name="ffn_prompt",
    )(x2d, gains, g_final.reshape(1, d), w_up, w_down, cwb)
    return y, carry[:, SUBLANES - 1::SUBLANES]


def _ffn_sample_body(x_ref, g_ref, gf_ref, wa_ref, wg_ref, wd_ref, cwa_ref, cwg_ref, ba_ref, bg_ref,
                     out_ref, ca_ref, cg_ref, h_ref, acc_ref, *, n_seq, seq_rows, final_norm):
    f = pl.program_id(0)

    @pl.when(f == 0)
    def _():
        xt = _to_time_major(x_ref[...], n_seq, seq_rows)
        h_ref[...] = _rms(xt, g_ref[0]).astype(BF16)
        acc_ref[...] = xt

    rows = n_seq * seq_rows
    hist = (CONV_W - 1) * n_seq

    def shift(u, prev, j):
        return jnp.concatenate([prev, u], axis=0)[hist - j * n_seq: hist - j * n_seq + rows]

    y, ua, ug = _ffn_chunk(h_ref[...], wa_ref[0], wg_ref[0], wd_ref[0], ba_ref[...], bg_ref[...],
                           cwa_ref[0], cwg_ref[0], shift)
    acc_ref[...] += y
    ca_ref[...] = ua[rows - hist:]
    cg_ref[...] = ug[rows - hist:]

    @pl.when(f == pl.num_programs(0) - 1)
    def _():
        yt = acc_ref[...]
        if final_norm:
            yt = _rms(yt, gf_ref[...])
        out_ref[...] = _from_time_major(yt, n_seq, seq_rows)


def _ffn_sample(x2d, gains, g_final, w_up, w_down, cwb, layer, buf, n_seq, final_norm):
    rows, d = x2d.shape
    seq_rows = rows // n_seq
    hist = (CONV_W - 1) * n_seq
    a_col = lambda f: f
    g_col = lambda f: NF + f
    c_shape = jax.ShapeDtypeStruct((hist, D_FF), F32)
    c_spec = pl.BlockSpec((hist, FF_CHUNK), lambda f: (0, f))
    return pl.pallas_call(
        functools.partial(_ffn_sample_body, n_seq=n_seq, seq_rows=seq_rows, final_norm=final_norm),
        grid=(NF,),
        in_specs=[_resident((rows, d)), _layer_resident(gains.shape, layer), _resident((1, d)),
                  pl.BlockSpec((1, d, FF_CHUNK), lambda f: (layer, 0, a_col(f))),
                  pl.BlockSpec((1, d, FF_CHUNK), lambda f: (layer, 0, g_col(f))),
                  pl.BlockSpec((1, FF_CHUNK, d), lambda f: (layer, f, 0)),
                  pl.BlockSpec((1, 4, FF_CHUNK), lambda f: (layer, 0, a_col(f))),
                  pl.BlockSpec((1, 4, FF_CHUNK), lambda f: (layer, 0, g_col(f))),
                  pl.BlockSpec((hist, FF_CHUNK), lambda f: (0, a_col(f))),
                  pl.BlockSpec((hist, FF_CHUNK), lambda f: (0, g_col(f)))],
        out_specs=[pl.BlockSpec((rows, d), lambda f: (0, 0)), c_spec, c_spec],
        out_shape=[jax.ShapeDtypeStruct((rows, d), F32), c_shape, c_shape],
        scratch_shapes=[pltpu.VMEM((rows, d), BF16), pltpu.VMEM((rows, d), F32)],
        compiler_params=_params(("arbitrary",)),
        name="ffn_sample",
    )(x2d, gains, g_final.reshape(1, d), w_up, w_up, w_down, cwb, cwb, buf, buf)


def _pool_mix(pooled, w_ref, scale_ref, x):
    mixed = [_dot(pooled[g].astype(BF16), w_ref[0, g]) for g in range(len(POOL_WINDOWS))]
    return x + jnp.concatenate(mixed, axis=-1) * scale_ref[0]


def _pool_prompt_body(x_ref, g_ref, w_ref, scale_ref, out_ref, hist_ref):
    t = pl.program_id(1)

    @pl.when(t == 0)
    def _():
        hist_ref[...] = jnp.zeros_like(hist_ref)

    x = x_ref[...]
    tile = x.shape[0]
    h = _rms(x, g_ref[0])
    ext = jnp.concatenate([hist_ref[0], h], axis=0)
    hist_ref[0] = h[tile - POOL_CARRY:]
    pos1 = (t * tile + 1 + lax.broadcasted_iota(jnp.int32, (tile, POOL_GC), 0)).astype(F32)
    pooled = []
    s = ext
    for g, w in enumerate(POOL_WINDOWS):
        s = s[:, POOL_GC * (1 if g else 0):]
        s = s + pltpu.roll(s, w // 2, axis=0)
        cnt = jnp.minimum(pos1, float(w))
        pooled.append(s[POOL_CARRY:, :POOL_GC] / cnt - h[:, g * POOL_GC:(g + 1) * POOL_GC])
    out_ref[...] = _pool_mix(pooled, w_ref, scale_ref, x)


def _pool_prompt(x2d, gains, layer, pool_w, pool_scale, pool_layer, n_seq):
    rows, d = x2d.shape
    seq_rows = rows // n_seq
    tile = min(seq_rows, WIDE_ROW_TILE)
    tps = seq_rows // tile
    row_spec = pl.BlockSpec((tile, d), lambda b, t: (b * tps + t, 0))
    return pl.pallas_call(
        _pool_prompt_body,
        grid=(n_seq, tps),
        in_specs=[row_spec, _layer_resident(gains.shape, layer), _layer_resident(pool_w.shape, pool_layer),
                  _layer_resident(pool_scale.shape, pool_layer)],
        out_specs=[row_spec, pl.BlockSpec((1, POOL_CARRY, d), lambda b, t: (b, 0, 0))],
        out_shape=[jax.ShapeDtypeStruct((rows, d), F32), jax.ShapeDtypeStruct((n_seq, POOL_CARRY, d), F32)],
        compiler_params=_params(("arbitrary", "arbitrary")),
        name="pool_prompt",
    )(x2d, gains, pool_w, pool_scale)


def _pool_sample_body(x_ref, g_ref, w_ref, scale_ref, buf_ref, out_ref, h_ref, *, n_seq, seq_rows, pos0):
    xt = _to_time_major(x_ref[...], n_seq, seq_rows)
    h = _rms(xt, g_ref[0])
    s = jnp.concatenate([buf_ref[...], h], axis=0)
    rows = n_seq * seq_rows
    first = POOL_BUF
    pooled = []
    for g, w in enumerate(POOL_WINDOWS):
        s = s[:, POOL_GC * (1 if g else 0):]
        step = (w // 2) * n_seq
        s = s[step:] + s[:-step]
        first -= w // 2
        win = s[first * n_seq: first * n_seq + rows, :POOL_GC]
        inv = [1.0 / min(pos0 + t + 1, w) for t in range(seq_rows)]
        if len(set(inv)) == 1:
            win = win * inv[0]
        else:
            win = jnp.concatenate([win[t * n_seq:(t + 1) * n_seq] * inv[t] for t in range(seq_rows)], axis=0)
        pooled.append(win - h[:, g * POOL_GC:(g + 1) * POOL_GC])
    out_ref[...] = _from_time_major(_pool_mix(pooled, w_ref, scale_ref, xt), n_seq, seq_rows)
    h_ref[...] = _from_time_major(h, n_seq, seq_rows)


def _pool_sample(x2d, gains, layer, pool_w, pool_scale, pool_layer, buf_t, n_seq, pos0):
    rows, d = x2d.shape
    seq_rows = rows // n_seq
    full = lambda s: pl.BlockSpec(s, lambda i: (0,) * len(s))
    return pl.pallas_call(
        functools.partial(_pool_sample_body, n_seq=n_seq, seq_rows=seq_rows, pos0=pos0),
        grid=(1,),
        in_specs=[full((rows, d)), _layer_resident(gains.shape, layer), _layer_resident(pool_w.shape, pool_layer),
                  _layer_resident(pool_scale.shape, pool_layer), full(buf_t.shape)],
        out_specs=[full((rows, d)), full((rows, d))],
        out_shape=[jax.ShapeDtypeStruct((rows, d), F32)] * 2,
        compiler_params=_params(("arbitrary",)),
        name="pool_sample",
    )(x2d, gains, pool_w, pool_scale, buf_t)


def _rope_tables(pos):
    inv = 1.0 / (ROPE_BASE ** (jnp.arange(0, RET_DK, 2, dtype=F32) / RET_DK))
    ang = pos[:, None] * inv[None, :]
    return jnp.cos(ang), jnp.sin(ang)


def kernel(x_prompt, x_sample, mem_prompt, cache_mem_k, cache_mem_v, state_ret, cache_pool, cache_ffn_conv, w_ret_in, ret_gn, w_ret_out, pool_w, pool_scale, norm_mem, w_xq, w_xk, w_xv, w_xo, w_up, conv_w, conv_b, w_down, norm_mix, norm_xattn, norm_ffn, norm_final):
    bp, lp, d = x_prompt.shape
    bs, ls, _ = x_sample.shape
    depth = w_up.shape[0]
    n_mem = mem_prompt.shape[1]

    w_in_b, w_out_b, pool_w_b, w_up_b, w_down_b = (
        w.astype(BF16) for w in (w_ret_in, w_ret_out, pool_w, w_up, w_down))
    cwb = jnp.concatenate([conv_w, conv_b[:, None, :]], axis=1)
    g_mix, g_x, g_ffn = _gain(norm_mix), _gain(norm_xattn), _gain(norm_ffn)
    gn_g, p_scale = _gain(ret_gn), _gain(pool_scale)

    mem_k2, mem_v2 = _mem_kv(mem_prompt.reshape(bp * n_mem, d), norm_mem, w_xk, w_xv)
    kv_p = tuple(a.reshape(depth * bp, n_mem * KV_ROW_STRIDE, LANES) for a in (mem_k2, mem_v2))
    kv_s = (_kv_rows(cache_mem_k), _kv_rows(cache_mem_v))
    states = state_ret.reshape((state_ret.shape[0] * bs,) + state_ret.shape[2:])

    cos_p, sin_p = _rope_tables(jnp.arange(lp, dtype=F32))
    cos_s, sin_s = _rope_tables(PAST_LEN + jnp.arange(ls, dtype=F32))
    cos_s, sin_s = jnp.tile(cos_s, (bs, 1)), jnp.tile(sin_s, (bs, 1))

    xp = x_prompt.reshape(bp * lp, d)
    xs = x_sample.reshape(bs * ls, d)
    ret_p, ret_s, pool_p, pool_s, conv_p, conv_s = [], [], [], [], [], []
    for i in range(depth):
        j = i // 2
        last = i == depth - 1
        if i % 2 == 0:
            q, k, v, gate = _ret_proj(xp, g_mix, i, cos_p, sin_p, w_in_b, j, bp, BF16)
            xp, s = _ret_prompt(q, k, v, gate, xp, gn_g, w_out_b, j, bp)
            ret_p.append(s)
            q, k, v, gate = _ret_proj(xs, g_mix, i, cos_s, sin_s, w_in_b, j, 1, F32)
            o, s = _ret_sample(q, k, v, states, j, bs, ls)
            xs = _ret_out(o, gate, xs, gn_g, w_out_b, j)
            ret_s.append(s)
        else:
            xp, hist = _pool_prompt(xp, g_mix, i, pool_w_b, p_scale, j, bp)
            pool_p.append(hist[:, POOL_CARRY - POOL_BUF:])
            buf_t = jnp.swapaxes(cache_pool[j], 0, 1).reshape(POOL_BUF * bs, d)
            xs, hs = _pool_sample(xs, g_mix, i, pool_w_b, p_scale, j, buf_t, bs, PAST_LEN)
            pool_s.append(jnp.concatenate([cache_pool[j], hs.reshape(bs, ls, d)], axis=1)[:, -POOL_BUF:])

        xp, xs = _xattn(xp, xs, g_x, w_xq, w_xo, i, kv_p, kv_s, bp, bs)

        xp, tail = _ffn_prompt(xp, g_ffn, norm_final, w_up_b, w_down_b, cwb, i, bp, last)
        conv_p.append(tail)
        buf = jnp.swapaxes(cache_ffn_conv[i], 0, 1).reshape((CONV_W - 1) * bs, 2 * D_FF)
        xs, ca, cg = _ffn_sample(xs, g_ffn, norm_final, w_up_b, w_down_b, cwb, i, buf, bs, last)
        tail = jnp.concatenate([ca, cg], axis=-1)
        conv_s.append(jnp.swapaxes(tail.reshape(CONV_W - 1, bs, 2 * D_FF), 0, 1))

    mk = _kv_unrows(mem_k2, (depth, bp), n_mem)
    mv = _kv_unrows(mem_v2, (depth, bp), n_mem)
    return (xp.reshape(bp, lp, d), xs.reshape(bs, ls, d), jnp.stack(ret_p), jnp.stack(ret_s).astype(state_ret.dtype),
            jnp.stack(pool_p), jnp.stack(pool_s), jnp.stack(conv_p), jnp.stack(conv_s), mk, mv)
```

```python
import functools

import jax
import jax.numpy as jnp
from jax import lax
from jax.experimental import pallas as pl
from jax.experimental.pallas import tpu as pltpu

F32 = jnp.float32
BF16 = jnp.bfloat16

D_MODEL = 1024
PAST_LEN = 16384
RET_HEADS = 4
RET_DK = D_MODEL // RET_HEADS
RET_DV = 2 * D_MODEL // RET_HEADS
HK = RET_HEADS * RET_DK
HV = RET_HEADS * RET_DV
ROPE_BASE = 10000.0
POOL_WINDOWS = (2, 4, 8, 16)
POOL_GC = D_MODEL // len(POOL_WINDOWS)
POOL_BUF = max(POOL_WINDOWS) - 1
X_HEADS = 4
X_HEAD_DIM = D_MODEL // X_HEADS
D_FF = 2816
CONV_W = 3
NORM_EPS = 1e-6
GN_EPS = 1e-5

SUBLANES = 8
LANES = 128
KV_LANE_TILES = X_HEAD_DIM // LANES
KV_ROW_STRIDE = X_HEADS * KV_LANE_TILES
assert KV_ROW_STRIDE == SUBLANES
ROW_TILE = 512
FFN_ROW_TILE = 512
WIDE_ROW_TILE = 1024
RET_CHUNK_PROMPT = 256
FF_CHUNK = 1408
NF = D_FF // FF_CHUNK
FFN_PROMPT_CHUNK = 1536
FFN_GATE_ROWS = 64
POOL_CARRY = 16
VMEM_LIMIT = 56 * 1024 * 1024


def _params(sem):
    return pltpu.CompilerParams(dimension_semantics=sem, vmem_limit_bytes=VMEM_LIMIT)


def _resident(shape):
    zeros = (0,) * len(shape)
    return pl.BlockSpec(shape, lambda *_: zeros, pipeline_mode=pl.Buffered(1))


def _layer_resident(stacked_shape, layer):
    idx = (layer,) + (0,) * (len(stacked_shape) - 1)
    return pl.BlockSpec((1,) + tuple(stacked_shape[1:]), lambda *_: idx, pipeline_mode=pl.Buffered(1))


def _gain(g):
    return g.reshape(g.shape[0], 1, g.shape[1])


def _dot(a, b):
    return jnp.dot(a, b, preferred_element_type=F32)


def _dot_nt(a, b):
    return lax.dot_general(a, b, (((1,), (1,)), ((), ())), preferred_element_type=F32)


def _dot_tn(a, b):
    return lax.dot_general(a, b, (((0,), (0,)), ((), ())), preferred_element_type=F32)


def _rms(x, g):
    return x * lax.rsqrt(jnp.mean(x * x, axis=-1, keepdims=True) + NORM_EPS) * g


def _silu(x):
    return x * (1.0 / (1.0 + jnp.exp(-x)))


def _to_time_major(x, n_seq, seq_rows):
    return jnp.swapaxes(x.reshape(n_seq, seq_rows, x.shape[-1]), 0, 1).reshape(x.shape)


def _from_time_major(x, n_seq, seq_rows):
    return jnp.swapaxes(x.reshape(seq_rows, n_seq, x.shape[-1]), 0, 1).reshape(x.shape)


def _memkv_body(mem_ref, g_ref, wk_ref, wv_ref, k_ref, v_ref):
    h = _rms(mem_ref[...], g_ref[0]).astype(BF16)
    for w_ref, o_ref in ((wk_ref, k_ref), (wv_ref, v_ref)):
        y = _dot(h, w_ref[0].astype(BF16))
        for hd in range(X_HEADS):
            for c in range(KV_LANE_TILES):
                col = hd * X_HEAD_DIM + c * LANES
                o_ref[0, pl.ds(c * X_HEADS + hd, y.shape[0], stride=KV_ROW_STRIDE), :] = y[:, col:col + LANES]


def _mem_kv(mem2d, norm_mem, wk, wv):
    rows, d = mem2d.shape
    depth = wk.shape[0]
    tile = min(rows, 1024)
    w_spec = pl.BlockSpec((1, d, d), lambda i, r: (i, 0, 0))
    o_spec = pl.BlockSpec((1, tile * KV_ROW_STRIDE, LANES), lambda i, r: (i, r, 0))
    return pl.pallas_call(
        _memkv_body,
        grid=(depth, rows // tile),
        in_specs=[pl.BlockSpec((tile, d), lambda i, r: (r, 0)),
                  pl.BlockSpec((1, 1, d), lambda i, r: (i, 0, 0)), w_spec, w_spec],
        out_specs=[o_spec, o_spec],
        out_shape=[jax.ShapeDtypeStruct((depth, rows * KV_ROW_STRIDE, LANES), F32)] * 2,
        compiler_params=_params(("arbitrary", "arbitrary")),
        name="mem_kv",
    )(mem2d, _gain(norm_mem), wk, wv)


def _kv_rows(a):
    lead, n_mem = a.shape[:-3], a.shape[-3]
    a = a.reshape(lead + (n_mem, X_HEADS, KV_LANE_TILES, LANES))
    a = jnp.swapaxes(a, -3, -2)
    return a.reshape((-1, n_mem * KV_ROW_STRIDE, LANES))


def _kv_unrows(a, lead, n_mem):
    a = a.reshape(lead + (n_mem, KV_LANE_TILES, X_HEADS, LANES))
    a = jnp.swapaxes(a, -3, -2)
    return a.reshape(lead + (n_mem, X_HEADS, X_HEAD_DIM))


def _head_kv(ref, i, hd, n_mem):
    parts = [ref[i, pl.ds(c * X_HEADS + hd, n_mem, stride=KV_ROW_STRIDE), :] for c in range(KV_LANE_TILES)]
    return jnp.concatenate(parts, axis=1).astype(BF16)


def _retproj_body(x_ref, g_ref, cos_ref, sin_ref, w_ref, q_ref, k_ref, v_ref, gate_ref):
    h = _rms(x_ref[...], g_ref[0]).astype(BF16)
    cos = cos_ref[...]
    sin = sin_ref[...]
    half = RET_DK // 2

    def rope_store(col0, out_ref, scale):
        p = _dot(h, w_ref[0, :, col0:col0 + HK])
        for hd in range(RET_HEADS):
            a = p[:, hd * RET_DK: hd * RET_DK + half]
            b = p[:, hd * RET_DK + half: (hd + 1) * RET_DK]
            out_ref[:, hd * RET_DK: hd * RET_DK + half] = ((a * cos - b * sin) * scale).astype(out_ref.dtype)
            out_ref[:, hd * RET_DK + half: (hd + 1) * RET_DK] = ((b * cos + a * sin) * scale).astype(out_ref.dtype)

    rope_store(0, q_ref, 1.0)
    rope_store(HK, k_ref, RET_DK ** -0.5)
    v_ref[...] = _dot(h, w_ref[0, :, 2 * HK: 2 * HK + HV]).astype(v_ref.dtype)
    gate_ref[...] = _dot(h, w_ref[0, :, 2 * HK + HV:]).astype(gate_ref.dtype)


def _ret_proj(x2d, gains, layer, cos, sin, w_in, ret_layer, n_seq, out_dtype):
    rows, d = x2d.shape
    seq_rows = rows // n_seq
    tile = min(seq_rows, WIDE_ROW_TILE)
    tps = seq_rows // tile
    row_spec = lambda w: pl.BlockSpec((tile, w), lambda b, t: (b * tps + t, 0))
    tab_spec = pl.BlockSpec((tile, RET_DK // 2), lambda b, t: (t, 0))
    return pl.pallas_call(
        _retproj_body,
        grid=(n_seq, tps),
        in_specs=[row_spec(d), _layer_resident(gains.shape, layer), tab_spec, tab_spec,
                  _layer_resident(w_in.shape, ret_layer)],
        out_specs=[row_spec(HK), row_spec(HK), row_spec(HV), row_spec(HV)],
        out_shape=[jax.ShapeDtypeStruct((rows, HK), out_dtype), jax.ShapeDtypeStruct((rows, HK), out_dtype),
                   jax.ShapeDtypeStruct((rows, HV), out_dtype), jax.ShapeDtypeStruct((rows, HV), out_dtype)],
        compiler_params=_params(("arbitrary", "arbitrary")),
        name="ret_proj",
    )(x2d, gains, cos, sin, w_in)


def _ret_units(q_ref, k_ref, v_ref, s_in, s_out, o_ref, tables, row_blocks, lookahead):
    inner_ref, qdec_ref, kdec_ref, cdec_ref = tables
    units = [(rows, i, h) for rows, i in row_blocks for h in range(RET_HEADS)]

    def issue(rows, i, h):
        qb = q_ref[rows, h * RET_DK:(h + 1) * RET_DK].astype(BF16)
        kb = k_ref[rows, h * RET_DK:(h + 1) * RET_DK].astype(BF16)
        return _dot_nt(qb, kb), _dot(qb, s_in[i, h].astype(BF16))

    def finish(rows, i, h, qk, qs):
        kh = k_ref[rows, h * RET_DK:(h + 1) * RET_DK]
        vb = v_ref[rows, h * RET_DV:(h + 1) * RET_DV].astype(BF16)
        p = (qk * inner_ref[h]).astype(BF16)
        o_ref[rows, h * RET_DV:(h + 1) * RET_DV] = _dot(p, vb) + qs * qdec_ref[h]
        kd = (kh.astype(F32) * kdec_ref[h]).astype(BF16)
        s_out[i, h] = s_in[i, h] * cdec_ref[h] + _dot_tn(kd, vb)

    if not lookahead:
        for unit in units:
            finish(*unit, *issue(*unit))
        return
    nxt = issue(*units[0])
    for n, unit in enumerate(units):
        cur = nxt
        if n + 1 < len(units):
            nxt = issue(*units[n + 1])
        finish(*unit, *cur)


def _decay_tables(chunk):
    lg = jnp.log(1.0 - 2.0 ** (-5.0 - jnp.arange(RET_HEADS, dtype=F32)))
    idx = jnp.arange(chunk, dtype=F32)
    rel = idx[:, None] - idx[None, :]
    inner = jnp.where(rel[None] >= 0, jnp.exp(jnp.maximum(rel, 0.0)[None] * lg[:, None, None]), 0.0)
    q_dec = jnp.exp((idx + 1.0)[None, :] * lg[:, None])[..., None]
    k_dec = jnp.exp((chunk - 1.0 - idx)[None, :] * lg[:, None])[..., None]
    c_dec = jnp.exp(chunk * lg)[:, None, None]
    return (inner, jnp.broadcast_to(q_dec, (RET_HEADS, chunk, RET_DV)),
            jnp.broadcast_to(k_dec, (RET_HEADS, chunk, RET_DK)),
            jnp.broadcast_to(c_dec, (RET_HEADS, 1, RET_DV)))


def _gn_gate(o, gate, gn):
    parts = []
    for h in range(RET_HEADS):
        oh = o[:, h * RET_DV:(h + 1) * RET_DV]
        mu = jnp.mean(oh, axis=-1, keepdims=True)
        dlt = oh - mu
        var = jnp.mean(dlt * dlt, axis=-1, keepdims=True)
        parts.append(dlt * lax.rsqrt(var + GN_EPS))
    on = jnp.concatenate(parts, axis=-1) * gn
    return (_silu(gate.astype(F32)) * on).astype(BF16)


def _ret_prompt_body(q_ref, k_ref, v_ref, gate_ref, x_ref, gn_ref, w_ref, inner_ref, qdec_ref, kdec_ref, cdec_ref,
                     out_ref, s_ref, o_ref, *, chunk):
    @pl.when(pl.program_id(1) == 0)
    def _():
        s_ref[...] = jnp.zeros_like(s_ref)

    chunks = [slice(c * chunk, (c + 1) * chunk) for c in range(q_ref.shape[0] // chunk)]
    _ret_units(q_ref, k_ref, v_ref, s_ref, s_ref, o_ref, (inner_ref, qdec_ref, kdec_ref, cdec_ref),
               [(rows, 0) for rows in chunks], lookahead=False)
    for rows in chunks:
        z = _gn_gate(o_ref[rows], gate_ref[rows], gn_ref[0])
        out_ref[rows] = x_ref[rows] + _dot(z, w_ref[0])


def _ret_prompt(q, k, v, gate, x2d, gn_g, w_out, ret_layer, n_seq):
    rows, d = x2d.shape
    seq_rows = rows // n_seq
    tile = min(seq_rows, ROW_TILE)
    chunk = min(tile, RET_CHUNK_PROMPT)
    tps = seq_rows // tile
    tabs = _decay_tables(chunk)
    row_spec = lambda w: pl.BlockSpec((tile, w), lambda b, t: (b * tps + t, 0))
    return pl.pallas_call(
        functools.partial(_ret_prompt_body, chunk=chunk),
        grid=(n_seq, tps),
        in_specs=[row_spec(HK), row_spec(HK), row_spec(HV), row_spec(HV), row_spec(d),
                  _layer_resident(gn_g.shape, ret_layer), _layer_resident(w_out.shape, ret_layer)]
                 + [_resident(t.shape) for t in tabs],
        out_specs=[row_spec(d), pl.BlockSpec((1, RET_HEADS, RET_DK, RET_DV), lambda b, t: (b, 0, 0, 0))],
        out_shape=[jax.ShapeDtypeStruct((rows, d), F32),
                   jax.ShapeDtypeStruct((n_seq, RET_HEADS, RET_DK, RET_DV), F32)],
        scratch_shapes=[pltpu.VMEM((tile, HV), F32)],
        compiler_params=_params(("arbitrary", "arbitrary")),
        name="ret_prompt",
    )(q, k, v, gate, x2d, gn_g, w_out, *tabs)


def _ret_sample_body(q_ref, k_ref, v_ref, s0_ref, inner_ref, qdec_ref, kdec_ref, cdec_ref, o_ref, s_ref, *, seq_rows):
    _ret_units(q_ref, k_ref, v_ref, s0_ref, s_ref, o_ref, (inner_ref, qdec_ref, kdec_ref, cdec_ref),
               [(slice(i * seq_rows, (i + 1) * seq_rows), i) for i in range(s0_ref.shape[0])], lookahead=True)


def _ret_sample(q, k, v, states, ret_layer, n_seq, seq_rows):
    bb = 4 if n_seq % 4 == 0 else 1
    steps = n_seq // bb
    tabs = _decay_tables(seq_rows)
    row_spec = lambda w: pl.BlockSpec((bb * seq_rows, w), lambda i: (i, 0))
    st_block = (bb, RET_HEADS, RET_DK, RET_DV)
    return pl.pallas_call(
        functools.partial(_ret_sample_body, seq_rows=seq_rows),
        grid=(steps,),
        in_specs=[row_spec(HK), row_spec(HK), row_spec(HV),
                  pl.BlockSpec(st_block, lambda i: (ret_layer * steps + i, 0, 0, 0))] + [_resident(t.shape) for t in tabs],
        out_specs=[row_spec(HV), pl.BlockSpec(st_block, lambda i: (i, 0, 0, 0))],
        out_shape=[jax.ShapeDtypeStruct((n_seq * seq_rows, HV), F32),
                   jax.ShapeDtypeStruct((n_seq, RET_HEADS, RET_DK, RET_DV), F32)],
        compiler_params=_params(("arbitrary",)),
        name="ret_sample",
    )(q, k, v, states, *tabs)


def _retout_body(o_ref, gate_ref, x_ref, gn_ref, w_ref, out_ref):
    out_ref[...] = x_ref[...] + _dot(_gn_gate(o_ref[...], gate_ref[...], gn_ref[0]), w_ref[0])


def _ret_out(o, gate, x2d, gn_g, w_out, ret_layer):
    rows, d = x2d.shape
    tile = min(rows, ROW_TILE)
    row_spec = lambda w: pl.BlockSpec((tile, w), lambda r: (r, 0))
    return pl.pallas_call(
        _retout_body,
        grid=(rows // tile,),
        in_specs=[row_spec(HV), row_spec(HV), row_spec(d), _layer_resident(gn_g.shape, ret_layer),
                  _layer_resident(w_out.shape, ret_layer)],
        out_specs=row_spec(d),
        out_shape=jax.ShapeDtypeStruct((rows, d), F32),
        compiler_params=_params(("arbitrary",)),
        name="ret_out",
    )(o, gate, x2d, gn_g, w_out)


def _softmax(s):
    e = jnp.exp(s - jnp.max(s, axis=-1, keepdims=True))
    return e / jnp.sum(e, axis=-1, keepdims=True)


def _xattn_block(x_ref, g_ref, wq_ref, wo_ref, keys, values, out_ref, att_ref, *s_refs, n_seq, seq_rows):
    head_cols = [slice(hd * X_HEAD_DIM, (hd + 1) * X_HEAD_DIM) for hd in range(X_HEADS)]
    h = _rms(x_ref[...], g_ref[0]).astype(BF16)
    q = _dot(h, wq_ref[...]) * (X_HEAD_DIM ** -0.5)
    if s_refs:
        s_ref, = s_refs
        pairs = [(slice(i * seq_rows, (i + 1) * seq_rows), cols, i, hd)
                 for i in range(n_seq) for hd, cols in enumerate(head_cols)]
        for n, (rows, cols, i, hd) in enumerate(pairs):
            s_ref[n * seq_rows:(n + 1) * seq_rows] = _dot_nt(q[rows, cols].astype(BF16), keys(i, hd))
        s_ref[...] = _softmax(s_ref[...])
        for n, (rows, cols, i, hd) in enumerate(pairs):
            att_ref[rows, cols] = _dot(s_ref[n * seq_rows:(n + 1) * seq_rows].astype(BF16), values(i, hd))
    else:
        scores = lambda hd: _dot_nt(q[:, head_cols[hd]].astype(BF16), keys(0, hd))
        nxt = scores(0)
        for hd, cols in enumerate(head_cols):
            s = nxt
            if hd + 1 < X_HEADS:
                nxt = scores(hd + 1)
            att_ref[:, cols] = _dot(_softmax(s).astype(BF16), values(0, hd))
    out_ref[...] = x_ref[...] + _dot(att_ref[...].astype(BF16), wo_ref[...])


def _xattn_body(xl_ref, xs_ref, g_ref, wq_ref, wo_ref, kl_ref, vl_ref, ks_ref, vs_ref, outl_ref, outs_ref,
                attl_ref, atts_ref, score_ref, wqb_ref, wob_ref, kvl_ref, *,
                long_steps, short_steps, tiles_per_seq, short_seqs, short_rows):
    step = pl.program_id(0)
    idx = step // 2
    n_mem = kl_ref.shape[1] // KV_ROW_STRIDE

    @pl.when(step == 0)
    def _():
        wqb_ref[...] = wq_ref[0].astype(BF16)
        wob_ref[...] = wo_ref[0].astype(BF16)

    @pl.when((step % 2 == 0) & (idx < long_steps))
    def _():
        @pl.when(idx % tiles_per_seq == 0)
        def _():
            for hd in range(X_HEADS):
                kvl_ref[hd] = _head_kv(kl_ref, 0, hd, n_mem)
                kvl_ref[X_HEADS + hd] = _head_kv(vl_ref, 0, hd, n_mem)

        _xattn_block(xl_ref, g_ref, wqb_ref, wob_ref, lambda i, hd: kvl_ref[hd], lambda i, hd: kvl_ref[X_HEADS + hd],
                     outl_ref, attl_ref, n_seq=1, seq_rows=xl_ref.shape[0])

    @pl.when((step % 2 == 1) & (idx < short_steps))
    def _():
        _xattn_block(xs_ref, g_ref, wqb_ref, wob_ref, lambda i, hd: _head_kv(ks_ref, i, hd, n_mem),
                     lambda i, hd: _head_kv(vs_ref, i, hd, n_mem), outs_ref, atts_ref, score_ref,
                     n_seq=short_seqs, seq_rows=short_rows)


def _xattn(x_long, x_short, gains, wq, wo, layer, kv_long, kv_short, n_long, n_short):
    d = x_long.shape[1]
    long_rows, short_rows = x_long.shape[0] // n_long, x_short.shape[0] // n_short
    assert long_rows % ROW_TILE == 0
    n_mem = kv_long[0].shape[1]
    tps = long_rows // ROW_TILE
    long_steps = n_long * tps
    bb = 4 if n_short % 4 == 0 else 1
    short_steps = n_short // bb
    li = lambda s: jnp.minimum(s // 2, long_steps - 1)
    si = lambda s: jnp.minimum(jnp.maximum(s - 1, 0) // 2, short_steps - 1)
    long_spec = pl.BlockSpec((ROW_TILE, d), lambda s: (li(s), 0))
    short_spec = pl.BlockSpec((bb * short_rows, d), lambda s: (si(s), 0))
    kvl_spec = pl.BlockSpec((1, n_mem, LANES), lambda s: (layer * n_long + li(s) // tps, 0, 0))
    kvs_spec = pl.BlockSpec((bb, n_mem, LANES), lambda s: (layer * short_steps + si(s), 0, 0))
    mem_tokens = n_mem // KV_ROW_STRIDE
    return pl.pallas_call(
        functools.partial(_xattn_body, long_steps=long_steps, short_steps=short_steps, tiles_per_seq=tps,
                          short_seqs=bb, short_rows=short_rows),
        grid=(2 * max(long_steps, short_steps),),
        in_specs=[long_spec, short_spec, _layer_resident(gains.shape, layer), _layer_resident(wq.shape, layer),
                  _layer_resident(wo.shape, layer), kvl_spec, kvl_spec, kvs_spec, kvs_spec],
        out_specs=[long_spec, short_spec],
        out_shape=[jax.ShapeDtypeStruct(x_long.shape, F32), jax.ShapeDtypeStruct(x_short.shape, F32)],
        scratch_shapes=[pltpu.VMEM((ROW_TILE, d), F32), pltpu.VMEM((bb * short_rows, d), F32),
                        pltpu.VMEM((bb * X_HEADS * short_rows, mem_tokens), F32),
                        pltpu.VMEM((d, d), BF16), pltpu.VMEM((d, d), BF16),
                        pltpu.VMEM((2 * X_HEADS, mem_tokens, X_HEAD_DIM), BF16)],
        compiler_params=_params(("arbitrary",)),
        name="xattn",
    )(x_long, x_short, gains, wq, wo, *kv_long, *kv_short)


def _conv_chunk(u, prev, cwb, shift):
    c = cwb[3:4] + cwb[0:1] * shift(u, prev, 2)
    c = c + cwb[1:2] * shift(u, prev, 1)
    return c + cwb[2:3] * u


def _ffn_gate(ua, ug, prev_a, prev_g, cwb_a, cwb_g, shift):
    return (_conv_chunk(ua, prev_a, cwb_a, shift) * _silu(_conv_chunk(ug, prev_g, cwb_g, shift))).astype(BF16)


def _ffn_chunk(h, wa, wg, wd, prev_a, prev_g, cwb_a, cwb_g, shift):
    ua = _dot(h, wa)
    ug = _dot(h, wg)
    return _dot(_ffn_gate(ua, ug, prev_a, prev_g, cwb_a, cwb_g, shift), wd), ua, ug


def _ffn_prompt_body(x_ref, g_ref, gf_ref, wu_ref, wd_ref, cwb_ref, out_ref, c_ref, tail_ref,
                     h_ref, acc_ref, ua0_ref, ug0_ref, ua1_ref, ug1_ref, z0_ref, z1_ref, *, final_norm):
    halo = (CONV_W - 1) * SUBLANES
    tile = acc_ref.shape[0]
    groups = tile // SUBLANES

    @pl.when(pl.program_id(1) == 0)
    def _():
        c_ref[...] = jnp.zeros_like(c_ref)

    x = jnp.swapaxes(x_ref[...].reshape(SUBLANES, groups, D_MODEL), 0, 1).reshape(tile, D_MODEL)
    h_ref[...] = _rms(x, g_ref[0]).astype(BF16)
    acc_ref[...] = x
    slots = ((ua0_ref, ug0_ref), (ua1_ref, ug1_ref))
    starts = range(0, D_FF, FFN_PROMPT_CHUNK)
    n_chunks = len(starts)

    def cols(f):
        lo, hi = starts[f], min(starts[f] + FFN_PROMPT_CHUNK, D_FF)
        return slice(lo, hi), slice(D_FF + lo, D_FF + hi), hi - lo

    def up(f):
        ca, cg, width = cols(f)
        first_sublane = lax.broadcasted_iota(jnp.int32, (SUBLANES, width), 0) == 0
        for u_ref, cc in zip(slots[f % 2], (ca, cg)):
            u = _dot(h_ref[...], wu_ref[0, :, cc])
            u_ref[halo:, :width] = u
            for m in range(CONV_W - 1):
                rows = slice(m * SUBLANES, (m + 1) * SUBLANES)
                src = u[tile - halo + m * SUBLANES: tile - halo + (m + 1) * SUBLANES]
                u_ref[rows, :width] = jnp.where(first_sublane, pltpu.roll(c_ref[0, rows, cc], 1, axis=0),
                                                pltpu.roll(src, 1, axis=0))
                c_ref[0, rows, cc] = src

    def conv(u_ref, cw, r0, n, width):
        c = cw[3:4] + cw[0:1] * u_ref[r0: r0 + n, :width]
        c = c + cw[1:2] * u_ref[r0 + SUBLANES: r0 + SUBLANES + n, :width]
        return c + cw[2:3] * u_ref[r0 + halo: r0 + halo + n, :width]

    def gate(f):
        ca, cg, width = cols(f)
        ua_ref, ug_ref = slots[f % 2]
        cwa, cwg = cwb_ref[0, :, ca], cwb_ref[0, :, cg]
        z_ref = (z0_ref, z1_ref)[f % 2]
        for r0 in range(0, tile, FFN_GATE_ROWS):
            z_ref[r0: r0 + FFN_GATE_ROWS, :width] = (
                conv(ua_ref, cwa, r0, FFN_GATE_ROWS, width)
                * _silu(conv(ug_ref, cwg, r0, FFN_GATE_ROWS, width))).astype(BF16)
        return z_ref[:, :width]

    up(0)
    for f in range(n_chunks):
        if f + 1 < n_chunks:
            up(f + 1)
        acc_ref[...] += _dot(gate(f), wd_ref[0, cols(f)[0], :])
    y = acc_ref[...]
    if final_norm:
        y = _rms(y, gf_ref[...])
    out_ref[...] = jnp.swapaxes(y.reshape(groups, SUBLANES, D_MODEL), 0, 1).reshape(tile, D_MODEL)
    for m in range(CONV_W - 1):
        tail_ref[0, m:m + 1, :] = c_ref[0, (m + 1) * SUBLANES - 1:(m + 1) * SUBLANES, :]


def _ffn_prompt(x2d, gains, g_final, w_up, w_down, cwb, layer, n_seq, final_norm):
    rows, d = x2d.shape
    seq_rows = rows // n_seq
    tile = min(seq_rows, FFN_ROW_TILE)
    tps = seq_rows // tile
    halo = (CONV_W - 1) * SUBLANES
    row_spec = pl.BlockSpec((tile, d), lambda b, t: (b * tps + t, 0))
    u_scratch = pltpu.VMEM((halo + tile, FFN_PROMPT_CHUNK), F32)
    z_scratch = pltpu.VMEM((tile, FFN_PROMPT_CHUNK), BF16)
    y, _, tail = pl.pallas_call(
        functools.partial(_ffn_prompt_body, final_norm=final_norm),
        grid=(n_seq, tps),
        in_specs=[row_spec, _layer_resident(gains.shape, layer), _resident((1, d)),
                  _layer_resident(w_up.shape, layer), _layer_resident(w_down.shape, layer),
                  _layer_resident(cwb.shape, layer)],
        out_specs=[row_spec, pl.BlockSpec((1, halo, 2 * D_FF), lambda b, t: (b, 0, 0)),
                   pl.BlockSpec((1, CONV_W - 1, 2 * D_FF), lambda b, t: (b, 0, 0))],
        out_shape=[jax.ShapeDtypeStruct((rows, d), F32), jax.ShapeDtypeStruct((n_seq, halo, 2 * D_FF), F32),
                   jax.ShapeDtypeStruct((n_seq, CONV_W - 1, 2 * D_FF), F32)],
        scratch_shapes=[pltpu.VMEM((tile, d), BF16), pltpu.VMEM((tile, d), F32),
                        u_scratch, u_scratch, u_scratch, u_scratch, z_scratch, z_scratch],
        compiler_params=_params(("arbitrary", "arbitrary")),
        name="ffn_prompt",
    )(x2d, gains, g_final.reshape(1, d), w_up, w_down, cwb)
    return y, tail


def _ffn_sample_body(x_ref, g_ref, gf_ref, wa_ref, wg_ref, wd_ref, cwa_ref, cwg_ref, ba_ref, bg_ref,
                     out_ref, ca_ref, cg_ref, h_ref, acc_ref, *, n_seq, seq_rows, final_norm):
    f = pl.program_id(0)

    @pl.when(f == 0)
    def _():
        xt = _to_time_major(x_ref[...], n_seq, seq_rows)
        h_ref[...] = _rms(xt, g_ref[0]).astype(BF16)
        acc_ref[...] = xt

    rows = n_seq * seq_rows
    hist = (CONV_W - 1) * n_seq

    def shift(u, prev, j):
        return jnp.concatenate([prev, u], axis=0)[hist - j * n_seq: hist - j * n_seq + rows]

    y, ua, ug = _ffn_chunk(h_ref[...], wa_ref[0], wg_ref[0], wd_ref[0], ba_ref[...], bg_ref[...],
                           cwa_ref[0], cwg_ref[0], shift)
    acc_ref[...] += y
    ca_ref[...] = ua[rows - hist:]
    cg_ref[...] = ug[rows - hist:]

    @pl.when(f == pl.num_programs(0) - 1)
    def _():
        yt = acc_ref[...]
        if final_norm:
            yt = _rms(yt, gf_ref[...])
        out_ref[...] = _from_time_major(yt, n_seq, seq_rows)


def _ffn_sample(x2d, gains, g_final, w_up, w_down, cwb, layer, buf, n_seq, final_norm):
    rows, d = x2d.shape
    seq_rows = rows // n_seq
    hist = (CONV_W - 1) * n_seq
    a_col = lambda f: f
    g_col = lambda f: NF + f
    c_shape = jax.ShapeDtypeStruct((hist, D_FF), F32)
    c_spec = pl.BlockSpec((hist, FF_CHUNK), lambda f: (0, f))
    return pl.pallas_call(
        functools.partial(_ffn_sample_body, n_seq=n_seq, seq_rows=seq_rows, final_norm=final_norm),
        grid=(NF,),
        in_specs=[_resident((rows, d)), _layer_resident(gains.shape, layer), _resident((1, d)),
                  pl.BlockSpec((1, d, FF_CHUNK), lambda f: (layer, 0, a_col(f))),
                  pl.BlockSpec((1, d, FF_CHUNK), lambda f: (layer, 0, g_col(f))),
                  pl.BlockSpec((1, FF_CHUNK, d), lambda f: (layer, f, 0)),
                  pl.BlockSpec((1, 4, FF_CHUNK), lambda f: (layer, 0, a_col(f))),
                  pl.BlockSpec((1, 4, FF_CHUNK), lambda f: (layer, 0, g_col(f))),
                  pl.BlockSpec((hist, FF_CHUNK), lambda f: (0, a_col(f))),
                  pl.BlockSpec((hist, FF_CHUNK), lambda f: (0, g_col(f)))],
        out_specs=[pl.BlockSpec((rows, d), lambda f: (0, 0)), c_spec, c_spec],
        out_shape=[jax.ShapeDtypeStruct((rows, d), F32), c_shape, c_shape],
        scratch_shapes=[pltpu.VMEM((rows, d), BF16), pltpu.VMEM((rows, d), F32)],
        compiler_params=_params(("arbitrary",)),
        name="ffn_sample",
    )(x2d, gains, g_final.reshape(1, d), w_up, w_up, w_down, cwb, cwb, buf, buf)


def _pool_mix(pooled, w_ref, scale_ref, x):
    mixed = [_dot(pooled[g].astype(BF16), w_ref[0, g]) for g in range(len(POOL_WINDOWS))]
    return x + jnp.concatenate(mixed, axis=-1) * scale_ref[0]


def _pool_prompt_body(x_ref, g_ref, w_ref, scale_ref, out_ref, hist_ref):
    t = pl.program_id(1)

    @pl.when(t == 0)
    def _():
        hist_ref[...] = jnp.zeros_like(hist_ref)

    x = x_ref[...]
    tile = x.shape[0]
    h = _rms(x, g_ref[0])
    ext = jnp.concatenate([hist_ref[0], h], axis=0)
    hist_ref[0] = h[tile - POOL_CARRY:]
    pos1 = (t * tile + 1 + lax.broadcasted_iota(jnp.int32, (tile, POOL_GC), 0)).astype(F32)
    pooled = []
    s = ext
    for g, w in enumerate(POOL_WINDOWS):
        s = s[:, POOL_GC * (1 if g else 0):]
        s = s + pltpu.roll(s, w // 2, axis=0)
        cnt = jnp.minimum(pos1, float(w))
        pooled.append(s[POOL_CARRY:, :POOL_GC] / cnt - h[:, g * POOL_GC:(g + 1) * POOL_GC])
    out_ref[...] = _pool_mix(pooled, w_ref, scale_ref, x)


def _pool_prompt(x2d, gains, layer, pool_w, pool_scale, pool_layer, n_seq):
    rows, d = x2d.shape
    seq_rows = rows // n_seq
    tile = min(seq_rows, WIDE_ROW_TILE)
    tps = seq_rows // tile
    row_spec = pl.BlockSpec((tile, d), lambda b, t: (b * tps + t, 0))
    return pl.pallas_call(
        _pool_prompt_body,
        grid=(n_seq, tps),
        in_specs=[row_spec, _layer_resident(gains.shape, layer), _layer_resident(pool_w.shape, pool_layer),
                  _layer_resident(pool_scale.shape, pool_layer)],
        out_specs=[row_spec, pl.BlockSpec((1, POOL_CARRY, d), lambda b, t: (b, 0, 0))],
        out_shape=[jax.ShapeDtypeStruct((rows, d), F32), jax.ShapeDtypeStruct((n_seq, POOL_CARRY, d), F32)],
        compiler_params=_params(("arbitrary", "arbitrary")),
        name="pool_prompt",
    )(x2d, gains, pool_w, pool_scale)


def _pool_sample_body(x_ref, g_ref, w_ref, scale_ref, buf_ref, out_ref, h_ref, *, n_seq, seq_rows, pos0):
    xt = _to_time_major(x_ref[...], n_seq, seq_rows)
    h = _rms(xt, g_ref[0])
    s = jnp.concatenate([buf_ref[...], h], axis=0)
    rows = n_seq * seq_rows
    first = POOL_BUF
    pooled = []
    for g, w in enumerate(POOL_WINDOWS):
        s = s[:, POOL_GC * (1 if g else 0):]
        step = (w // 2) * n_seq
        s = s[step:] + s[:-step]
        first -= w // 2
        win = s[first * n_seq: first * n_seq + rows, :POOL_GC]
        inv = [1.0 / min(pos0 + t + 1, w) for t in range(seq_rows)]
        if len(set(inv)) == 1:
            win = win * inv[0]
        else:
            win = jnp.concatenate([win[t * n_seq:(t + 1) * n_seq] * inv[t] for t in range(seq_rows)], axis=0)
        pooled.append(win - h[:, g * POOL_GC:(g + 1) * POOL_GC])
    out_ref[...] = _from_time_major(_pool_mix(pooled, w_ref, scale_ref, xt), n_seq, seq_rows)
    h_ref[...] = _from_time_major(h, n_seq, seq_rows)


def _pool_sample(x2d, gains, layer, pool_w, pool_scale, pool_layer, buf_t, n_seq, pos0):
    rows, d = x2d.shape
    seq_rows = rows // n_seq
    full = lambda s: pl.BlockSpec(s, lambda i: (0,) * len(s))
    return pl.pallas_call(
        functools.partial(_pool_sample_body, n_seq=n_seq, seq_rows=seq_rows, pos0=pos0),
        grid=(1,),
        in_specs=[full((rows, d)), _layer_resident(gains.shape, layer), _layer_resident(pool_w.shape, pool_layer),
                  _layer_resident(pool_scale.shape, pool_layer), full(buf_t.shape)],
        out_specs=[full((rows, d)), full((rows, d))],
        out_shape=[jax.ShapeDtypeStruct((rows, d), F32)] * 2,
        compiler_params=_params(("arbitrary",)),
        name="pool_sample",
    )(x2d, gains, pool_w, pool_scale, buf_t)


def _rope_tables(pos):
    inv = 1.0 / (ROPE_BASE ** (jnp.arange(0, RET_DK, 2, dtype=F32) / RET_DK))
    ang = pos[:, None] * inv[None, :]
    return jnp.cos(ang), jnp.sin(ang)


def kernel(x_prompt, x_sample, mem_prompt, cache_mem_k, cache_mem_v, state_ret, cache_pool, cache_ffn_conv, w_ret_in, ret_gn, w_ret_out, pool_w, pool_scale, norm_mem, w_xq, w_xk, w_xv, w_xo, w_up, conv_w, conv_b, w_down, norm_mix, norm_xattn, norm_ffn, norm_final):
    bp, lp, d = x_prompt.shape
    bs, ls, _ = x_sample.shape
    depth = w_up.shape[0]
    n_mem = mem_prompt.shape[1]

    w_in_b, w_out_b, pool_w_b, w_up_b, w_down_b = (
        w.astype(BF16) for w in (w_ret_in, w_ret_out, pool_w, w_up, w_down))
    cwb = jnp.concatenate([conv_w, conv_b[:, None, :]], axis=1)
    g_mix, g_x, g_ffn = _gain(norm_mix), _gain(norm_xattn), _gain(norm_ffn)
    gn_g, p_scale = _gain(ret_gn), _gain(pool_scale)

    mem_k2, mem_v2 = _mem_kv(mem_prompt.reshape(bp * n_mem, d), norm_mem, w_xk, w_xv)
    kv_p = tuple(a.reshape(depth * bp, n_mem * KV_ROW_STRIDE, LANES) for a in (mem_k2, mem_v2))
    kv_s = (_kv_rows(cache_mem_k), _kv_rows(cache_mem_v))
    states = state_ret.reshape((state_ret.shape[0] * bs,) + state_ret.shape[2:])

    cos_p, sin_p = _rope_tables(jnp.arange(lp, dtype=F32))
    cos_s, sin_s = _rope_tables(PAST_LEN + jnp.arange(ls, dtype=F32))
    cos_s, sin_s = jnp.tile(cos_s, (bs, 1)), jnp.tile(sin_s, (bs, 1))

    xp = x_prompt.reshape(bp * lp, d)
    xs = x_sample.reshape(bs * ls, d)
    ret_p, ret_s, pool_p, pool_s, conv_p, conv_s = [], [], [], [], [], []
    for i in range(depth):
        j = i // 2
        last = i == depth - 1
        if i % 2 == 0:
            q, k, v, gate = _ret_proj(xp, g_mix, i, cos_p, sin_p, w_in_b, j, bp, BF16)
            xp, s = _ret_prompt(q, k, v, gate, xp, gn_g, w_out_b, j, bp)
            ret_p.append(s)
            q, k, v, gate = _ret_proj(xs, g_mix, i, cos_s, sin_s, w_in_b, j, 1, F32)
            o, s = _ret_sample(q, k, v, states, j, bs, ls)
            xs = _ret_out(o, gate, xs, gn_g, w_out_b, j)
            ret_s.append(s)
        else:
            xp, hist = _pool_prompt(xp, g_mix, i, pool_w_b, p_scale, j, bp)
            pool_p.append(hist[:, POOL_CARRY - POOL_BUF:])
            buf_t = jnp.swapaxes(cache_pool[j], 0, 1).reshape(POOL_BUF * bs, d)
            xs, hs = _pool_sample(xs, g_mix, i, pool_w_b, p_scale, j, buf_t, bs, PAST_LEN)
            pool_s.append(jnp.concatenate([cache_pool[j], hs.reshape(bs, ls, d)], axis=1)[:, -POOL_BUF:])

        xp, xs = _xattn(xp, xs, g_x, w_xq, w_xo, i, kv_p, kv_s, bp, bs)

        xp, tail = _ffn_prompt(xp, g_ffn, norm_final, w_up_b, w_down_b, cwb, i, bp, last)
        conv_p.append(tail)
        buf = jnp.swapaxes(cache_ffn_conv[i], 0, 1).reshape((CONV_W - 1) * bs, 2 * D_FF)
        xs, ca, cg = _ffn_sample(xs, g_ffn, norm_final, w_up_b, w_down_b, cwb, i, buf, bs, last)
        tail = jnp.concatenate([ca, cg], axis=-1)
        conv_s.append(jnp.swapaxes(tail.reshape(CONV_W - 1, bs, 2 * D_FF), 0, 1))

    mk = _kv_unrows(mem_k2, (depth, bp), n_mem)
    mv = _kv_unrows(mem_v2, (depth, bp), n_mem)
    return (xp.reshape(bp, lp, d), xs.reshape(bs, ls, d), jnp.stack(ret_p), jnp.stack(ret_s).astype(state_ret.dtype),
            jnp.stack(pool_p), jnp.stack(pool_s), jnp.stack(conv_p), jnp.stack(conv_s), mk, mv)
```

```python
import functools

import jax
import jax.numpy as jnp
from jax import lax
from jax.experimental import pallas as pl
from jax.experimental.pallas import tpu as pltpu

F32 = jnp.float32
BF16 = jnp.bfloat16

D_MODEL = 1024
PAST_LEN = 16384
RET_HEADS = 4
RET_DK = D_MODEL // RET_HEADS
RET_DV = 2 * D_MODEL // RET_HEADS
HK = RET_HEADS * RET_DK
HV = RET_HEADS * RET_DV
ROPE_BASE = 10000.0
POOL_WINDOWS = (2, 4, 8, 16)
POOL_GC = D_MODEL // len(POOL_WINDOWS)
POOL_BUF = max(POOL_WINDOWS) - 1
X_HEADS = 4
X_HEAD_DIM = D_MODEL // X_HEADS
D_FF = 2816
CONV_W = 3
NORM_EPS = 1e-6
GN_EPS = 1e-5

SUBLANES = 8
LANES = 128
KV_LANE_TILES = X_HEAD_DIM // LANES
KV_ROW_STRIDE = X_HEADS * KV_LANE_TILES
assert KV_ROW_STRIDE == SUBLANES
ROW_TILE = 512
FFN_ROW_TILE = 512
WIDE_ROW_TILE = 1024
RET_CHUNK_PROMPT = 256
FF_CHUNK = 1408
NF = D_FF // FF_CHUNK
FFN_PROMPT_CHUNK = 1536
FFN_GATE_ROWS = 64
POOL_BLOCK = 128
POOL_CARRY = 16
VMEM_LIMIT = 56 * 1024 * 1024


def _params(sem):
    return pltpu.CompilerParams(dimension_semantics=sem, vmem_limit_bytes=VMEM_LIMIT)


def _resident(shape):
    zeros = (0,) * len(shape)
    return pl.BlockSpec(shape, lambda *_: zeros, pipeline_mode=pl.Buffered(1))


def _layer_resident(stacked_shape, layer):
    idx = (layer,) + (0,) * (len(stacked_shape) - 1)
    return pl.BlockSpec((1,) + tuple(stacked_shape[1:]), lambda *_: idx, pipeline_mode=pl.Buffered(1))


def _gain(g):
    return g.reshape(g.shape[0], 1, g.shape[1])


def _dot(a, b):
    return jnp.dot(a, b, preferred_element_type=F32)


def _dot_nt(a, b):
    return lax.dot_general(a, b, (((1,), (1,)), ((), ())), preferred_element_type=F32)


def _dot_tn(a, b):
    return lax.dot_general(a, b, (((0,), (0,)), ((), ())), preferred_element_type=F32)


def _rms(x, g):
    return x * lax.rsqrt(jnp.mean(x * x, axis=-1, keepdims=True) + NORM_EPS) * g


def _silu(x):
    return x * (1.0 / (1.0 + jnp.exp(-x)))


def _to_time_major(x, n_seq, seq_rows):
    return jnp.swapaxes(x.reshape(n_seq, seq_rows, x.shape[-1]), 0, 1).reshape(x.shape)


def _from_time_major(x, n_seq, seq_rows):
    return jnp.swapaxes(x.reshape(seq_rows, n_seq, x.shape[-1]), 0, 1).reshape(x.shape)


def _memkv_body(mem_ref, g_ref, wk_ref, wv_ref, k_ref, v_ref):
    h = _rms(mem_ref[...], g_ref[0]).astype(BF16)
    for w_ref, o_ref in ((wk_ref, k_ref), (wv_ref, v_ref)):
        y = _dot(h, w_ref[0].astype(BF16))
        for hd in range(X_HEADS):
            for c in range(KV_LANE_TILES):
                col = hd * X_HEAD_DIM + c * LANES
                o_ref[0, pl.ds(c * X_HEADS + hd, y.shape[0], stride=KV_ROW_STRIDE), :] = y[:, col:col + LANES]


def _mem_kv(mem2d, norm_mem, wk, wv):
    rows, d = mem2d.shape
    depth = wk.shape[0]
    tile = min(rows, 1024)
    w_spec = pl.BlockSpec((1, d, d), lambda i, r: (i, 0, 0))
    o_spec = pl.BlockSpec((1, tile * KV_ROW_STRIDE, LANES), lambda i, r: (i, r, 0))
    return pl.pallas_call(
        _memkv_body,
        grid=(depth, rows // tile),
        in_specs=[pl.BlockSpec((tile, d), lambda i, r: (r, 0)),
                  pl.BlockSpec((1, 1, d), lambda i, r: (i, 0, 0)), w_spec, w_spec],
        out_specs=[o_spec, o_spec],
        out_shape=[jax.ShapeDtypeStruct((depth, rows * KV_ROW_STRIDE, LANES), F32)] * 2,
        compiler_params=_params(("arbitrary", "arbitrary")),
        name="mem_kv",
    )(mem2d, _gain(norm_mem), wk, wv)


def _kv_rows(a):
    lead, n_mem = a.shape[:-3], a.shape[-3]
    a = a.reshape(lead + (n_mem, X_HEADS, KV_LANE_TILES, LANES))
    a = jnp.swapaxes(a, -3, -2)
    return a.reshape((-1, n_mem * KV_ROW_STRIDE, LANES))


def _kv_unrows(a, lead, n_mem):
    a = a.reshape(lead + (n_mem, KV_LANE_TILES, X_HEADS, LANES))
    a = jnp.swapaxes(a, -3, -2)
    return a.reshape(lead + (n_mem, X_HEADS, X_HEAD_DIM))


def _head_kv(ref, i, hd, n_mem):
    parts = [ref[i, pl.ds(c * X_HEADS + hd, n_mem, stride=KV_ROW_STRIDE), :] for c in range(KV_LANE_TILES)]
    return jnp.concatenate(parts, axis=1).astype(BF16)


def _retproj_body(x_ref, g_ref, cos_ref, sin_ref, w_ref, q_ref, k_ref, v_ref, gate_ref):
    h = _rms(x_ref[...], g_ref[0]).astype(BF16)
    cos = cos_ref[...]
    sin = sin_ref[...]
    half = RET_DK // 2

    def rope_store(col0, out_ref, scale):
        p = _dot(h, w_ref[0, :, col0:col0 + HK])
        for hd in range(RET_HEADS):
            a = p[:, hd * RET_DK: hd * RET_DK + half]
            b = p[:, hd * RET_DK + half: (hd + 1) * RET_DK]
            out_ref[:, hd * RET_DK: hd * RET_DK + half] = ((a * cos - b * sin) * scale).astype(out_ref.dtype)
            out_ref[:, hd * RET_DK + half: (hd + 1) * RET_DK] = ((b * cos + a * sin) * scale).astype(out_ref.dtype)

    rope_store(0, q_ref, 1.0)
    rope_store(HK, k_ref, RET_DK ** -0.5)
    v_ref[...] = _dot(h, w_ref[0, :, 2 * HK: 2 * HK + HV]).astype(v_ref.dtype)
    gate_ref[...] = _dot(h, w_ref[0, :, 2 * HK + HV:]).astype(gate_ref.dtype)


def _ret_proj(x2d, gains, layer, cos, sin, w_in, ret_layer, n_seq, out_dtype):
    rows, d = x2d.shape
    seq_rows = rows // n_seq
    tile = min(seq_rows, WIDE_ROW_TILE)
    tps = seq_rows // tile
    row_spec = lambda w: pl.BlockSpec((tile, w), lambda b, t: (b * tps + t, 0))
    tab_spec = pl.BlockSpec((tile, RET_DK // 2), lambda b, t: (t, 0))
    return pl.pallas_call(
        _retproj_body,
        grid=(n_seq, tps),
        in_specs=[row_spec(d), _layer_resident(gains.shape, layer), tab_spec, tab_spec,
                  _layer_resident(w_in.shape, ret_layer)],
        out_specs=[row_spec(HK), row_spec(HK), row_spec(HV), row_spec(HV)],
        out_shape=[jax.ShapeDtypeStruct((rows, HK), out_dtype), jax.ShapeDtypeStruct((rows, HK), out_dtype),
                   jax.ShapeDtypeStruct((rows, HV), out_dtype), jax.ShapeDtypeStruct((rows, HV), out_dtype)],
        compiler_params=_params(("arbitrary", "arbitrary")),
        name="ret_proj",
    )(x2d, gains, cos, sin, w_in)


def _ret_units(q_ref, k_ref, v_ref, s_in, s_out, o_ref, tables, row_blocks, lookahead):
    inner_ref, qdec_ref, kdec_ref, cdec_ref = tables
    units = [(rows, i, h) for rows, i in row_blocks for h in range(RET_HEADS)]

    def issue(rows, i, h):
        qb = q_ref[rows, h * RET_DK:(h + 1) * RET_DK].astype(BF16)
        kb = k_ref[rows, h * RET_DK:(h + 1) * RET_DK].astype(BF16)
        return _dot_nt(qb, kb), _dot(qb, s_in[i, h].astype(BF16))

    def finish(rows, i, h, qk, qs):
        kh = k_ref[rows, h * RET_DK:(h + 1) * RET_DK]
        vb = v_ref[rows, h * RET_DV:(h + 1) * RET_DV].astype(BF16)
        p = (qk * inner_ref[h]).astype(BF16)
        o_ref[rows, h * RET_DV:(h + 1) * RET_DV] = _dot(p, vb) + qs * qdec_ref[h]
        kd = (kh.astype(F32) * kdec_ref[h]).astype(BF16)
        s_out[i, h] = s_in[i, h] * cdec_ref[h] + _dot_tn(kd, vb)

    if not lookahead:
        for unit in units:
            finish(*unit, *issue(*unit))
        return
    nxt = issue(*units[0])
    for n, unit in enumerate(units):
        cur = nxt
        if n + 1 < len(units):
            nxt = issue(*units[n + 1])
        finish(*unit, *cur)


def _decay_tables(chunk):
    lg = jnp.log(1.0 - 2.0 ** (-5.0 - jnp.arange(RET_HEADS, dtype=F32)))
    idx = jnp.arange(chunk, dtype=F32)
    rel = idx[:, None] - idx[None, :]
    inner = jnp.where(rel[None] >= 0, jnp.exp(jnp.maximum(rel, 0.0)[None] * lg[:, None, None]), 0.0)
    q_dec = jnp.exp((idx + 1.0)[None, :] * lg[:, None])[..., None]
    k_dec = jnp.exp((chunk - 1.0 - idx)[None, :] * lg[:, None])[..., None]
    c_dec = jnp.exp(chunk * lg)[:, None, None]
    return (inner, jnp.broadcast_to(q_dec, (RET_HEADS, chunk, RET_DV)),
            jnp.broadcast_to(k_dec, (RET_HEADS, chunk, RET_DK)),
            jnp.broadcast_to(c_dec, (RET_HEADS, 1, RET_DV)))


def _gn_gate(o, gate, gn):
    parts = []
    for h in range(RET_HEADS):
        oh = o[:, h * RET_DV:(h + 1) * RET_DV]
        mu = jnp.mean(oh, axis=-1, keepdims=True)
        dlt = oh - mu
        var = jnp.mean(dlt * dlt, axis=-1, keepdims=True)
        parts.append(dlt * lax.rsqrt(var + GN_EPS))
    on = jnp.concatenate(parts, axis=-1) * gn
    return (_silu(gate.astype(F32)) * on).astype(BF16)


def _ret_prompt_body(q_ref, k_ref, v_ref, gate_ref, x_ref, gn_ref, w_ref, inner_ref, qdec_ref, kdec_ref, cdec_ref,
                     out_ref, s_ref, o_ref, *, chunk):
    @pl.when(pl.program_id(1) == 0)
    def _():
        s_ref[...] = jnp.zeros_like(s_ref)

    chunks = [slice(c * chunk, (c + 1) * chunk) for c in range(q_ref.shape[0] // chunk)]
    _ret_units(q_ref, k_ref, v_ref, s_ref, s_ref, o_ref, (inner_ref, qdec_ref, kdec_ref, cdec_ref),
               [(rows, 0) for rows in chunks], lookahead=False)
    for rows in chunks:
        z = _gn_gate(o_ref[rows], gate_ref[rows], gn_ref[0])
        out_ref[rows] = x_ref[rows] + _dot(z, w_ref[0])


def _ret_prompt(q, k, v, gate, x2d, gn_g, w_out, ret_layer, n_seq):
    rows, d = x2d.shape
    seq_rows = rows // n_seq
    tile = min(seq_rows, ROW_TILE)
    chunk = min(tile, RET_CHUNK_PROMPT)
    tps = seq_rows // tile
    tabs = _decay_tables(chunk)
    row_spec = lambda w: pl.BlockSpec((tile, w), lambda b, t: (b * tps + t, 0))
    return pl.pallas_call(
        functools.partial(_ret_prompt_body, chunk=chunk),
        grid=(n_seq, tps),
        in_specs=[row_spec(HK), row_spec(HK), row_spec(HV), row_spec(HV), row_spec(d),
                  _layer_resident(gn_g.shape, ret_layer), _layer_resident(w_out.shape, ret_layer)]
                 + [_resident(t.shape) for t in tabs],
        out_specs=[row_spec(d), pl.BlockSpec((1, RET_HEADS, RET_DK, RET_DV), lambda b, t: (b, 0, 0, 0))],
        out_shape=[jax.ShapeDtypeStruct((rows, d), F32),
                   jax.ShapeDtypeStruct((n_seq, RET_HEADS, RET_DK, RET_DV), F32)],
        scratch_shapes=[pltpu.VMEM((tile, HV), F32)],
        compiler_params=_params(("arbitrary", "arbitrary")),
        name="ret_prompt",
    )(q, k, v, gate, x2d, gn_g, w_out, *tabs)


def _ret_sample_body(q_ref, k_ref, v_ref, s0_ref, inner_ref, qdec_ref, kdec_ref, cdec_ref, o_ref, s_ref, *, seq_rows):
    _ret_units(q_ref, k_ref, v_ref, s0_ref, s_ref, o_ref, (inner_ref, qdec_ref, kdec_ref, cdec_ref),
               [(slice(i * seq_rows, (i + 1) * seq_rows), i) for i in range(s0_ref.shape[0])], lookahead=True)


def _ret_sample(q, k, v, states, ret_layer, n_seq, seq_rows):
    bb = 4 if n_seq % 4 == 0 else 1
    steps = n_seq // bb
    tabs = _decay_tables(seq_rows)
    row_spec = lambda w: pl.BlockSpec((bb * seq_rows, w), lambda i: (i, 0))
    st_block = (bb, RET_HEADS, RET_DK, RET_DV)
    return pl.pallas_call(
        functools.partial(_ret_sample_body, seq_rows=seq_rows),
        grid=(steps,),
        in_specs=[row_spec(HK), row_spec(HK), row_spec(HV),
                  pl.BlockSpec(st_block, lambda i: (ret_layer * steps + i, 0, 0, 0))] + [_resident(t.shape) for t in tabs],
        out_specs=[row_spec(HV), pl.BlockSpec(st_block, lambda i: (i, 0, 0, 0))],
        out_shape=[jax.ShapeDtypeStruct((n_seq * seq_rows, HV), F32),
                   jax.ShapeDtypeStruct((n_seq, RET_HEADS, RET_DK, RET_DV), F32)],
        compiler_params=_params(("arbitrary",)),
        name="ret_sample",
    )(q, k, v, states, *tabs)


def _retout_body(o_ref, gate_ref, x_ref, gn_ref, w_ref, out_ref):
    out_ref[...] = x_ref[...] + _dot(_gn_gate(o_ref[...], gate_ref[...], gn_ref[0]), w_ref[0])


def _ret_out(o, gate, x2d, gn_g, w_out, ret_layer):
    rows, d = x2d.shape
    tile = min(rows, ROW_TILE)
    row_spec = lambda w: pl.BlockSpec((tile, w), lambda r: (r, 0))
    return pl.pallas_call(
        _retout_body,
        grid=(rows // tile,),
        in_specs=[row_spec(HV), row_spec(HV), row_spec(d), _layer_resident(gn_g.shape, ret_layer),
                  _layer_resident(w_out.shape, ret_layer)],
        out_specs=row_spec(d),
        out_shape=jax.ShapeDtypeStruct((rows, d), F32),
        compiler_params=_params(("arbitrary",)),
        name="ret_out",
    )(o, gate, x2d, gn_g, w_out)


def _softmax(s):
    e = jnp.exp(s - jnp.max(s, axis=-1, keepdims=True))
    return e / jnp.sum(e, axis=-1, keepdims=True)


def _xattn_block(x_ref, g_ref, wq_ref, wo_ref, keys, values, out_ref, att_ref, *s_refs, n_seq, seq_rows):
    head_cols = [slice(hd * X_HEAD_DIM, (hd + 1) * X_HEAD_DIM) for hd in range(X_HEADS)]
    h = _rms(x_ref[...], g_ref[0]).astype(BF16)
    q = _dot(h, wq_ref[...]) * (X_HEAD_DIM ** -0.5)
    if s_refs:
        s_ref, = s_refs
        pairs = [(slice(i * seq_rows, (i + 1) * seq_rows), cols, i, hd)
                 for i in range(n_seq) for hd, cols in enumerate(head_cols)]
        for n, (rows, cols, i, hd) in enumerate(pairs):
            s_ref[n * seq_rows:(n + 1) * seq_rows] = _dot_nt(q[rows, cols].astype(BF16), keys(i, hd))
        s_ref[...] = _softmax(s_ref[...])
        for n, (rows, cols, i, hd) in enumerate(pairs):
            att_ref[rows, cols] = _dot(s_ref[n * seq_rows:(n + 1) * seq_rows].astype(BF16), values(i, hd))
    else:
        scores = lambda hd: _dot_nt(q[:, head_cols[hd]].astype(BF16), keys(0, hd))
        nxt = scores(0)
        for hd, cols in enumerate(head_cols):
            s = nxt
            if hd + 1 < X_HEADS:
                nxt = scores(hd + 1)
            att_ref[:, cols] = _dot(_softmax(s).astype(BF16), values(0, hd))
    out_ref[...] = x_ref[...] + _dot(att_ref[...].astype(BF16), wo_ref[...])


def _xattn_body(xl_ref, xs_ref, g_ref, wq_ref, wo_ref, kl_ref, vl_ref, ks_ref, vs_ref, outl_ref, outs_ref,
                attl_ref, atts_ref, score_ref, wqb_ref, wob_ref, kvl_ref, *,
                long_steps, short_steps, tiles_per_seq, short_seqs, short_rows):
    step = pl.program_id(0)
    idx = step // 2
    n_mem = kl_ref.shape[1] // KV_ROW_STRIDE

    @pl.when(step == 0)
    def _():
        wqb_ref[...] = wq_ref[0].astype(BF16)
        wob_ref[...] = wo_ref[0].astype(BF16)

    @pl.when((step % 2 == 0) & (idx < long_steps))
    def _():
        @pl.when(idx % tiles_per_seq == 0)
        def _():
            for hd in range(X_HEADS):
                kvl_ref[hd] = _head_kv(kl_ref, 0, hd, n_mem)
                kvl_ref[X_HEADS + hd] = _head_kv(vl_ref, 0, hd, n_mem)

        _xattn_block(xl_ref, g_ref, wqb_ref, wob_ref, lambda i, hd: kvl_ref[hd], lambda i, hd: kvl_ref[X_HEADS + hd],
                     outl_ref, attl_ref, n_seq=1, seq_rows=xl_ref.shape[0])

    @pl.when((step % 2 == 1) & (idx < short_steps))
    def _():
        _xattn_block(xs_ref, g_ref, wqb_ref, wob_ref, lambda i, hd: _head_kv(ks_ref, i, hd, n_mem),
                     lambda i, hd: _head_kv(vs_ref, i, hd, n_mem), outs_ref, atts_ref, score_ref,
                     n_seq=short_seqs, seq_rows=short_rows)


def _xattn(x_long, x_short, gains, wq, wo, layer, kv_long, kv_short, n_long, n_short):
    d = x_long.shape[1]
    long_rows, short_rows = x_long.shape[0] // n_long, x_short.shape[0] // n_short
    assert long_rows % ROW_TILE == 0
    n_mem = kv_long[0].shape[1]
    tps = long_rows // ROW_TILE
    long_steps = n_long * tps
    bb = 4 if n_short % 4 == 0 else 1
    short_steps = n_short // bb
    li = lambda s: jnp.minimum(s // 2, long_steps - 1)
    si = lambda s: jnp.minimum(jnp.maximum(s - 1, 0) // 2, short_steps - 1)
    long_spec = pl.BlockSpec((ROW_TILE, d), lambda s: (li(s), 0))
    short_spec = pl.BlockSpec((bb * short_rows, d), lambda s: (si(s), 0))
    kvl_spec = pl.BlockSpec((1, n_mem, LANES), lambda s: (layer * n_long + li(s) // tps, 0, 0))
    kvs_spec = pl.BlockSpec((bb, n_mem, LANES), lambda s: (layer * short_steps + si(s), 0, 0))
    mem_tokens = n_mem // KV_ROW_STRIDE
    return pl.pallas_call(
        functools.partial(_xattn_body, long_steps=long_steps, short_steps=short_steps, tiles_per_seq=tps,
                          short_seqs=bb, short_rows=short_rows),
        grid=(2 * max(long_steps, short_steps),),
        in_specs=[long_spec, short_spec, _layer_resident(gains.shape, layer), _layer_resident(wq.shape, layer),
                  _layer_resident(wo.shape, layer), kvl_spec, kvl_spec, kvs_spec, kvs_spec],
        out_specs=[long_spec, short_spec],
        out_shape=[jax.ShapeDtypeStruct(x_long.shape, F32), jax.ShapeDtypeStruct(x_short.shape, F32)],
        scratch_shapes=[pltpu.VMEM((ROW_TILE, d), F32), pltpu.VMEM((bb * short_rows, d), F32),
                        pltpu.VMEM((bb * X_HEADS * short_rows, mem_tokens), F32),
                        pltpu.VMEM((d, d), BF16), pltpu.VMEM((d, d), BF16),
                        pltpu.VMEM((2 * X_HEADS, mem_tokens, X_HEAD_DIM), BF16)],
        compiler_params=_params(("arbitrary",)),
        name="xattn",
    )(x_long, x_short, gains, wq, wo, *kv_long, *kv_short)


def _conv_chunk(u, prev, cwb, shift):
    c = cwb[3:4] + cwb[0:1] * shift(u, prev, 2)
    c = c + cwb[1:2] * shift(u, prev, 1)
    return c + cwb[2:3] * u


def _ffn_gate(ua, ug, prev_a, prev_g, cwb_a, cwb_g, shift):
    return (_conv_chunk(ua, prev_a, cwb_a, shift) * _silu(_conv_chunk(ug, prev_g, cwb_g, shift))).astype(BF16)


def _ffn_chunk(h, wa, wg, wd, prev_a, prev_g, cwb_a, cwb_g, shift):
    ua = _dot(h, wa)
    ug = _dot(h, wg)
    return _dot(_ffn_gate(ua, ug, prev_a, prev_g, cwb_a, cwb_g, shift), wd), ua, ug


def _ffn_prompt_body(x_ref, g_ref, gf_ref, wu_ref, wd_ref, cwb_ref, out_ref, c_ref, tail_ref,
                     h_ref, acc_ref, ua0_ref, ug0_ref, ua1_ref, ug1_ref, z0_ref, z1_ref, *, final_norm):
    halo = (CONV_W - 1) * SUBLANES
    tile = acc_ref.shape[0]
    groups = tile // SUBLANES

    @pl.when(pl.program_id(1) == 0)
    def _():
        c_ref[...] = jnp.zeros_like(c_ref)

    x = jnp.swapaxes(x_ref[...].reshape(SUBLANES, groups, D_MODEL), 0, 1).reshape(tile, D_MODEL)
    h_ref[...] = _rms(x, g_ref[0]).astype(BF16)
    acc_ref[...] = x
    slots = ((ua0_ref, ug0_ref), (ua1_ref, ug1_ref))
    starts = range(0, D_FF, FFN_PROMPT_CHUNK)
    n_chunks = len(starts)

    def cols(f):
        lo, hi = starts[f], min(starts[f] + FFN_PROMPT_CHUNK, D_FF)
        return slice(lo, hi), slice(D_FF + lo, D_FF + hi), hi - lo

    def up(f):
        ca, cg, width = cols(f)
        first_sublane = lax.broadcasted_iota(jnp.int32, (SUBLANES, width), 0) == 0
        for u_ref, cc in zip(slots[f % 2], (ca, cg)):
            u = _dot(h_ref[...], wu_ref[0, :, cc])
            u_ref[halo:, :width] = u
            for m in range(CONV_W - 1):
                rows = slice(m * SUBLANES, (m + 1) * SUBLANES)
                src = u[tile - halo + m * SUBLANES: tile - halo + (m + 1) * SUBLANES]
                u_ref[rows, :width] = jnp.where(first_sublane, pltpu.roll(c_ref[0, rows, cc], 1, axis=0),
                                                pltpu.roll(src, 1, axis=0))
                c_ref[0, rows, cc] = src

    def conv(u_ref, cw, r0, n, width):
        c = cw[3:4] + cw[0:1] * u_ref[r0: r0 + n, :width]
        c = c + cw[1:2] * u_ref[r0 + SUBLANES: r0 + SUBLANES + n, :width]
        return c + cw[2:3] * u_ref[r0 + halo: r0 + halo + n, :width]

    def gate(f):
        ca, cg, width = cols(f)
        ua_ref, ug_ref = slots[f % 2]
        cwa, cwg = cwb_ref[0, :, ca], cwb_ref[0, :, cg]
        z_ref = (z0_ref, z1_ref)[f % 2]
        for r0 in range(0, tile, FFN_GATE_ROWS):
            z_ref[r0: r0 + FFN_GATE_ROWS, :width] = (
                conv(ua_ref, cwa, r0, FFN_GATE_ROWS, width)
                * _silu(conv(ug_ref, cwg, r0, FFN_GATE_ROWS, width))).astype(BF16)
        return z_ref[:, :width]

    up(0)
    for f in range(n_chunks):
        if f + 1 < n_chunks:
            up(f + 1)
        acc_ref[...] += _dot(gate(f), wd_ref[0, cols(f)[0], :])
    y = acc_ref[...]
    if final_norm:
        y = _rms(y, gf_ref[...])
    out_ref[...] = jnp.swapaxes(y.reshape(groups, SUBLANES, D_MODEL), 0, 1).reshape(tile, D_MODEL)
    for m in range(CONV_W - 1):
        tail_ref[0, m:m + 1, :] = c_ref[0, (m + 1) * SUBLANES - 1:(m + 1) * SUBLANES, :]


def _ffn_prompt(x2d, gains, g_final, w_up, w_down, cwb, layer, n_seq, final_norm):
    rows, d = x2d.shape
    seq_rows = rows // n_seq
    tile = min(seq_rows, FFN_ROW_TILE)
    tps = seq_rows // tile
    halo = (CONV_W - 1) * SUBLANES
    row_spec = pl.BlockSpec((tile, d), lambda b, t: (b * tps + t, 0))
    u_scratch = pltpu.VMEM((halo + tile, FFN_PROMPT_CHUNK), F32)
    z_scratch = pltpu.VMEM((tile, FFN_PROMPT_CHUNK), BF16)
    y, _, tail = pl.pallas_call(
        functools.partial(_ffn_prompt_body, final_norm=final_norm),
        grid=(n_seq, tps),
        in_specs=[row_spec, _layer_resident(gains.shape, layer), _resident((1, d)),
                  _layer_resident(w_up.shape, layer), _layer_resident(w_down.shape, layer),
                  _layer_resident(cwb.shape, layer)],
        out_specs=[row_spec, pl.BlockSpec((1, halo, 2 * D_FF), lambda b, t: (b, 0, 0)),
                   pl.BlockSpec((1, CONV_W - 1, 2 * D_FF), lambda b, t: (b, 0, 0))],
        out_shape=[jax.ShapeDtypeStruct((rows, d), F32), jax.ShapeDtypeStruct((n_seq, halo, 2 * D_FF), F32),
                   jax.ShapeDtypeStruct((n_seq, CONV_W - 1, 2 * D_FF), F32)],
        scratch_shapes=[pltpu.VMEM((tile, d), BF16), pltpu.VMEM((tile, d), F32),
                        u_scratch, u_scratch, u_scratch, u_scratch, z_scratch, z_scratch],
        compiler_params=_params(("arbitrary", "arbitrary")),
        name="ffn_prompt",
    )(x2d, gains, g_final.reshape(1, d), w_up, w_down, cwb)
    return y, tail


def _ffn_sample_body(x_ref, g_ref, gf_ref, wa_ref, wg_ref, wd_ref, cwa_ref, cwg_ref, ba_ref, bg_ref,
                     out_ref, ca_ref, cg_ref, h_ref, acc_ref, *, n_seq, seq_rows, final_norm):
    f = pl.program_id(0)

    @pl.when(f == 0)
    def _():
        xt = _to_time_major(x_ref[...], n_seq, seq_rows)
        h_ref[...] = _rms(xt, g_ref[0]).astype(BF16)
        acc_ref[...] = xt

    rows = n_seq * seq_rows
    hist = (CONV_W - 1) * n_seq

    def shift(u, prev, j):
        return jnp.concatenate([prev, u], axis=0)[hist - j * n_seq: hist - j * n_seq + rows]

    y, ua, ug = _ffn_chunk(h_ref[...], wa_ref[0], wg_ref[0], wd_ref[0], ba_ref[...], bg_ref[...],
                           cwa_ref[0], cwg_ref[0], shift)
    acc_ref[...] += y
    ca_ref[...] = ua[rows - hist:]
    cg_ref[...] = ug[rows - hist:]

    @pl.when(f == pl.num_programs(0) - 1)
    def _():
        yt = acc_ref[...]
        if final_norm:
            yt = _rms(yt, gf_ref[...])
        out_ref[...] = _from_time_major(yt, n_seq, seq_rows)


def _ffn_sample(x2d, gains, g_final, w_up, w_down, cwb, layer, buf, n_seq, final_norm):
    rows, d = x2d.shape
    seq_rows = rows // n_seq
    hist = (CONV_W - 1) * n_seq
    a_col = lambda f: f
    g_col = lambda f: NF + f
    c_shape = jax.ShapeDtypeStruct((hist, D_FF), F32)
    c_spec = pl.BlockSpec((hist, FF_CHUNK), lambda f: (0, f))
    return pl.pallas_call(
        functools.partial(_ffn_sample_body, n_seq=n_seq, seq_rows=seq_rows, final_norm=final_norm),
        grid=(NF,),
        in_specs=[_resident((rows, d)), _layer_resident(gains.shape, layer), _resident((1, d)),
                  pl.BlockSpec((1, d, FF_CHUNK), lambda f: (layer, 0, a_col(f))),
                  pl.BlockSpec((1, d, FF_CHUNK), lambda f: (layer, 0, g_col(f))),
                  pl.BlockSpec((1, FF_CHUNK, d), lambda f: (layer, f, 0)),
                  pl.BlockSpec((1, 4, FF_CHUNK), lambda f: (layer, 0, a_col(f))),
                  pl.BlockSpec((1, 4, FF_CHUNK), lambda f: (layer, 0, g_col(f))),
                  pl.BlockSpec((hist, FF_CHUNK), lambda f: (0, a_col(f))),
                  pl.BlockSpec((hist, FF_CHUNK), lambda f: (0, g_col(f)))],
        out_specs=[pl.BlockSpec((rows, d), lambda f: (0, 0)), c_spec, c_spec],
        out_shape=[jax.ShapeDtypeStruct((rows, d), F32), c_shape, c_shape],
        scratch_shapes=[pltpu.VMEM((rows, d), BF16), pltpu.VMEM((rows, d), F32)],
        compiler_params=_params(("arbitrary",)),
        name="ffn_sample",
    )(x2d, gains, g_final.reshape(1, d), w_up, w_up, w_down, cwb, cwb, buf, buf)


def _pool_mix(pooled, w_ref, scale_ref, x):
    mixed = [_dot(pooled[g].astype(BF16), w_ref[0, g]) for g in range(len(POOL_WINDOWS))]
    return x + jnp.concatenate(mixed, axis=-1) * scale_ref[0]


def _pool_window_tables(block):
    i = jnp.arange(block, dtype=jnp.int32)[:, None]
    back = (i + POOL_CARRY) - jnp.arange(2 * block, dtype=jnp.int32)[None, :]
    tabs = []
    for first in (True, False):
        per = []
        for w in POOL_WINDOWS:
            cnt = jnp.minimum(i + 1, w).astype(F32) if first else jnp.full((block, 1), float(w), F32)
            per.append(jnp.where((back >= 0) & (back < w), 1.0 / cnt, 0.0) - jnp.where(back == 0, 1.0, 0.0))
        tabs.append(jnp.stack(per))
    return jnp.stack(tabs).astype(BF16)


def _pool_prompt_body(x_ref, g_ref, w_ref, scale_ref, tab_ref, out_ref, hist_ref):
    t = pl.program_id(1)

    @pl.when(t == 0)
    def _():
        hist_ref[...] = jnp.zeros_like(hist_ref)

    x = x_ref[...]
    tile = x.shape[0]
    block = tab_ref.shape[2]
    h = _rms(x, g_ref[0])
    ext = jnp.concatenate([hist_ref[0], h, jnp.zeros((block - POOL_CARRY, x.shape[1]), F32)], axis=0).astype(BF16)
    hist_ref[0] = h[tile - POOL_CARRY:]
    for r0 in range(0, tile, block):
        which = jnp.where(t == 0, 0, 1) if r0 == 0 else 1
        e = ext[r0:r0 + 2 * block]
        pooled = [_dot(tab_ref[which, g], e[:, g * POOL_GC:(g + 1) * POOL_GC]) for g in range(len(POOL_WINDOWS))]
        out_ref[r0:r0 + block] = _pool_mix(pooled, w_ref, scale_ref, x[r0:r0 + block])


def _pool_prompt(x2d, gains, layer, pool_w, pool_scale, pool_layer, n_seq):
    rows, d = x2d.shape
    seq_rows = rows // n_seq
    tile = min(seq_rows, WIDE_ROW_TILE)
    tps = seq_rows // tile
    tabs = _pool_window_tables(min(tile, POOL_BLOCK))
    row_spec = pl.BlockSpec((tile, d), lambda b, t: (b * tps + t, 0))
    return pl.pallas_call(
        _pool_prompt_body,
        grid=(n_seq, tps),
        in_specs=[row_spec, _layer_resident(gains.shape, layer), _layer_resident(pool_w.shape, pool_layer),
                  _layer_resident(pool_scale.shape, pool_layer), _resident(tabs.shape)],
        out_specs=[row_spec, pl.BlockSpec((1, POOL_CARRY, d), lambda b, t: (b, 0, 0))],
        out_shape=[jax.ShapeDtypeStruct((rows, d), F32), jax.ShapeDtypeStruct((n_seq, POOL_CARRY, d), F32)],
        compiler_params=_params(("arbitrary", "arbitrary")),
        name="pool_prompt",
    )(x2d, gains, pool_w, pool_scale, tabs)


def _pool_sample_body(x_ref, g_ref, w_ref, scale_ref, buf_ref, out_ref, h_ref, *, n_seq, seq_rows, pos0):
    xt = _to_time_major(x_ref[...], n_seq, seq_rows)
    h = _rms(xt, g_ref[0])
    s = jnp.concatenate([buf_ref[...], h], axis=0)
    rows = n_seq * seq_rows
    first = POOL_BUF
    pooled = []
    for g, w in enumerate(POOL_WINDOWS):
        s = s[:, POOL_GC * (1 if g else 0):]
        step = (w // 2) * n_seq
        s = s[step:] + s[:-step]
        first -= w // 2
        win = s[first * n_seq: first * n_seq + rows, :POOL_GC]
        inv = [1.0 / min(pos0 + t + 1, w) for t in range(seq_rows)]
        if len(set(inv)) == 1:
            win = win * inv[0]
        else:
            win = jnp.concatenate([win[t * n_seq:(t + 1) * n_seq] * inv[t] for t in range(seq_rows)], axis=0)
        pooled.append(win - h[:, g * POOL_GC:(g + 1) * POOL_GC])
    out_ref[...] = _from_time_major(_pool_mix(pooled, w_ref, scale_ref, xt), n_seq, seq_rows)
    h_ref[...] = _from_time_major(h, n_seq, seq_rows)


def _pool_sample(x2d, gains, layer, pool_w, pool_scale, pool_layer, buf_t, n_seq, pos0):
    rows, d = x2d.shape
    seq_rows = rows // n_seq
    full = lambda s: pl.BlockSpec(s, lambda i: (0,) * len(s))
    return pl.pallas_call(
        functools.partial(_pool_sample_body, n_seq=n_seq, seq_rows=seq_rows, pos0=pos0),
        grid=(1,),
        in_specs=[full((rows, d)), _layer_resident(gains.shape, layer), _layer_resident(pool_w.shape, pool_layer),
                  _layer_resident(pool_scale.shape, pool_layer), full(buf_t.shape)],
        out_specs=[full((rows, d)), full((rows, d))],
        out_shape=[jax.ShapeDtypeStruct((rows, d), F32)] * 2,
        compiler_params=_params(("arbitrary",)),
        name="pool_sample",
    )(x2d, gains, pool_w, pool_scale, buf_t)


def _rope_tables(pos):
    inv = 1.0 / (ROPE_BASE ** (jnp.arange(0, RET_DK, 2, dtype=F32) / RET_DK))
    ang = pos[:, None] * inv[None, :]
    return jnp.cos(ang), jnp.sin(ang)


def kernel(x_prompt, x_sample, mem_prompt, cache_mem_k, cache_mem_v, state_ret, cache_pool, cache_ffn_conv, w_ret_in, ret_gn, w_ret_out, pool_w, pool_scale, norm_mem, w_xq, w_xk, w_xv, w_xo, w_up, conv_w, conv_b, w_down, norm_mix, norm_xattn, norm_ffn, norm_final):
    bp, lp, d = x_prompt.shape
    bs, ls, _ = x_sample.shape
    depth = w_up.shape[0]
    n_mem = mem_prompt.shape[1]

    w_in_b, w_out_b, pool_w_b, w_up_b, w_down_b = (
        w.astype(BF16) for w in (w_ret_in, w_ret_out, pool_w, w_up, w_down))
    cwb = jnp.concatenate([conv_w, conv_b[:, None, :]], axis=1)
    g_mix, g_x, g_ffn = _gain(norm_mix), _gain(norm_xattn), _gain(norm_ffn)
    gn_g, p_scale = _gain(ret_gn), _gain(pool_scale)

    mem_k2, mem_v2 = _mem_kv(mem_prompt.reshape(bp * n_mem, d), norm_mem, w_xk, w_xv)
    kv_p = tuple(a.reshape(depth * bp, n_mem * KV_ROW_STRIDE, LANES) for a in (mem_k2, mem_v2))
    kv_s = (_kv_rows(cache_mem_k), _kv_rows(cache_mem_v))
    states = state_ret.reshape((state_ret.shape[0] * bs,) + state_ret.shape[2:])

    cos_p, sin_p = _rope_tables(jnp.arange(lp, dtype=F32))
    cos_s, sin_s = _rope_tables(PAST_LEN + jnp.arange(ls, dtype=F32))
    cos_s, sin_s = jnp.tile(cos_s, (bs, 1)), jnp.tile(sin_s, (bs, 1))

    xp = x_prompt.reshape(bp * lp, d)
    xs = x_sample.reshape(bs * ls, d)
    ret_p, ret_s, pool_p, pool_s, conv_p, conv_s = [], [], [], [], [], []
    for i in range(depth):
        j = i // 2
        last = i == depth - 1
        if i % 2 == 0:
            q, k, v, gate = _ret_proj(xp, g_mix, i, cos_p, sin_p, w_in_b, j, bp, BF16)
            xp, s = _ret_prompt(q, k, v, gate, xp, gn_g, w_out_b, j, bp)
            ret_p.append(s)
            q, k, v, gate = _ret_proj(xs, g_mix, i, cos_s, sin_s, w_in_b, j, 1, F32)
            o, s = _ret_sample(q, k, v, states, j, bs, ls)
            xs = _ret_out(o, gate, xs, gn_g, w_out_b, j)
            ret_s.append(s)
        else:
            xp, hist = _pool_prompt(xp, g_mix, i, pool_w_b, p_scale, j, bp)
            pool_p.append(hist[:, POOL_CARRY - POOL_BUF:])
            buf_t = jnp.swapaxes(cache_pool[j], 0, 1).reshape(POOL_BUF * bs, d)
            xs, hs = _pool_sample(xs, g_mix, i, pool_w_b, p_scale, j, buf_t, bs, PAST_LEN)
            pool_s.append(jnp.concatenate([cache_pool[j], hs.reshape(bs, ls, d)], axis=1)[:, -POOL_BUF:])

        xp, xs = _xattn(xp, xs, g_x, w_xq, w_xo, i, kv_p, kv_s, bp, bs)

        xp, tail = _ffn_prompt(xp, g_ffn, norm_final, w_up_b, w_down_b, cwb, i, bp, last)
        conv_p.append(tail)
        buf = jnp.swapaxes(cache_ffn_conv[i], 0, 1).reshape((CONV_W - 1) * bs, 2 * D_FF)
        xs, ca, cg = _ffn_sample(xs, g_ffn, norm_final, w_up_b, w_down_b, cwb, i, buf, bs, last)
        tail = jnp.concatenate([ca, cg], axis=-1)
        conv_s.append(jnp.swapaxes(tail.reshape(CONV_W - 1, bs, 2 * D_FF), 0, 1))

    mk = _kv_unrows(mem_k2, (depth, bp), n_mem)
    mv = _kv_unrows(mem_v2, (depth, bp), n_mem)
    return (xp.reshape(bp, lp, d), xs.reshape(bs, ls, d), jnp.stack(ret_p), jnp.stack(ret_s).astype(state_ret.dtype),
            jnp.stack(pool_p), jnp.stack(pool_s), jnp.stack(conv_p), jnp.stack(conv_s), mk, mv)
```

```python
import functools

import jax
import jax.numpy as jnp
from jax import lax
from jax.experimental import pallas as pl
from jax.experimental.pallas import tpu as pltpu

F32 = jnp.float32
BF16 = jnp.bfloat16

D_MODEL = 1024
PAST_LEN = 16384
RET_HEADS = 4
RET_DK = D_MODEL // RET_HEADS
RET_DV = 2 * D_MODEL // RET_HEADS
HK = RET_HEADS * RET_DK
HV = RET_HEADS * RET_DV
ROPE_BASE = 10000.0
POOL_WINDOWS = (2, 4, 8, 16)
POOL_GC = D_MODEL // len(POOL_WINDOWS)
POOL_BUF = max(POOL_WINDOWS) - 1
X_HEADS = 4
X_HEAD_DIM = D_MODEL // X_HEADS
D_FF = 2816
CONV_W = 3
NORM_EPS = 1e-6
GN_EPS = 1e-5

SUBLANES = 8
LANES = 128
KV_LANE_TILES = X_HEAD_DIM // LANES
KV_ROW_STRIDE = X_HEADS * KV_LANE_TILES
assert KV_ROW_STRIDE == SUBLANES
ROW_TILE = 512
FFN_ROW_TILE = 512
WIDE_ROW_TILE = 1024
RET_CHUNK_PROMPT = 256
FF_CHUNK = 1408
NF = D_FF // FF_CHUNK
FFN_PROMPT_CHUNK = 1536
FFN_GATE_ROWS = 64
POOL_BLOCK = 128
POOL_CARRY = 16
VMEM_LIMIT = 56 * 1024 * 1024


def _params(sem):
    return pltpu.CompilerParams(dimension_semantics=sem, vmem_limit_bytes=VMEM_LIMIT)


def _resident(shape):
    zeros = (0,) * len(shape)
    return pl.BlockSpec(shape, lambda *_: zeros, pipeline_mode=pl.Buffered(1))


def _layer_resident(stacked_shape, layer):
    idx = (layer,) + (0,) * (len(stacked_shape) - 1)
    return pl.BlockSpec((1,) + tuple(stacked_shape[1:]), lambda *_: idx, pipeline_mode=pl.Buffered(1))


def _gain(g):
    return g.reshape(g.shape[0], 1, g.shape[1])


def _dot(a, b):
    return jnp.dot(a, b, preferred_element_type=F32)


def _dot_nt(a, b):
    return lax.dot_general(a, b, (((1,), (1,)), ((), ())), preferred_element_type=F32)


def _dot_tn(a, b):
    return lax.dot_general(a, b, (((0,), (0,)), ((), ())), preferred_element_type=F32)


def _rms(x, g):
    return x * lax.rsqrt(jnp.mean(x * x, axis=-1, keepdims=True) + NORM_EPS) * g


def _silu(x):
    return x * (1.0 / (1.0 + jnp.exp(-x)))


def _to_time_major(x, n_seq, seq_rows):
    return jnp.swapaxes(x.reshape(n_seq, seq_rows, x.shape[-1]), 0, 1).reshape(x.shape)


def _from_time_major(x, n_seq, seq_rows):
    return jnp.swapaxes(x.reshape(seq_rows, n_seq, x.shape[-1]), 0, 1).reshape(x.shape)


def _memkv_body(mem_ref, g_ref, wk_ref, wv_ref, k_ref, v_ref):
    h = _rms(mem_ref[...], g_ref[0]).astype(BF16)
    for w_ref, o_ref in ((wk_ref, k_ref), (wv_ref, v_ref)):
        y = _dot(h, w_ref[0].astype(BF16))
        for hd in range(X_HEADS):
            for c in range(KV_LANE_TILES):
                col = hd * X_HEAD_DIM + c * LANES
                o_ref[0, pl.ds(c * X_HEADS + hd, y.shape[0], stride=KV_ROW_STRIDE), :] = y[:, col:col + LANES]


def _mem_kv(mem2d, norm_mem, wk, wv):
    rows, d = mem2d.shape
    depth = wk.shape[0]
    tile = min(rows, 1024)
    w_spec = pl.BlockSpec((1, d, d), lambda i, r: (i, 0, 0))
    o_spec = pl.BlockSpec((1, tile * KV_ROW_STRIDE, LANES), lambda i, r: (i, r, 0))
    return pl.pallas_call(
        _memkv_body,
        grid=(depth, rows // tile),
        in_specs=[pl.BlockSpec((tile, d), lambda i, r: (r, 0)),
                  pl.BlockSpec((1, 1, d), lambda i, r: (i, 0, 0)), w_spec, w_spec],
        out_specs=[o_spec, o_spec],
        out_shape=[jax.ShapeDtypeStruct((depth, rows * KV_ROW_STRIDE, LANES), F32)] * 2,
        compiler_params=_params(("arbitrary", "arbitrary")),
        name="mem_kv",
    )(mem2d, _gain(norm_mem), wk, wv)


def _kv_rows(a):
    lead, n_mem = a.shape[:-3], a.shape[-3]
    a = a.reshape(lead + (n_mem, X_HEADS, KV_LANE_TILES, LANES))
    a = jnp.swapaxes(a, -3, -2)
    return a.reshape((-1, n_mem * KV_ROW_STRIDE, LANES))


def _kv_unrows(a, lead, n_mem):
    a = a.reshape(lead + (n_mem, KV_LANE_TILES, X_HEADS, LANES))
    a = jnp.swapaxes(a, -3, -2)
    return a.reshape(lead + (n_mem, X_HEADS, X_HEAD_DIM))


def _head_kv(ref, i, hd, n_mem):
    parts = [ref[i, pl.ds(c * X_HEADS + hd, n_mem, stride=KV_ROW_STRIDE), :] for c in range(KV_LANE_TILES)]
    return jnp.concatenate(parts, axis=1).astype(BF16)


def _retproj_body(x_ref, g_ref, cos_ref, sin_ref, w_ref, q_ref, k_ref, v_ref, gate_ref):
    h = _rms(x_ref[...], g_ref[0]).astype(BF16)
    cos = cos_ref[...]
    sin = sin_ref[...]
    half = RET_DK // 2

    def rope_store(col0, out_ref, scale):
        p = _dot(h, w_ref[0, :, col0:col0 + HK])
        for hd in range(RET_HEADS):
            a = p[:, hd * RET_DK: hd * RET_DK + half]
            b = p[:, hd * RET_DK + half: (hd + 1) * RET_DK]
            out_ref[:, hd * RET_DK: hd * RET_DK + half] = ((a * cos - b * sin) * scale).astype(out_ref.dtype)
            out_ref[:, hd * RET_DK + half: (hd + 1) * RET_DK] = ((b * cos + a * sin) * scale).astype(out_ref.dtype)

    rope_store(0, q_ref, 1.0)
    rope_store(HK, k_ref, RET_DK ** -0.5)
    v_ref[...] = _dot(h, w_ref[0, :, 2 * HK: 2 * HK + HV]).astype(v_ref.dtype)
    gate_ref[...] = _dot(h, w_ref[0, :, 2 * HK + HV:]).astype(gate_ref.dtype)


def _ret_proj(x2d, gains, layer, cos, sin, w_in, ret_layer, n_seq, out_dtype):
    rows, d = x2d.shape
    seq_rows = rows // n_seq
    tile = min(seq_rows, WIDE_ROW_TILE)
    tps = seq_rows // tile
    row_spec = lambda w: pl.BlockSpec((tile, w), lambda b, t: (b * tps + t, 0))
    tab_spec = pl.BlockSpec((tile, RET_DK // 2), lambda b, t: (t, 0))
    return pl.pallas_call(
        _retproj_body,
        grid=(n_seq, tps),
        in_specs=[row_spec(d), _layer_resident(gains.shape, layer), tab_spec, tab_spec,
                  _layer_resident(w_in.shape, ret_layer)],
        out_specs=[row_spec(HK), row_spec(HK), row_spec(HV), row_spec(HV)],
        out_shape=[jax.ShapeDtypeStruct((rows, HK), out_dtype), jax.ShapeDtypeStruct((rows, HK), out_dtype),
                   jax.ShapeDtypeStruct((rows, HV), out_dtype), jax.ShapeDtypeStruct((rows, HV), out_dtype)],
        compiler_params=_params(("arbitrary", "arbitrary")),
        name="ret_proj",
    )(x2d, gains, cos, sin, w_in)


def _ret_units(q_ref, k_ref, v_ref, s_in, s_out, o_ref, tables, row_blocks, lookahead):
    inner_ref, qdec_ref, kdec_ref, cdec_ref = tables
    units = [(rows, i, h) for rows, i in row_blocks for h in range(RET_HEADS)]

    def issue(rows, i, h):
        qb = q_ref[rows, h * RET_DK:(h + 1) * RET_DK].astype(BF16)
        kb = k_ref[rows, h * RET_DK:(h + 1) * RET_DK].astype(BF16)
        return _dot_nt(qb, kb), _dot(qb, s_in[i, h].astype(BF16))

    def finish(rows, i, h, qk, qs):
        kh = k_ref[rows, h * RET_DK:(h + 1) * RET_DK]
        vb = v_ref[rows, h * RET_DV:(h + 1) * RET_DV].astype(BF16)
        p = (qk * inner_ref[h]).astype(BF16)
        o_ref[rows, h * RET_DV:(h + 1) * RET_DV] = _dot(p, vb) + qs * qdec_ref[h]
        kd = (kh.astype(F32) * kdec_ref[h]).astype(BF16)
        s_out[i, h] = s_in[i, h] * cdec_ref[h] + _dot_tn(kd, vb)

    if not lookahead:
        for unit in units:
            finish(*unit, *issue(*unit))
        return
    nxt = issue(*units[0])
    for n, unit in enumerate(units):
        cur = nxt
        if n + 1 < len(units):
            nxt = issue(*units[n + 1])
        finish(*unit, *cur)


def _decay_tables(chunk):
    lg = jnp.log(1.0 - 2.0 ** (-5.0 - jnp.arange(RET_HEADS, dtype=F32)))
    idx = jnp.arange(chunk, dtype=F32)
    rel = idx[:, None] - idx[None, :]
    inner = jnp.where(rel[None] >= 0, jnp.exp(jnp.maximum(rel, 0.0)[None] * lg[:, None, None]), 0.0)
    q_dec = jnp.exp((idx + 1.0)[None, :] * lg[:, None])[..., None]
    k_dec = jnp.exp((chunk - 1.0 - idx)[None, :] * lg[:, None])[..., None]
    c_dec = jnp.exp(chunk * lg)[:, None, None]
    return (inner, jnp.broadcast_to(q_dec, (RET_HEADS, chunk, RET_DV)),
            jnp.broadcast_to(k_dec, (RET_HEADS, chunk, RET_DK)),
            jnp.broadcast_to(c_dec, (RET_HEADS, 1, RET_DV)))


def _gn_gate(o, gate, gn):
    parts = []
    for h in range(RET_HEADS):
        oh = o[:, h * RET_DV:(h + 1) * RET_DV]
        mu = jnp.mean(oh, axis=-1, keepdims=True)
        dlt = oh - mu
        var = jnp.mean(dlt * dlt, axis=-1, keepdims=True)
        parts.append(dlt * lax.rsqrt(var + GN_EPS))
    on = jnp.concatenate(parts, axis=-1) * gn
    return (_silu(gate.astype(F32)) * on).astype(BF16)


def _ret_prompt_body(q_ref, k_ref, v_ref, gate_ref, x_ref, gn_ref, w_ref, inner_ref, qdec_ref, kdec_ref, cdec_ref,
                     out_ref, s_ref, o_ref, *, chunk):
    @pl.when(pl.program_id(1) == 0)
    def _():
        s_ref[...] = jnp.zeros_like(s_ref)

    chunks = [slice(c * chunk, (c + 1) * chunk) for c in range(q_ref.shape[0] // chunk)]
    _ret_units(q_ref, k_ref, v_ref, s_ref, s_ref, o_ref, (inner_ref, qdec_ref, kdec_ref, cdec_ref),
               [(rows, 0) for rows in chunks], lookahead=False)
    for rows in chunks:
        z = _gn_gate(o_ref[rows], gate_ref[rows], gn_ref[0])
        out_ref[rows] = x_ref[rows] + _dot(z, w_ref[0])


def _ret_prompt(q, k, v, gate, x2d, gn_g, w_out, ret_layer, n_seq):
    rows, d = x2d.shape
    seq_rows = rows // n_seq
    tile = min(seq_rows, ROW_TILE)
    chunk = min(tile, RET_CHUNK_PROMPT)
    tps = seq_rows // tile
    tabs = _decay_tables(chunk)
    row_spec = lambda w: pl.BlockSpec((tile, w), lambda b, t: (b * tps + t, 0))
    return pl.pallas_call(
        functools.partial(_ret_prompt_body, chunk=chunk),
        grid=(n_seq, tps),
        in_specs=[row_spec(HK), row_spec(HK), row_spec(HV), row_spec(HV), row_spec(d),
                  _layer_resident(gn_g.shape, ret_layer), _layer_resident(w_out.shape, ret_layer)]
                 + [_resident(t.shape) for t in tabs],
        out_specs=[row_spec(d), pl.BlockSpec((1, RET_HEADS, RET_DK, RET_DV), lambda b, t: (b, 0, 0, 0))],
        out_shape=[jax.ShapeDtypeStruct((rows, d), F32),
                   jax.ShapeDtypeStruct((n_seq, RET_HEADS, RET_DK, RET_DV), F32)],
        scratch_shapes=[pltpu.VMEM((tile, HV), F32)],
        compiler_params=_params(("arbitrary", "arbitrary")),
        name="ret_prompt",
    )(q, k, v, gate, x2d, gn_g, w_out, *tabs)


def _ret_sample_body(q_ref, k_ref, v_ref, s0_ref, inner_ref, qdec_ref, kdec_ref, cdec_ref, o_ref, s_ref, *, seq_rows):
    _ret_units(q_ref, k_ref, v_ref, s0_ref, s_ref, o_ref, (inner_ref, qdec_ref, kdec_ref, cdec_ref),
               [(slice(i * seq_rows, (i + 1) * seq_rows), i) for i in range(s0_ref.shape[0])], lookahead=True)


def _ret_sample(q, k, v, states, ret_layer, n_seq, seq_rows):
    bb = 4 if n_seq % 4 == 0 else 1
    steps = n_seq // bb
    tabs = _decay_tables(seq_rows)
    row_spec = lambda w: pl.BlockSpec((bb * seq_rows, w), lambda i: (i, 0))
    st_block = (bb, RET_HEADS, RET_DK, RET_DV)
    return pl.pallas_call(
        functools.partial(_ret_sample_body, seq_rows=seq_rows),
        grid=(steps,),
        in_specs=[row_spec(HK), row_spec(HK), row_spec(HV),
                  pl.BlockSpec(st_block, lambda i: (ret_layer * steps + i, 0, 0, 0))] + [_resident(t.shape) for t in tabs],
        out_specs=[row_spec(HV), pl.BlockSpec(st_block, lambda i: (i, 0, 0, 0))],
        out_shape=[jax.ShapeDtypeStruct((n_seq * seq_rows, HV), F32),
                   jax.ShapeDtypeStruct((n_seq, RET_HEADS, RET_DK, RET_DV), F32)],
        compiler_params=_params(("arbitrary",)),
        name="ret_sample",
    )(q, k, v, states, *tabs)


def _retout_body(o_ref, gate_ref, x_ref, gn_ref, w_ref, out_ref):
    out_ref[...] = x_ref[...] + _dot(_gn_gate(o_ref[...], gate_ref[...], gn_ref[0]), w_ref[0])


def _ret_out(o, gate, x2d, gn_g, w_out, ret_layer):
    rows, d = x2d.shape
    tile = min(rows, ROW_TILE)
    row_spec = lambda w: pl.BlockSpec((tile, w), lambda r: (r, 0))
    return pl.pallas_call(
        _retout_body,
        grid=(rows // tile,),
        in_specs=[row_spec(HV), row_spec(HV), row_spec(d), _layer_resident(gn_g.shape, ret_layer),
                  _layer_resident(w_out.shape, ret_layer)],
        out_specs=row_spec(d),
        out_shape=jax.ShapeDtypeStruct((rows, d), F32),
        compiler_params=_params(("arbitrary",)),
        name="ret_out",
    )(o, gate, x2d, gn_g, w_out)


def _softmax(s):
    e = jnp.exp(s - jnp.max(s, axis=-1, keepdims=True))
    return e / jnp.sum(e, axis=-1, keepdims=True)


def _xattn_block(x_ref, g_ref, wq_ref, wo_ref, keys, values, out_ref, att_ref, *s_refs, n_seq, seq_rows):
    head_cols = [slice(hd * X_HEAD_DIM, (hd + 1) * X_HEAD_DIM) for hd in range(X_HEADS)]
    h = _rms(x_ref[...], g_ref[0]).astype(BF16)
    q = _dot(h, wq_ref[...]) * (X_HEAD_DIM ** -0.5)
    if s_refs:
        s_ref, = s_refs
        pairs = [(slice(i * seq_rows, (i + 1) * seq_rows), cols, i, hd)
                 for i in range(n_seq) for hd, cols in enumerate(head_cols)]
        for n, (rows, cols, i, hd) in enumerate(pairs):
            s_ref[n * seq_rows:(n + 1) * seq_rows] = _dot_nt(q[rows, cols].astype(BF16), keys(i, hd))
        s_ref[...] = _softmax(s_ref[...])
        for n, (rows, cols, i, hd) in enumerate(pairs):
            att_ref[rows, cols] = _dot(s_ref[n * seq_rows:(n + 1) * seq_rows].astype(BF16), values(i, hd))
    else:
        scores = lambda hd: _dot_nt(q[:, head_cols[hd]].astype(BF16), keys(0, hd))
        nxt = scores(0)
        for hd, cols in enumerate(head_cols):
            s = nxt
            if hd + 1 < X_HEADS:
                nxt = scores(hd + 1)
            att_ref[:, cols] = _dot(_softmax(s).astype(BF16), values(0, hd))
    out_ref[...] = x_ref[...] + _dot(att_ref[...].astype(BF16), wo_ref[...])


def _xattn_body(*refs, long_steps, short_steps, tiles_per_seq, short_seqs, short_rows, pool):
    if pool:
        (xl_ref, xs_ref, g_ref, wq_ref, wo_ref, kl_ref, vl_ref, ks_ref, vs_ref, gm_ref, pw_ref, ps_ref, tab_ref,
         outl_ref, outs_ref, hist_ref, attl_ref, atts_ref, score_ref, wqb_ref, wob_ref, kvl_ref, xmid_ref) = refs
    else:
        (xl_ref, xs_ref, g_ref, wq_ref, wo_ref, kl_ref, vl_ref, ks_ref, vs_ref, outl_ref, outs_ref,
         attl_ref, atts_ref, score_ref, wqb_ref, wob_ref, kvl_ref) = refs
    step = pl.program_id(0)
    idx = step // 2
    n_mem = kl_ref.shape[1] // KV_ROW_STRIDE

    @pl.when(step == 0)
    def _():
        wqb_ref[...] = wq_ref[0].astype(BF16)
        wob_ref[...] = wo_ref[0].astype(BF16)

    @pl.when((step % 2 == 0) & (idx < long_steps))
    def _():
        @pl.when(idx % tiles_per_seq == 0)
        def _():
            for hd in range(X_HEADS):
                kvl_ref[hd] = _head_kv(kl_ref, 0, hd, n_mem)
                kvl_ref[X_HEADS + hd] = _head_kv(vl_ref, 0, hd, n_mem)

        x_in = xl_ref
        if pool:
            @pl.when(idx % tiles_per_seq == 0)
            def _():
                hist_ref[...] = jnp.zeros_like(hist_ref)

            x = xl_ref[...]
            tile, block = x.shape[0], tab_ref.shape[2]
            h = _rms(x, gm_ref[0])
            ext = jnp.concatenate([hist_ref[0], h, jnp.zeros((block - POOL_CARRY, x.shape[1]), F32)],
                                  axis=0).astype(BF16)
            hist_ref[0] = h[tile - POOL_CARRY:]
            for r0 in range(0, tile, block):
                which = jnp.where(idx % tiles_per_seq == 0, 0, 1) if r0 == 0 else 1
                e = ext[r0:r0 + 2 * block]
                pooled = [_dot(tab_ref[which, g], e[:, g * POOL_GC:(g + 1) * POOL_GC])
                          for g in range(len(POOL_WINDOWS))]
                xmid_ref[r0:r0 + block] = _pool_mix(pooled, pw_ref, ps_ref, x[r0:r0 + block])
            x_in = xmid_ref
        _xattn_block(x_in, g_ref, wqb_ref, wob_ref, lambda i, hd: kvl_ref[hd], lambda i, hd: kvl_ref[X_HEADS + hd],
                     outl_ref, attl_ref, n_seq=1, seq_rows=xl_ref.shape[0])

    @pl.when((step % 2 == 1) & (idx < short_steps))
    def _():
        _xattn_block(xs_ref, g_ref, wqb_ref, wob_ref, lambda i, hd: _head_kv(ks_ref, i, hd, n_mem),
                     lambda i, hd: _head_kv(vs_ref, i, hd, n_mem), outs_ref, atts_ref, score_ref,
                     n_seq=short_seqs, seq_rows=short_rows)


def _xattn(x_long, x_short, gains, wq, wo, layer, kv_long, kv_short, n_long, n_short, pool=None):
    d = x_long.shape[1]
    long_rows, short_rows = x_long.shape[0] // n_long, x_short.shape[0] // n_short
    assert long_rows % ROW_TILE == 0
    n_mem = kv_long[0].shape[1]
    tps = long_rows // ROW_TILE
    long_steps = n_long * tps
    bb = 4 if n_short % 4 == 0 else 1
    short_steps = n_short // bb
    li = lambda s: jnp.minimum(s // 2, long_steps - 1)
    si = lambda s: jnp.minimum(jnp.maximum(s - 1, 0) // 2, short_steps - 1)
    long_spec = pl.BlockSpec((ROW_TILE, d), lambda s: (li(s), 0))
    short_spec = pl.BlockSpec((bb * short_rows, d), lambda s: (si(s), 0))
    kvl_spec = pl.BlockSpec((1, n_mem, LANES), lambda s: (layer * n_long + li(s) // tps, 0, 0))
    kvs_spec = pl.BlockSpec((bb, n_mem, LANES), lambda s: (layer * short_steps + si(s), 0, 0))
    mem_tokens = n_mem // KV_ROW_STRIDE
    pool_in, pool_specs, pool_out_specs, pool_out_shape, pool_scratch = [], [], [], [], []
    if pool:
        gmix, mix_layer, pool_w, pool_scale, pool_layer = pool
        tabs = _pool_window_tables(POOL_BLOCK)
        pool_in = [gmix, pool_w, pool_scale, tabs]
        pool_specs = [_layer_resident(gmix.shape, mix_layer), _layer_resident(pool_w.shape, pool_layer),
                      _layer_resident(pool_scale.shape, pool_layer), _resident(tabs.shape)]
        pool_out_specs = [pl.BlockSpec((1, POOL_CARRY, d), lambda s: (li(s) // tps, 0, 0))]
        pool_out_shape = [jax.ShapeDtypeStruct((n_long, POOL_CARRY, d), F32)]
        pool_scratch = [pltpu.VMEM((ROW_TILE, d), F32)]
    return pl.pallas_call(
        functools.partial(_xattn_body, long_steps=long_steps, short_steps=short_steps, tiles_per_seq=tps,
                          short_seqs=bb, short_rows=short_rows, pool=bool(pool)),
        grid=(2 * max(long_steps, short_steps),),
        in_specs=[long_spec, short_spec, _layer_resident(gains.shape, layer), _layer_resident(wq.shape, layer),
                  _layer_resident(wo.shape, layer), kvl_spec, kvl_spec, kvs_spec, kvs_spec] + pool_specs,
        out_specs=[long_spec, short_spec] + pool_out_specs,
        out_shape=[jax.ShapeDtypeStruct(x_long.shape, F32), jax.ShapeDtypeStruct(x_short.shape, F32)] + pool_out_shape,
        scratch_shapes=[pltpu.VMEM((ROW_TILE, d), F32), pltpu.VMEM((bb * short_rows, d), F32),
                        pltpu.VMEM((bb * X_HEADS * short_rows, mem_tokens), F32),
                        pltpu.VMEM((d, d), BF16), pltpu.VMEM((d, d), BF16),
                        pltpu.VMEM((2 * X_HEADS, mem_tokens, X_HEAD_DIM), BF16)] + pool_scratch,
        compiler_params=_params(("arbitrary",)),
        name="xattn",
    )(x_long, x_short, gains, wq, wo, *kv_long, *kv_short, *pool_in)


def _conv_chunk(u, prev, cwb, shift):
    c = cwb[3:4] + cwb[0:1] * shift(u, prev, 2)
    c = c + cwb[1:2] * shift(u, prev, 1)
    return c + cwb[2:3] * u


def _ffn_gate(ua, ug, prev_a, prev_g, cwb_a, cwb_g, shift):
    return (_conv_chunk(ua, prev_a, cwb_a, shift) * _silu(_conv_chunk(ug, prev_g, cwb_g, shift))).astype(BF16)


def _ffn_chunk(h, wa, wg, wd, prev_a, prev_g, cwb_a, cwb_g, shift):
    ua = _dot(h, wa)
    ug = _dot(h, wg)
    return _dot(_ffn_gate(ua, ug, prev_a, prev_g, cwb_a, cwb_g, shift), wd), ua, ug


def _ffn_prompt_body(x_ref, g_ref, gf_ref, wu_ref, wd_ref, cwb_ref, out_ref, c_ref, tail_ref,
                     h_ref, acc_ref, ua0_ref, ug0_ref, ua1_ref, ug1_ref, z0_ref, z1_ref, *, final_norm):
    halo = (CONV_W - 1) * SUBLANES
    tile = acc_ref.shape[0]
    groups = tile // SUBLANES

    @pl.when(pl.program_id(1) == 0)
    def _():
        c_ref[...] = jnp.zeros_like(c_ref)

    x = jnp.swapaxes(x_ref[...].reshape(SUBLANES, groups, D_MODEL), 0, 1).reshape(tile, D_MODEL)
    h_ref[...] = _rms(x, g_ref[0]).astype(BF16)
    acc_ref[...] = x
    slots = ((ua0_ref, ug0_ref), (ua1_ref, ug1_ref))
    starts = range(0, D_FF, FFN_PROMPT_CHUNK)
    n_chunks = len(starts)

    def cols(f):
        lo, hi = starts[f], min(starts[f] + FFN_PROMPT_CHUNK, D_FF)
        return slice(lo, hi), slice(D_FF + lo, D_FF + hi), hi - lo

    def up(f):
        ca, cg, width = cols(f)
        first_sublane = lax.broadcasted_iota(jnp.int32, (SUBLANES, width), 0) == 0
        for u_ref, cc in zip(slots[f % 2], (ca, cg)):
            u = _dot(h_ref[...], wu_ref[0, :, cc])
            u_ref[halo:, :width] = u
            for m in range(CONV_W - 1):
                rows = slice(m * SUBLANES, (m + 1) * SUBLANES)
                src = u[tile - halo + m * SUBLANES: tile - halo + (m + 1) * SUBLANES]
                u_ref[rows, :width] = jnp.where(first_sublane, pltpu.roll(c_ref[0, rows, cc], 1, axis=0),
                                                pltpu.roll(src, 1, axis=0))
                c_ref[0, rows, cc] = src

    def conv(u_ref, cw, r0, n, width):
        c = cw[3:4] + cw[0:1] * u_ref[r0: r0 + n, :width]
        c = c + cw[1:2] * u_ref[r0 + SUBLANES: r0 + SUBLANES + n, :width]
        return c + cw[2:3] * u_ref[r0 + halo: r0 + halo + n, :width]

    def gate(f):
        ca, cg, width = cols(f)
        ua_ref, ug_ref = slots[f % 2]
        cwa, cwg = cwb_ref[0, :, ca], cwb_ref[0, :, cg]
        z_ref = (z0_ref, z1_ref)[f % 2]
        for r0 in range(0, tile, FFN_GATE_ROWS):
            z_ref[r0: r0 + FFN_GATE_ROWS, :width] = (
                conv(ua_ref, cwa, r0, FFN_GATE_ROWS, width)
                * _silu(conv(ug_ref, cwg, r0, FFN_GATE_ROWS, width))).astype(BF16)
        return z_ref[:, :width]

    up(0)
    for f in range(n_chunks):
        if f + 1 < n_chunks:
            up(f + 1)
        acc_ref[...] += _dot(gate(f), wd_ref[0, cols(f)[0], :])
    y = acc_ref[...]
    if final_norm:
        y = _rms(y, gf_ref[...])
    out_ref[...] = jnp.swapaxes(y.reshape(groups, SUBLANES, D_MODEL), 0, 1).reshape(tile, D_MODEL)
    for m in range(CONV_W - 1):
        tail_ref[0, m:m + 1, :] = c_ref[0, (m + 1) * SUBLANES - 1:(m + 1) * SUBLANES, :]


def _ffn_prompt(x2d, gains, g_final, w_up, w_down, cwb, layer, n_seq, final_norm):
    rows, d = x2d.shape
    seq_rows = rows // n_seq
    tile = min(seq_rows, FFN_ROW_TILE)
    tps = seq_rows // tile
    halo = (CONV_W - 1) * SUBLANES
    row_spec = pl.BlockSpec((tile, d), lambda b, t: (b * tps + t, 0))
    u_scratch = pltpu.VMEM((halo + tile, FFN_PROMPT_CHUNK), F32)
    z_scratch = pltpu.VMEM((tile, FFN_PROMPT_CHUNK), BF16)
    y, _, tail = pl.pallas_call(
        functools.partial(_ffn_prompt_body, final_norm=final_norm),
        grid=(n_seq, tps),
        in_specs=[row_spec, _layer_resident(gains.shape, layer), _resident((1, d)),
                  _layer_resident(w_up.shape, layer), _layer_resident(w_down.shape, layer),
                  _layer_resident(cwb.shape, layer)],
        out_specs=[row_spec, pl.BlockSpec((1, halo, 2 * D_FF), lambda b, t: (b, 0, 0)),
                   pl.BlockSpec((1, CONV_W - 1, 2 * D_FF), lambda b, t: (b, 0, 0))],
        out_shape=[jax.ShapeDtypeStruct((rows, d), F32), jax.ShapeDtypeStruct((n_seq, halo, 2 * D_FF), F32),
                   jax.ShapeDtypeStruct((n_seq, CONV_W - 1, 2 * D_FF), F32)],
        scratch_shapes=[pltpu.VMEM((tile, d), BF16), pltpu.VMEM((tile, d), F32),
                        u_scratch, u_scratch, u_scratch, u_scratch, z_scratch, z_scratch],
        compiler_params=_params(("arbitrary", "arbitrary")),
        name="ffn_prompt",
    )(x2d, gains, g_final.reshape(1, d), w_up, w_down, cwb)
    return y, tail


def _ffn_sample_body(x_ref, g_ref, gf_ref, wa_ref, wg_ref, wd_ref, cwa_ref, cwg_ref, ba_ref, bg_ref,
                     out_ref, ca_ref, cg_ref, h_ref, acc_ref, *, n_seq, seq_rows, final_norm):
    f = pl.program_id(0)

    @pl.when(f == 0)
    def _():
        xt = _to_time_major(x_ref[...], n_seq, seq_rows)
        h_ref[...] = _rms(xt, g_ref[0]).astype(BF16)
        acc_ref[...] = xt

    rows = n_seq * seq_rows
    hist = (CONV_W - 1) * n_seq

    def shift(u, prev, j):
        return jnp.concatenate([prev, u], axis=0)[hist - j * n_seq: hist - j * n_seq + rows]

    y, ua, ug = _ffn_chunk(h_ref[...], wa_ref[0], wg_ref[0], wd_ref[0], ba_ref[...], bg_ref[...],
                           cwa_ref[0], cwg_ref[0], shift)
    acc_ref[...] += y
    ca_ref[...] = ua[rows - hist:]
    cg_ref[...] = ug[rows - hist:]

    @pl.when(f == pl.num_programs(0) - 1)
    def _():
        yt = acc_ref[...]
        if final_norm:
            yt = _rms(yt, gf_ref[...])
        out_ref[...] = _from_time_major(yt, n_seq, seq_rows)


def _ffn_sample(x2d, gains, g_final, w_up, w_down, cwb, layer, buf, n_seq, final_norm):
    rows, d = x2d.shape
    seq_rows = rows // n_seq
    hist = (CONV_W - 1) * n_seq
    a_col = lambda f: f
    g_col = lambda f: NF + f
    c_shape = jax.ShapeDtypeStruct((hist, D_FF), F32)
    c_spec = pl.BlockSpec((hist, FF_CHUNK), lambda f: (0, f))
    return pl.pallas_call(
        functools.partial(_ffn_sample_body, n_seq=n_seq, seq_rows=seq_rows, final_norm=final_norm),
        grid=(NF,),
        in_specs=[_resident((rows, d)), _layer_resident(gains.shape, layer), _resident((1, d)),
                  pl.BlockSpec((1, d, FF_CHUNK), lambda f: (layer, 0, a_col(f))),
                  pl.BlockSpec((1, d, FF_CHUNK), lambda f: (layer, 0, g_col(f))),
                  pl.BlockSpec((1, FF_CHUNK, d), lambda f: (layer, f, 0)),
                  pl.BlockSpec((1, 4, FF_CHUNK), lambda f: (layer, 0, a_col(f))),
                  pl.BlockSpec((1, 4, FF_CHUNK), lambda f: (layer, 0, g_col(f))),
                  pl.BlockSpec((hist, FF_CHUNK), lambda f: (0, a_col(f))),
                  pl.BlockSpec((hist, FF_CHUNK), lambda f: (0, g_col(f)))],
        out_specs=[pl.BlockSpec((rows, d), lambda f: (0, 0)), c_spec, c_spec],
        out_shape=[jax.ShapeDtypeStruct((rows, d), F32), c_shape, c_shape],
        scratch_shapes=[pltpu.VMEM((rows, d), BF16), pltpu.VMEM((rows, d), F32)],
        compiler_params=_params(("arbitrary",)),
        name="ffn_sample",
    )(x2d, gains, g_final.reshape(1, d), w_up, w_up, w_down, cwb, cwb, buf, buf)


def _pool_mix(pooled, w_ref, scale_ref, x):
    mixed = [_dot(pooled[g].astype(BF16), w_ref[0, g]) for g in range(len(POOL_WINDOWS))]
    return x + jnp.concatenate(mixed, axis=-1) * scale_ref[0]


def _pool_window_tables(block):
    i = jnp.arange(block, dtype=jnp.int32)[:, None]
    back = (i + POOL_CARRY) - jnp.arange(2 * block, dtype=jnp.int32)[None, :]
    tabs = []
    for first in (True, False):
        per = []
        for w in POOL_WINDOWS:
            cnt = jnp.minimum(i + 1, w).astype(F32) if first else jnp.full((block, 1), float(w), F32)
            per.append(jnp.where((back >= 0) & (back < w), 1.0 / cnt, 0.0) - jnp.where(back == 0, 1.0, 0.0))
        tabs.append(jnp.stack(per))
    return jnp.stack(tabs).astype(BF16)


def _pool_prompt_body(x_ref, g_ref, w_ref, scale_ref, tab_ref, out_ref, hist_ref):
    t = pl.program_id(1)

    @pl.when(t == 0)
    def _():
        hist_ref[...] = jnp.zeros_like(hist_ref)

    x = x_ref[...]
    tile = x.shape[0]
    block = tab_ref.shape[2]
    h = _rms(x, g_ref[0])
    ext = jnp.concatenate([hist_ref[0], h, jnp.zeros((block - POOL_CARRY, x.shape[1]), F32)], axis=0).astype(BF16)
    hist_ref[0] = h[tile - POOL_CARRY:]
    for r0 in range(0, tile, block):
        which = jnp.where(t == 0, 0, 1) if r0 == 0 else 1
        e = ext[r0:r0 + 2 * block]
        pooled = [_dot(tab_ref[which, g], e[:, g * POOL_GC:(g + 1) * POOL_GC]) for g in range(len(POOL_WINDOWS))]
        out_ref[r0:r0 + block] = _pool_mix(pooled, w_ref, scale_ref, x[r0:r0 + block])


def _pool_prompt(x2d, gains, layer, pool_w, pool_scale, pool_layer, n_seq):
    rows, d = x2d.shape
    seq_rows = rows // n_seq
    tile = min(seq_rows, WIDE_ROW_TILE)
    tps = seq_rows // tile
    tabs = _pool_window_tables(min(tile, POOL_BLOCK))
    row_spec = pl.BlockSpec((tile, d), lambda b, t: (b * tps + t, 0))
    return pl.pallas_call(
        _pool_prompt_body,
        grid=(n_seq, tps),
        in_specs=[row_spec, _layer_resident(gains.shape, layer), _layer_resident(pool_w.shape, pool_layer),
                  _layer_resident(pool_scale.shape, pool_layer), _resident(tabs.shape)],
        out_specs=[row_spec, pl.BlockSpec((1, POOL_CARRY, d), lambda b, t: (b, 0, 0))],
        out_shape=[jax.ShapeDtypeStruct((rows, d), F32), jax.ShapeDtypeStruct((n_seq, POOL_CARRY, d), F32)],
        compiler_params=_params(("arbitrary", "arbitrary")),
        name="pool_prompt",
    )(x2d, gains, pool_w, pool_scale, tabs)


def _pool_sample_body(x_ref, g_ref, w_ref, scale_ref, buf_ref, out_ref, h_ref, *, n_seq, seq_rows, pos0):
    xt = _to_time_major(x_ref[...], n_seq, seq_rows)
    h = _rms(xt, g_ref[0])
    s = jnp.concatenate([buf_ref[...], h], axis=0)
    rows = n_seq * seq_rows
    first = POOL_BUF
    pooled = []
    for g, w in enumerate(POOL_WINDOWS):
        s = s[:, POOL_GC * (1 if g else 0):]
        step = (w // 2) * n_seq
        s = s[step:] + s[:-step]
        first -= w // 2
        win = s[first * n_seq: first * n_seq + rows, :POOL_GC]
        inv = [1.0 / min(pos0 + t + 1, w) for t in range(seq_rows)]
        if len(set(inv)) == 1:
            win = win * inv[0]
        else:
            win = jnp.concatenate([win[t * n_seq:(t + 1) * n_seq] * inv[t] for t in range(seq_rows)], axis=0)
        pooled.append(win - h[:, g * POOL_GC:(g + 1) * POOL_GC])
    out_ref[...] = _from_time_major(_pool_mix(pooled, w_ref, scale_ref, xt), n_seq, seq_rows)
    h_ref[...] = _from_time_major(h, n_seq, seq_rows)


def _pool_sample(x2d, gains, layer, pool_w, pool_scale, pool_layer, buf_t, n_seq, pos0):
    rows, d = x2d.shape
    seq_rows = rows // n_seq
    full = lambda s: pl.BlockSpec(s, lambda i: (0,) * len(s))
    return pl.pallas_call(
        functools.partial(_pool_sample_body, n_seq=n_seq, seq_rows=seq_rows, pos0=pos0),
        grid=(1,),
        in_specs=[full((rows, d)), _layer_resident(gains.shape, layer), _layer_resident(pool_w.shape, pool_layer),
                  _layer_resident(pool_scale.shape, pool_layer), full(buf_t.shape)],
        out_specs=[full((rows, d)), full((rows, d))],
        out_shape=[jax.ShapeDtypeStruct((rows, d), F32)] * 2,
        compiler_params=_params(("arbitrary",)),
        name="pool_sample",
    )(x2d, gains, pool_w, pool_scale, buf_t)


def _rope_tables(pos):
    inv = 1.0 / (ROPE_BASE ** (jnp.arange(0, RET_DK, 2, dtype=F32) / RET_DK))
    ang = pos[:, None] * inv[None, :]
    return jnp.cos(ang), jnp.sin(ang)


def kernel(x_prompt, x_sample, mem_prompt, cache_mem_k, cache_mem_v, state_ret, cache_pool, cache_ffn_conv, w_ret_in, ret_gn, w_ret_out, pool_w, pool_scale, norm_mem, w_xq, w_xk, w_xv, w_xo, w_up, conv_w, conv_b, w_down, norm_mix, norm_xattn, norm_ffn, norm_final):
    bp, lp, d = x_prompt.shape
    bs, ls, _ = x_sample.shape
    depth = w_up.shape[0]
    n_mem = mem_prompt.shape[1]

    w_in_b, w_out_b, pool_w_b, w_up_b, w_down_b = (
        w.astype(BF16) for w in (w_ret_in, w_ret_out, pool_w, w_up, w_down))
    cwb = jnp.concatenate([conv_w, conv_b[:, None, :]], axis=1)
    g_mix, g_x, g_ffn = _gain(norm_mix), _gain(norm_xattn), _gain(norm_ffn)
    gn_g, p_scale = _gain(ret_gn), _gain(pool_scale)

    mem_k2, mem_v2 = _mem_kv(mem_prompt.reshape(bp * n_mem, d), norm_mem, w_xk, w_xv)
    kv_p = tuple(a.reshape(depth * bp, n_mem * KV_ROW_STRIDE, LANES) for a in (mem_k2, mem_v2))
    kv_s = (_kv_rows(cache_mem_k), _kv_rows(cache_mem_v))
    states = state_ret.reshape((state_ret.shape[0] * bs,) + state_ret.shape[2:])

    cos_p, sin_p = _rope_tables(jnp.arange(lp, dtype=F32))
    cos_s, sin_s = _rope_tables(PAST_LEN + jnp.arange(ls, dtype=F32))
    cos_s, sin_s = jnp.tile(cos_s, (bs, 1)), jnp.tile(sin_s, (bs, 1))

    xp = x_prompt.reshape(bp * lp, d)
    xs = x_sample.reshape(bs * ls, d)
    ret_p, ret_s, pool_p, pool_s, conv_p, conv_s = [], [], [], [], [], []
    for i in range(depth):
        j = i // 2
        last = i == depth - 1
        if i % 2 == 0:
            q, k, v, gate = _ret_proj(xp, g_mix, i, cos_p, sin_p, w_in_b, j, bp, BF16)
            xp, s = _ret_prompt(q, k, v, gate, xp, gn_g, w_out_b, j, bp)
            ret_p.append(s)
            q, k, v, gate = _ret_proj(xs, g_mix, i, cos_s, sin_s, w_in_b, j, 1, F32)
            o, s = _ret_sample(q, k, v, states, j, bs, ls)
            xs = _ret_out(o, gate, xs, gn_g, w_out_b, j)
            ret_s.append(s)
        else:
            buf_t = jnp.swapaxes(cache_pool[j], 0, 1).reshape(POOL_BUF * bs, d)
            xs, hs = _pool_sample(xs, g_mix, i, pool_w_b, p_scale, j, buf_t, bs, PAST_LEN)
            pool_s.append(jnp.concatenate([cache_pool[j], hs.reshape(bs, ls, d)], axis=1)[:, -POOL_BUF:])

        if i % 2 == 0:
            xp, xs = _xattn(xp, xs, g_x, w_xq, w_xo, i, kv_p, kv_s, bp, bs)
        else:
            xp, xs, hist = _xattn(xp, xs, g_x, w_xq, w_xo, i, kv_p, kv_s, bp, bs,
                                  pool=(g_mix, i, pool_w_b, p_scale, j))
            pool_p.append(hist[:, POOL_CARRY - POOL_BUF:])

        xp, tail = _ffn_prompt(xp, g_ffn, norm_final, w_up_b, w_down_b, cwb, i, bp, last)
        conv_p.append(tail)
        buf = jnp.swapaxes(cache_ffn_conv[i], 0, 1).reshape((CONV_W - 1) * bs, 2 * D_FF)
        xs, ca, cg = _ffn_sample(xs, g_ffn, norm_final, w_up_b, w_down_b, cwb, i, buf, bs, last)
        tail = jnp.concatenate([ca, cg], axis=-1)
        conv_s.append(jnp.swapaxes(tail.reshape(CONV_W - 1, bs, 2 * D_FF), 0, 1))

    mk = _kv_unrows(mem_k2, (depth, bp), n_mem)
    mv = _kv_unrows(mem_v2, (depth, bp), n_mem)
    return (xp.reshape(bp, lp, d), xs.reshape(bs, ls, d), jnp.stack(ret_p), jnp.stack(ret_s).astype(state_ret.dtype),
            jnp.stack(pool_p), jnp.stack(pool_s), jnp.stack(conv_p), jnp.stack(conv_s), mk, mv)
```
